```python
import jax, jax.numpy as jnp
from jax import lax
import numpy as np

D_MODEL = 1024
BATCH = 8
SEQ = 2048
DEPTH = 2
DEC_BATCH = 32
DEC_SEQ = 1
PAST_LEN = 8192
PAGE_SIZE = 128

POOL_WINDOWS = (2, 4, 8, 16)
POOL_GROUPS = len(POOL_WINDOWS)
POOL_GDIM = D_MODEL // 16
POOL_W = POOL_GROUPS * POOL_GDIM
POOL_BUF = max(POOL_WINDOWS) - 1
HEAD_DIM = 64
ATT_HEADS = D_MODEL // 128
ATT_W = ATT_HEADS * HEAD_DIM
MOBA_BLOCK = 256
MOBA_TOPK = 3
MOBA_QCHUNK = 32
DN_HEADS = D_MODEL // 256
DN_DK = 64
DN_DV = 64
DN_W = DN_HEADS * DN_DV
DN_CHUNK = 64
CONV_W = 4
MIX_W = POOL_W + ATT_W + DN_W
D_FF = -(-8 * D_MODEL // (3 * 256)) * 256
N_MOD = 6
IN_SPLITS = (POOL_W, ATT_W, ATT_W, ATT_W, DN_W, DN_W, DN_W, DN_W, DN_HEADS, DN_HEADS)
IN_COLS = sum(IN_SPLITS)
NORM_EPS = 1e-6

kernel_name = 'hybrid_pool_moba_gdn_decode_step'


def rms_norm(x, w):
    xf = x.astype(jnp.float32)
    y = xf * lax.rsqrt(jnp.mean(xf * xf, axis=-1, keepdims=True) + NORM_EPS)
    return (y * w.astype(jnp.float32)).astype(x.dtype)


def l2_norm(x):
    return x * lax.rsqrt(jnp.sum(x * x, axis=-1, keepdims=True) + NORM_EPS)


def pool_mixer(u, buf, pos0, pool_w, pool_scale):
    B, L, _ = u.shape
    ext = jnp.concatenate([buf.astype(u.dtype), u], axis=1)
    cs = jnp.pad(jnp.cumsum(ext.astype(jnp.float32), axis=1), ((0, 0), (1, 0), (0, 0)))
    pos = pos0 + jnp.arange(L)
    top = cs[:, POOL_BUF + 1:POOL_BUF + 1 + L]
    uf = u.astype(jnp.float32)
    diffs = []
    for g, w in enumerate(POOL_WINDOWS):
        sl = slice(g * POOL_GDIM, (g + 1) * POOL_GDIM)
        wsum = top[..., sl] - cs[:, POOL_BUF + 1 - w:POOL_BUF + 1 - w + L, sl]
        cnt = jnp.minimum(w, pos + 1).astype(jnp.float32)[None, :, None]
        diffs.append(wsum / cnt - uf[..., sl])
    d = jnp.stack(diffs, axis=2)
    y = jnp.einsum('blgc,gcd->blgd', d, pool_w.astype(jnp.float32)).reshape(B, L, POOL_W)
    y = y * pool_scale.astype(jnp.float32)
    return y.astype(u.dtype), ext[:, ext.shape[1] - POOL_BUF:]


def moba_attention(q, k_all, v_all, q_start):
    B, Lq, H, hd = q.shape
    T = k_all.shape[1]
    nbk = -(-T // MOBA_BLOCK)
    padw = ((0, 0), (0, nbk * MOBA_BLOCK - T), (0, 0), (0, 0))
    kb = jnp.pad(k_all, padw).reshape(B, nbk, MOBA_BLOCK, H, hd).transpose(0, 3, 1, 2, 4)
    vb = jnp.pad(v_all, padw).reshape(B, nbk, MOBA_BLOCK, H, hd).transpose(0, 3, 1, 2, 4)
    kmean = jnp.mean(kb.astype(jnp.float32), axis=3)
    qh = q.transpose(0, 2, 1, 3)
    qpos = q_start + jnp.arange(Lq)
    qblk = qpos // MOBA_BLOCK
    gate = jnp.einsum('bhqd,bhnd->bhqn', qh.astype(jnp.float32), kmean)
    is_past = jnp.arange(nbk)[None, :] < qblk[:, None]
    gate = jnp.where(is_past, gate, -jnp.inf)
    n_top = min(MOBA_TOPK, nbk)
    _, sel = lax.top_k(gate, n_top)
    sel_ok = sel < qblk[:, None]
    own = jnp.broadcast_to(qblk[:, None], (B, H, Lq, 1)).astype(sel.dtype)
    blk_idx = jnp.concatenate([sel, own], axis=-1)
    blk_ok = jnp.concatenate([sel_ok, jnp.ones((B, H, Lq, 1), bool)], axis=-1)
    nsel = n_top + 1
    qc = MOBA_QCHUNK if Lq % MOBA_QCHUNK == 0 else Lq
    nc = Lq // qc

    def chunks(a):
        return jnp.moveaxis(a.reshape(B, H, nc, qc, *a.shape[3:]), 2, 0)

    bi = jnp.arange(B)[:, None, None, None]
    hi = jnp.arange(H)[None, :, None, None]
    offs = jnp.arange(MOBA_BLOCK)
    scale = hd ** -0.5

    def attend(args):
        q_c, idx_c, ok_c, pos_c = args
        kg = kb[bi, hi, idx_c]
        vg = vb[bi, hi, idx_c]
        kpos = idx_c[..., None] * MOBA_BLOCK + offs
        mask = ok_c[..., None] & (kpos <= pos_c[:, None, None])
        s = jnp.einsum('bhqd,bhqnkd->bhqnk', q_c, kg, preferred_element_type=jnp.float32) * scale
        s = jnp.where(mask, s, -jnp.inf).reshape(B, H, qc, nsel * MOBA_BLOCK)
        p = jax.nn.softmax(s, axis=-1).reshape(B, H, qc, nsel, MOBA_BLOCK)
        o = jnp.einsum('bhqnk,bhqnkd->bhqd', p, vg.astype(jnp.float32))
        return o.astype(q.dtype)

    o = lax.map(attend, (chunks(qh), chunks(blk_idx), chunks(blk_ok), qpos.reshape(nc, qc)))
    o = jnp.moveaxis(o, 0, 2).reshape(B, H, Lq, hd).transpose(0, 2, 1, 3)
    return o


def short_conv(x, buf, w):
    ext = jnp.concatenate([buf.astype(x.dtype), x], axis=1)
    y = lax.conv_general_dilated(ext, w.astype(x.dtype)[:, None, :], window_strides=(1,),
                                 padding='VALID', dimension_numbers=('NWC', 'WIO', 'NWC'),
                                 feature_group_count=x.shape[-1])
    return jax.nn.silu(y), ext[:, ext.shape[1] - (CONV_W - 1):]


def gated_delta_rule(q, k, v, g, beta, S0):
    B, L, H, dk = q.shape
    dv = v.shape[-1]
    C = DN_CHUNK
    pad = (-L) % C
    N = (L + pad) // C

    def prep(a):
        a = jnp.pad(a, ((0, 0), (0, pad)) + ((0, 0),) * (a.ndim - 2))
        a = jnp.moveaxis(a, 2, 1)
        return a.reshape(B, H, N, C, *a.shape[3:])

    q, k, v, g, beta = prep(q), prep(k), prep(v), prep(g), prep(beta)
    gcum = jnp.cumsum(g, axis=-1)
    incl = jnp.tril(jnp.ones((C, C), bool))
    strict = jnp.tril(jnp.ones((C, C), bool), -1)
    diff = gcum[..., :, None] - gcum[..., None, :]
    decay = jnp.where(incl, jnp.exp(jnp.where(incl, diff, 0.0)), 0.0)
    kbeta = k * beta[..., None]
    M = jnp.where(strict, jnp.einsum('bhnid,bhnjd->bhnij', kbeta, k) * decay, 0.0)
    A = M + jnp.eye(C, dtype=M.dtype)
    rhs = jnp.concatenate([v * beta[..., None], kbeta * jnp.exp(gcum)[..., None]], axis=-1)
    sol = lax.linalg.triangular_solve(A, rhs, left_side=True, lower=True, unit_diagonal=True)
    u_base, kcum = sol[..., :dv], sol[..., dv:]
    attn = jnp.einsum('bhnid,bhnjd->bhnij', q, k) * decay
    q_dec = q * jnp.exp(gcum)[..., None]
    k_dec = k * jnp.exp(gcum[..., -1:] - gcum)[..., None]
    g_tot = jnp.exp(gcum[..., -1])
    xs = tuple(jnp.moveaxis(a, 2, 0) for a in (u_base, kcum, attn, q_dec, k_dec, g_tot))

    def step(S, xc):
        u_c, kc_c, at_c, qd_c, kd_c, gt_c = xc
        v_new = u_c - jnp.einsum('bhck,bhkv->bhcv', kc_c, S)
        o = jnp.einsum('bhck,bhkv->bhcv', qd_c, S) + jnp.einsum('bhij,bhjv->bhiv', at_c, v_new)
        S = S * gt_c[..., None, None] + jnp.einsum('bhck,bhcv->bhkv', kd_c, v_new)
        return S, o

    S, o = lax.scan(step, S0, xs)
    o = jnp.moveaxis(o, 0, 2).reshape(B, H, N * C, dv)[:, :, :L].transpose(0, 2, 1, 3)
    return o, S


def decoder_layer(x, c, past_k, past_v, pool_buf, conv_buf, S0, lw):
    (norm1_w, ada_w, ada_b, w_in, pool_w, pool_scale, q_norm_w, k_norm_w, conv_w,
     a_log, dt_bias, dn_norm_w, w_out, norm2_w, w_gate, w_up, w_down) = lw
    B, L, _ = x.shape
    dt = x.dtype
    pos0 = past_k.shape[1]
    mod = (jax.nn.silu(c) @ ada_w + ada_b)[:, None, :]
    sh1, sc1, g1, sh2, sc2, g2 = jnp.split(mod, N_MOD, axis=-1)
    h = rms_norm(x, norm1_w) * (1 + sc1) + sh1
    proj = h @ w_in
    u_pool, q_a, k_a, v_a, q_d, k_d, v_d, z, b_raw, a_raw = jnp.split(
        proj, np.cumsum(IN_SPLITS)[:-1], axis=-1)
    y_pool, pool_new = pool_mixer(u_pool, pool_buf, pos0, pool_w, pool_scale)
    q_a = rms_norm(q_a.reshape(B, L, ATT_HEADS, HEAD_DIM), q_norm_w)
    k_a = rms_norm(k_a.reshape(B, L, ATT_HEADS, HEAD_DIM), k_norm_w)
    v_a = v_a.reshape(B, L, ATT_HEADS, HEAD_DIM)
    k_all = jnp.concatenate([past_k.astype(dt), k_a], axis=1)
    v_all = jnp.concatenate([past_v.astype(dt), v_a], axis=1)
    y_att = moba_attention(q_a, k_all, v_all, pos0).reshape(B, L, ATT_W)
    qkv, conv_new = short_conv(jnp.concatenate([q_d, k_d, v_d], axis=-1), conv_buf, conv_w)
    qkv = qkv.astype(jnp.float32)
    q_c = l2_norm(qkv[..., :DN_W].reshape(B, L, DN_HEADS, DN_DK)) * (DN_DK ** -0.5)
    k_c = l2_norm(qkv[..., DN_W:2 * DN_W].reshape(B, L, DN_HEADS, DN_DK))
    v_c = qkv[..., 2 * DN_W:].reshape(B, L, DN_HEADS, DN_DV)
    beta = jax.nn.sigmoid(b_raw.astype(jnp.float32))
    g = -jnp.exp(a_log.astype(jnp.float32)) * jax.nn.softplus(
        a_raw.astype(jnp.float32) + dt_bias.astype(jnp.float32))
    o_c, S_new = gated_delta_rule(q_c, k_c, v_c, g, beta, S0.astype(jnp.float32))
    y_dn = rms_norm(o_c, dn_norm_w) * jax.nn.silu(z.astype(jnp.float32)).reshape(B, L, DN_HEADS, DN_DV)
    y_dn = y_dn.reshape(B, L, DN_W).astype(dt)
    mix = jnp.concatenate([y_pool, y_att, y_dn], axis=-1)
    x = x + g1 * (mix @ w_out)
    h2 = rms_norm(x, norm2_w) * (1 + sc2) + sh2
    x = x + g2 * ((jax.nn.silu(h2 @ w_gate) * (h2 @ w_up)) @ w_down)
    return x, k_a, v_a, pool_new, conv_new, S_new


def setup_inputs(seed: int = 0) -> dict:
    key = jax.random.key(seed)
    ks = jax.random.split(key, 32)
    f32 = jnp.float32

    def nrm(k, shape, s=1.0):
        return jax.random.normal(k, shape, f32) * s

    n_pages = PAST_LEN // PAGE_SIZE
    n_used = DEC_BATCH * n_pages
    n_phys = n_used + n_used // 4
    page_table = jax.random.permutation(ks[0], n_phys)[:n_used].reshape(DEC_BATCH, n_pages).astype(jnp.int32)
    return {
        'x_prompt': nrm(ks[1], (BATCH, SEQ, D_MODEL)),
        'x_sample': nrm(ks[2], (DEC_BATCH, DEC_SEQ, D_MODEL)),
        'cache_k': nrm(ks[3], (DEPTH, n_phys, PAGE_SIZE, ATT_HEADS, HEAD_DIM)),
        'cache_v': nrm(ks[4], (DEPTH, n_phys, PAGE_SIZE, ATT_HEADS, HEAD_DIM)),
        'state_pool': nrm(ks[5], (DEPTH, DEC_BATCH, POOL_BUF, POOL_W)),
        'state_conv': nrm(ks[6], (DEPTH, DEC_BATCH, CONV_W - 1, 3 * DN_W)),
        'state_delta': nrm(ks[7], (DEPTH, DEC_BATCH, DN_HEADS, DN_DK, DN_DV), DN_DK ** -0.5),
        'page_table': page_table,
        'c_prompt': nrm(ks[8], (BATCH, D_MODEL)),
        'c_sample': nrm(ks[9], (DEC_BATCH, D_MODEL)),
        'norm1_w': 1.0 + nrm(ks[10], (DEPTH, D_MODEL), 0.02),
        'ada_w': nrm(ks[11], (DEPTH, D_MODEL, N_MOD * D_MODEL), 0.5 * D_MODEL ** -0.5),
        'ada_b': nrm(ks[12], (DEPTH, N_MOD * D_MODEL), 0.02),
        'w_in': nrm(ks[13], (DEPTH, D_MODEL, IN_COLS), D_MODEL ** -0.5),
        'pool_w': nrm(ks[14], (DEPTH, POOL_GROUPS, POOL_GDIM, POOL_GDIM), POOL_GDIM ** -0.5),
        'pool_scale': 1.0 + nrm(ks[15], (DEPTH, POOL_W), 0.1),
        'q_norm_w': 1.0 + nrm(ks[16], (DEPTH, HEAD_DIM), 0.02),
        'k_norm_w': 1.0 + nrm(ks[17], (DEPTH, HEAD_DIM), 0.02),
        'conv_w': nrm(ks[18], (DEPTH, CONV_W, 3 * DN_W), CONV_W ** -0.5),
        'a_log': jnp.log(jax.random.uniform(ks[19], (DEPTH, DN_HEADS), f32, 1.0, 16.0)),
        'dt_bias': nrm(ks[20], (DEPTH, DN_HEADS), 0.1),
        'dn_norm_w': 1.0 + nrm(ks[21], (DEPTH, DN_DV), 0.02),
        'w_out': nrm(ks[22], (DEPTH, MIX_W, D_MODEL), MIX_W ** -0.5),
        'norm2_w': 1.0 + nrm(ks[23], (DEPTH, D_MODEL), 0.02),
        'w_gate': nrm(ks[24], (DEPTH, D_MODEL, D_FF), D_MODEL ** -0.5),
        'w_up': nrm(ks[25], (DEPTH, D_MODEL, D_FF), D_MODEL ** -0.5),
        'w_down': nrm(ks[26], (DEPTH, D_FF, D_MODEL), D_FF ** -0.5),
    }


def reference(x_prompt, x_sample, cache_k, cache_v, state_pool, state_conv, state_delta, page_table,
              c_prompt, c_sample, norm1_w, ada_w, ada_b, w_in, pool_w, pool_scale, q_norm_w, k_norm_w,
              conv_w, a_log, dt_bias, dn_norm_w, w_out, norm2_w, w_gate, w_up, w_down):
    B = x_prompt.shape[0]
    DB = x_sample.shape[0]
    n_pages = page_table.shape[1]
    dt = x_prompt.dtype
    yp, ys = x_prompt, x_sample
    kp_l, vp_l, ks_l, vs_l = [], [], [], []
    pp_l, ps_l, cp_l, cs_l, sp_l, ss_l = [], [], [], [], [], []
    for l in range(DEPTH):
        lw = (norm1_w[l], ada_w[l], ada_b[l], w_in[l], pool_w[l], pool_scale[l], q_norm_w[l], k_norm_w[l],
              conv_w[l], a_log[l], dt_bias[l], dn_norm_w[l], w_out[l], norm2_w[l], w_gate[l], w_up[l], w_down[l])
        yp, kp, vp, pp, cp, sp = decoder_layer(
            yp, c_prompt,
            jnp.zeros((B, 0, ATT_HEADS, HEAD_DIM), dt), jnp.zeros((B, 0, ATT_HEADS, HEAD_DIM), dt),
            jnp.zeros((B, POOL_BUF, POOL_W), dt), jnp.zeros((B, CONV_W - 1, 3 * DN_W), dt),
            jnp.zeros((B, DN_HEADS, DN_DK, DN_DV), jnp.float32), lw)
        past_k = cache_k[l][page_table].reshape(DB, n_pages * PAGE_SIZE, ATT_HEADS, HEAD_DIM)
        past_v = cache_v[l][page_table].reshape(DB, n_pages * PAGE_SIZE, ATT_HEADS, HEAD_DIM)
        ys, ks_, vs_, ps, cs, ss = decoder_layer(
            ys, c_sample, past_k, past_v, state_pool[l], state_conv[l], state_delta[l], lw)
        kp_l.append(kp); vp_l.append(vp); ks_l.append(ks_); vs_l.append(vs_)
        pp_l.append(pp); ps_l.append(ps); cp_l.append(cp); cs_l.append(cs)
        sp_l.append(sp.astype(state_delta.dtype)); ss_l.append(ss.astype(state_delta.dtype))
    return (yp, ys,
            jnp.stack(kp_l), jnp.stack(vp_l), jnp.stack(ks_l), jnp.stack(vs_l),
            jnp.stack(pp_l), jnp.stack(ps_l), jnp.stack(cp_l), jnp.stack(cs_l),
            jnp.stack(sp_l), jnp.stack(ss_l))
```

```python
import functools

import jax
import jax.numpy as jnp
from jax import lax
from jax.experimental import pallas as pl
from jax.experimental.pallas import tpu as pltpu

F32 = jnp.float32
BF16 = jnp.bfloat16

D_MODEL = 1024
PAGE = 128
POOL_WINDOWS = (2, 4, 8, 16)
POOL_GDIM = 64
POOL_W = 256
POOL_BUF = 15
HEAD_DIM = 64
ATT_HEADS = 8
ATT_W = 512
BLK = 256
TOPK = 3
DN_HEADS = 4
DN_DK = 64
DN_W = 256
CHUNK = 64
CONV_W = 4
D_FF = 2816
N_MOD = 6
EPS = 1e-6
NEG = -1e30

C_U = (0, 256)
C_Q = (256, 768)
C_K = (768, 1280)
C_V = (1280, 1792)
C_D = (1792, 2560)
C_Z = (2560, 2816)
C_B = (2816, 3072)
C_A = (3072, 3328)
IN_EXT = 3328

VMEM_LIMIT = 56 * 1024 * 1024


def _cparams(*sem):
    return pltpu.CompilerParams(dimension_semantics=sem, vmem_limit_bytes=VMEM_LIMIT)


def _sigmoid(x):
    return 1.0 / (1.0 + jnp.exp(-x))


def _bdot(a, b):
    return jnp.dot(a.astype(BF16), b.astype(BF16), preferred_element_type=F32)


def _bdot_t(a, b):
    return lax.dot_general(a.astype(BF16), b.astype(BF16), (((1,), (1,)), ((), ())),
                           preferred_element_type=F32)


def _split3(a):
    hi = a.astype(BF16)
    r1 = a - hi.astype(F32)
    mid = r1.astype(BF16)
    lo = (r1 - mid.astype(F32)).astype(BF16)
    return hi, mid, lo


def _dot3_rhs_exact(a, b01):
    return sum(jnp.dot(p, b01, preferred_element_type=F32) for p in _split3(a))


def _dot3_lhs_exact(a01, b):
    return sum(jnp.dot(a01, p, preferred_element_type=F32) for p in _split3(b))


def _dot3_t_rhs_exact(a, b01):
    return sum(lax.dot_general(p, b01, (((1,), (1,)), ((), ())), preferred_element_type=F32)
               for p in _split3(a))


def _mod_kernel(c_ref, w_ref, b_ref, o_ref):
    c = c_ref[...]
    a = (c * _sigmoid(c)).astype(BF16)
    o_ref[...] = jnp.dot(a, w_ref[...].astype(BF16), preferred_element_type=F32) + b_ref[...]


def _modulation(c_all, ada_w, ada_b):
    depth, _, ncol = ada_w.shape
    nseq = c_all.shape[0]
    tn = 1536
    return pl.pallas_call(
        _mod_kernel,
        grid=(depth, ncol // tn),
        in_specs=[pl.BlockSpec((nseq, D_MODEL), lambda l, j: (0, 0)),
                  pl.BlockSpec((None, D_MODEL, tn), lambda l, j: (l, 0, j)),
                  pl.BlockSpec((None, 1, tn), lambda l, j: (l, 0, j))],
        out_specs=pl.BlockSpec((None, nseq, tn), lambda l, j: (l, 0, j)),
        out_shape=jax.ShapeDtypeStruct((depth, nseq, ncol), F32),
        compiler_params=_cparams("arbitrary", "arbitrary"),
        name="modulation",
    )(c_all, ada_w, ada_b.reshape(depth, 1, ncol))


def _mod_spec(mod, l, k, per_token):
    if per_token:
        return pl.BlockSpec((None, None, mod.shape[2], D_MODEL), lambda b, t: (l, k, 0, 0))
    return pl.BlockSpec((None, None, None, 1, D_MODEL), lambda b, t: (l, b, k, 0, 0))


def _in_kernel(x_ref, sh_ref, sc_ref, n1_ref, w_ref, qn_ref, kn_ref, ones_ref,
               u_ref, q_ref, k_ref, v_ref, d_ref, z_ref, b_ref, a_ref):
    x = x_ref[...]
    ms = jnp.mean(x * x, axis=-1, keepdims=True)
    h = (x * lax.rsqrt(ms + EPS) * n1_ref[...]) * (1.0 + sc_ref[...]) + sh_ref[...]
    hb = h.astype(BF16)

    def proj(c):
        return jnp.dot(hb, w_ref[:, c[0]:c[1]], preferred_element_type=F32)

    def head_norm(t, w4):
        ss = jnp.dot((t * t).astype(BF16), ones_ref[...], preferred_element_type=F32)
        return t * lax.rsqrt(ss * (1.0 / HEAD_DIM) + EPS) * w4

    u_ref[...] = proj(C_U)
    q_ref[...] = (head_norm(proj(C_Q), qn_ref[...]) * (HEAD_DIM ** -0.5)).astype(BF16)
    k_ref[...] = head_norm(proj(C_K), kn_ref[...])
    v_ref[...] = proj(C_V)
    d_ref[...] = proj(C_D)
    z_ref[...] = proj(C_Z)
    b_ref[...] = proj(C_B)
    a_ref[...] = proj(C_A)


def _in_proj(x3, mod, l, per_token, n1, w_ext, qn4, kn4, ones_att, tm):
    nb, length, _ = x3.shape
    widths = (POOL_W, ATT_W, ATT_W, ATT_W, 3 * DN_W, DN_W, DN_W, DN_W)
    dtypes = (F32, BF16, F32, F32, F32, F32, F32, F32)
    const = lambda b, t: (0, 0)
    tok = lambda w: pl.BlockSpec((None, tm, w), lambda b, t: (b, t, 0))
    return pl.pallas_call(
        _in_kernel,
        grid=(nb, length // tm),
        in_specs=[tok(D_MODEL), _mod_spec(mod, l, 0, per_token), _mod_spec(mod, l, 1, per_token),
                  pl.BlockSpec((1, D_MODEL), const),
                  pl.BlockSpec((D_MODEL, IN_EXT), const),
                  pl.BlockSpec((1, ATT_W), const), pl.BlockSpec((1, ATT_W), const),
                  pl.BlockSpec((ATT_W, ATT_W), const)],
        out_specs=[tok(w) for w in widths],
        out_shape=[jax.ShapeDtypeStruct((nb, length, w), dt) for w, dt in zip(widths, dtypes)],
        compiler_params=_cparams("arbitrary", "arbitrary"),
        name="in_proj",
    )(x3, mod, mod, n1, w_ext, qn4, kn4, ones_att)


def _pool_kernel(u_ref, buf_ref, pw_ref, ps_ref, y_ref, new_ref, ext_ref, *, length, pos0):
    ext_ref[pl.ds(1, POOL_BUF), :] = buf_ref[...]
    ext_ref[pl.ds(16, length), :] = u_ref[...]
    lane = lax.broadcasted_iota(jnp.int32, (1, POOL_W), 1)
    grp = jnp.right_shift(lane, 6)
    wl = jnp.where(grp == 0, 2, jnp.where(grp == 1, 4, jnp.where(grp == 2, 8, 16)))
    ch = min(length, 256)
    row = lax.broadcasted_iota(jnp.int32, (ch, 1), 0)
    for c in range(length // ch):
        base = 16 + c * ch
        cur = ext_ref[pl.ds(base, ch), :]
        acc = cur
        sums = {}
        for i in range(1, 16):
            acc = acc + ext_ref[pl.ds(base - i, ch), :]
            if i + 1 in POOL_WINDOWS:
                sums[i + 1] = acc
        wsum = jnp.where(grp == 0, sums[2], jnp.where(grp == 1, sums[4],
                                                      jnp.where(grp == 2, sums[8], sums[16])))
        cnt = jnp.minimum(wl, row + (pos0 + c * ch + 1)).astype(F32)
        d = wsum / cnt - cur
        y = jnp.dot(d.astype(BF16), pw_ref[...], preferred_element_type=F32) * ps_ref[...]
        y_ref[pl.ds(c * ch, ch), :] = y
    new_ref[...] = ext_ref[pl.ds(length + 1, POOL_BUF), :]


def _pool(u3, buf3, pw_bd, ps, pos0):
    nb, length, _ = u3.shape
    const = lambda b: (0, 0)
    return pl.pallas_call(
        functools.partial(_pool_kernel, length=length, pos0=pos0),
        grid=(nb,),
        in_specs=[pl.BlockSpec((None, length, POOL_W), lambda b: (b, 0, 0)),
                  pl.BlockSpec((None, POOL_BUF, POOL_W), lambda b: (b, 0, 0)),
                  pl.BlockSpec((POOL_W, POOL_W), const), pl.BlockSpec((1, POOL_W), const)],
        out_specs=[pl.BlockSpec((None, length, POOL_W), lambda b: (b, 0, 0)),
                   pl.BlockSpec((None, POOL_BUF, POOL_W), lambda b: (b, 0, 0))],
        out_shape=[jax.ShapeDtypeStruct((nb, length, POOL_W), F32),
                   jax.ShapeDtypeStruct((nb, POOL_BUF, POOL_W), F32)],
        scratch_shapes=[pltpu.VMEM((16 + length, POOL_W), F32)],
        compiler_params=_cparams("arbitrary"),
        name="pool_mixer",
    )(u3, buf3, pw_bd, ps)


def _attn_kernel(q_ref, k_ref, v_ref, o_ref, km_scr, vt_scr, ot_scr, bias_scr, *, length):
    nb = length // BLK
    lane = lax.broadcasted_iota(jnp.int32, (1, 128), 1)
    k2 = k_ref[...]
    kmean = jnp.sum(k2.reshape(nb, BLK, 128), axis=1) * (1.0 / BLK)
    vt = v_ref[...].T
    for j in range(nb):
        vt_scr[j] = vt[:, j * BLK:(j + 1) * BLK].astype(BF16)
    krow = lax.broadcasted_iota(jnp.int32, (BLK, BLK), 0)
    qcol = lax.broadcasted_iota(jnp.int32, (BLK, BLK), 1)
    causal = krow <= qcol
    blkrow = lax.broadcasted_iota(jnp.int32, (nb, BLK), 0)
    pad_rows = jnp.zeros((16 - nb, 128), F32) if nb < 16 else None

    for hh in range(2):
        hm = jnp.right_shift(lane, 6) == hh
        km_scr[...] = jnp.where(hm, k2, 0.0).astype(BF16).reshape(nb, BLK, 128)
        kmh = jnp.where(hm, kmean, 0.0)
        if pad_rows is not None:
            kmh = jnp.concatenate([kmh, pad_rows], axis=0)
        for i in range(nb):
            qi = q_ref[pl.ds(i * BLK, BLK), :]
            if i > 0:
                gate = _bdot_t(kmh, qi)[:nb]
                past = blkrow < i
                gate = jnp.where(past, gate, -jnp.inf)
                cnt = jnp.zeros((nb, BLK), jnp.int32)
                for jp in range(i):
                    gj = gate[jp:jp + 1, :]
                    beats = jnp.where(gj > gate, 1, jnp.where(gj == gate, (jp < blkrow).astype(jnp.int32), 0))
                    cnt = cnt + beats
                sel = jnp.where(past, cnt, TOPK) < TOPK
                bias_scr[pl.ds(0, nb), :] = jnp.where(sel, 0.0, NEG)
            s = _bdot_t(km_scr[i], qi)
            s = jnp.where(causal, s, NEG)
            m = jnp.max(s, axis=0, keepdims=True)
            p = jnp.exp(s - m)
            l = jnp.sum(p, axis=0, keepdims=True)
            acc = jnp.dot(vt_scr[i], p.astype(BF16), preferred_element_type=F32)
            if i > 0:
                def body(j, carry, qi=qi):
                    m, l, acc = carry
                    s = _bdot_t(km_scr[j], qi) + bias_scr[pl.ds(j, 1), :]
                    mn = jnp.maximum(m, jnp.max(s, axis=0, keepdims=True))
                    alpha = jnp.exp(m - mn)
                    p = jnp.exp(s - mn)
                    l = l * alpha + jnp.sum(p, axis=0, keepdims=True)
                    acc = acc * alpha + jnp.dot(vt_scr[j], p.astype(BF16), preferred_element_type=F32)
                    return mn, l, acc
                m, l, acc = lax.fori_loop(0, i, body, (m, l, acc))
            ot_scr[hh * 64:(hh + 1) * 64, i * BLK:(i + 1) * BLK] = acc[hh * 64:(hh + 1) * 64, :] / l
    o_ref[...] = ot_scr[...].T


def _attn_prompt(q3, k3, v3):
    nb, length, _ = q3.shape
    nblk = length // BLK
    spec = pl.BlockSpec((None, length, 128), lambda b, h: (b, 0, h))
    return pl.pallas_call(
        functools.partial(_attn_kernel, length=length),
        grid=(nb, ATT_W // 128),
        in_specs=[spec, spec, spec],
        out_specs=spec,
        out_shape=jax.ShapeDtypeStruct((nb, length, ATT_W), F32),
        scratch_shapes=[pltpu.VMEM((nblk, BLK, 128), BF16),
                        pltpu.VMEM((nblk, 128, BLK), BF16),
                        pltpu.VMEM((128, length), F32),
                        pltpu.VMEM((max(8, nblk), BLK), F32)],
        compiler_params=_cparams("arbitrary", "arbitrary"),
        name="moba_prompt",
    )(q3, k3, v3)


def _sa_kernel(pt_ref, q_ref, kn_ref, vn_ref, *refs, gpages, n_steps):
    kp = refs[:gpages]
    vp = refs[gpages:2 * gpages]
    o_ref = refs[2 * gpages]
    g_scr, m_scr, l_scr, o_scr = refs[2 * gpages + 1:]
    step = pl.program_id(1)
    q = q_ref[...]
    ones = jnp.ones((HEAD_DIM, 128), BF16)

    def scores(kpage):
        prod = (kpage * q[None]).reshape(PAGE * ATT_HEADS, HEAD_DIM).astype(BF16)
        return jnp.dot(prod, ones, preferred_element_type=F32).reshape(PAGE, ATT_HEADS, 128)

    for blk in range(gpages // 2):
        s0 = scores(kp[2 * blk][...])
        s1 = scores(kp[2 * blk + 1][...])
        gsum = jnp.sum(s0, axis=0) + jnp.sum(s1, axis=0)
        mb = jnp.maximum(jnp.max(s0, axis=0), jnp.max(s1, axis=0))
        p0 = jnp.exp(s0 - mb[None])
        p1 = jnp.exp(s1 - mb[None])
        lb = jnp.sum(p0, axis=0) + jnp.sum(p1, axis=0)
        ob = (jnp.sum(p0[:, :, :HEAD_DIM] * vp[2 * blk][...], axis=0)
              + jnp.sum(p1[:, :, :HEAD_DIM] * vp[2 * blk + 1][...], axis=0))
        idx = step * (gpages // 2) + blk
        g_scr[idx] = gsum
        m_scr[idx] = mb
        l_scr[idx] = lb
        o_scr[idx] = ob

    @pl.when(step == n_steps - 1)
    def _merge():
        nblk = n_steps * (gpages // 2)
        gates = g_scr[...]
        bidx = lax.broadcasted_iota(jnp.int32, gates.shape, 0)
        cnt = jnp.zeros(gates.shape, jnp.int32)
        for jp in range(nblk):
            gj = gates[jp][None]
            cnt = cnt + jnp.where(gj > gates, 1, jnp.where(gj == gates, (jp < bidx).astype(jnp.int32), 0))
        sel = cnt < TOPK
        sself = jnp.dot((q * kn_ref[...]).astype(BF16), ones, preferred_element_type=F32)
        mb = m_scr[...]
        mtot = jnp.maximum(jnp.max(jnp.where(sel, mb, -jnp.inf), axis=0), sself)
        w = jnp.where(sel, jnp.exp(jnp.where(sel, mb - mtot[None], 0.0)), 0.0)
        wself = jnp.exp(sself - mtot)
        denom = jnp.sum(w * l_scr[...], axis=0) + wself
        num = jnp.sum(w[:, :, :HEAD_DIM] * o_scr[...], axis=0) + wself[:, :HEAD_DIM] * vn_ref[...]
        o_ref[...] = num / denom[:, :HEAD_DIM]


def _attn_sample(q3, kn3, vn3, cache_k, cache_v, page_table, l, gpages=4):
    nb = q3.shape[0]
    n_pages = page_table.shape[1]
    n_steps = n_pages // gpages
    nblk = n_pages // 2
    tok = pl.BlockSpec((None, ATT_HEADS, HEAD_DIM), lambda b, s, pt: (b, 0, 0))

    def page_spec(i):
        return pl.BlockSpec((None, None, PAGE, ATT_HEADS, HEAD_DIM),
                            lambda b, s, pt: (l, pt[b, s * gpages + i], 0, 0, 0))

    grid_spec = pltpu.PrefetchScalarGridSpec(
        num_scalar_prefetch=1,
        grid=(nb, n_steps),
        in_specs=[tok, tok, tok] + [page_spec(i) for i in range(gpages)] * 2,
        out_specs=tok,
        scratch_shapes=[pltpu.VMEM((nblk, ATT_HEADS, 128), F32),
                        pltpu.VMEM((nblk, ATT_HEADS, 128), F32),
                        pltpu.VMEM((nblk, ATT_HEADS, 128), F32),
                        pltpu.VMEM((nblk, ATT_HEADS, HEAD_DIM), F32)],
    )
    return pl.pallas_call(
        functools.partial(_sa_kernel, gpages=gpages, n_steps=n_steps),
        grid_spec=grid_spec,
        out_shape=jax.ShapeDtypeStruct((nb, ATT_HEADS, HEAD_DIM), F32),
        compiler_params=_cparams("arbitrary", "arbitrary"),
        name="moba_sample",
    )(page_table, q3, kn3, vn3, *([cache_k] * gpages), *([cache_v] * gpages))


def _gdn_kernel(d_ref, z_ref, b_ref, a_ref, c0_ref, s0_ref, cw_ref, alog_ref, dtb_ref, dnw_ref,
                y_ref, cout_ref, sout_ref, ext_scr, s_scr, *, bt, n_chunks, l_valid):
    n = pl.program_id(1)
    C = CHUNK
    lane = lax.broadcasted_iota(jnp.int32, (1, DN_W), 1)
    lane_head = jnp.right_shift(lane, 6)
    lane_j = jnp.bitwise_and(lane, 63)
    row = lax.broadcasted_iota(jnp.int32, (C, 1), 0)
    incl4 = lane_j <= row
    strict4 = lane_j < row
    eye4 = (lane_j == row).astype(F32)
    ut4 = (row <= lane_j).astype(F32)
    r64 = lax.broadcasted_iota(jnp.int32, (C, C), 0)
    c64 = lax.broadcasted_iota(jnp.int32, (C, C), 1)
    l_incl = (c64 <= r64).astype(BF16)
    hmask = [lane_head == h for h in range(DN_HEADS)]
    r256 = lax.broadcasted_iota(jnp.int32, (DN_W, DN_W), 0)
    c256 = lax.broadcasted_iota(jnp.int32, (DN_W, DN_W), 1)
    bdmask = jnp.right_shift(r256, 6) == jnp.right_shift(c256, 6)
    ones_bd = bdmask.astype(BF16)
    pr = lax.broadcasted_iota(jnp.int32, (DN_DK, DN_W), 0)
    pc = lax.broadcasted_iota(jnp.int32, (DN_DK, DN_W), 1)
    place = [(pc == pr + 64 * h).astype(BF16) for h in range(DN_HEADS)]

    def stack_mask(a4):
        return jnp.concatenate([jnp.where(hmask[h], a4, 0.0).astype(BF16) for h in range(DN_HEADS)],
                               axis=0)

    @pl.when(n == 0)
    def _init():
        for bi in range(bt):
            ext_scr[bi, pl.ds(5, CONV_W - 1), :] = c0_ref[bi]
            s_scr[bi] = jnp.concatenate(
                [_dot3_rhs_exact(s0_ref[bi, h], place[h]) for h in range(DN_HEADS)], axis=0)

    last_valid = l_valid - (n_chunks - 1) * C
    masked = last_valid != C

    for bi in range(bt):
        ext_scr[bi, pl.ds(8, C), :] = d_ref[bi]
        cw = cw_ref[...]
        yc = (cw[0:1] * ext_scr[bi, pl.ds(5, C), :] + cw[1:2] * ext_scr[bi, pl.ds(6, C), :]
              + cw[2:3] * ext_scr[bi, pl.ds(7, C), :] + cw[3:4] * ext_scr[bi, pl.ds(8, C), :])
        act = yc * _sigmoid(yc)

        @pl.when(n == n_chunks - 1)
        def _conv_out(bi=bi):
            cout_ref[bi] = ext_scr[bi, pl.ds(last_valid + 5, CONV_W - 1), :]

        ext_scr[bi, pl.ds(5, CONV_W - 1), :] = ext_scr[bi, pl.ds(8 + C - (CONV_W - 1), CONV_W - 1), :]

        q_raw = act[:, 0:DN_W]
        k_raw = act[:, DN_W:2 * DN_W]
        v4 = act[:, 2 * DN_W:3 * DN_W]
        q4 = q_raw * lax.rsqrt(_bdot(q_raw * q_raw, ones_bd) + EPS) * (DN_DK ** -0.5)
        k4 = k_raw * lax.rsqrt(_bdot(k_raw * k_raw, ones_bd) + EPS)
        beta4 = _sigmoid(b_ref[bi])
        xg = a_ref[bi] + dtb_ref[...]
        g4 = -jnp.exp(alog_ref[...]) * (jnp.maximum(xg, 0.0) + jnp.log1p(jnp.exp(-jnp.abs(xg))))
        if masked:
            valid = (n * C + row) < l_valid
            q4 = jnp.where(valid, q4, 0.0)
            k4 = jnp.where(valid, k4, 0.0)
            v4 = jnp.where(valid, v4, 0.0)
            beta4 = jnp.where(valid, beta4, 0.0)
            g4 = jnp.where(valid, g4, 0.0)

        gi = _dot3_lhs_exact(l_incl, g4)
        gj = jnp.sum(g4 * ut4, axis=0, keepdims=True)
        decay4 = jnp.where(incl4, jnp.exp(jnp.where(incl4, gi - gj, 0.0)), 0.0)
        eg4 = jnp.exp(gi)
        glast = gi[C - 1:C, :]
        kfac = jnp.exp(glast - gi)
        gtot = jnp.exp(glast)

        kb4 = k4 * beta4
        mk = _bdot_t(jnp.concatenate([kb4, q4], axis=0), stack_mask(k4))
        m4 = jnp.where(strict4, mk[:C] * decay4, 0.0)
        attn4 = mk[C:] * decay4

        p = -m4
        t = eye4 + p
        p = _bdot(p, stack_mask(p))
        for _ in range(4):
            r = _bdot(jnp.concatenate([t, p], axis=0), stack_mask(p))
            t = t + r[:C]
            p = r[C:]
        t = t + _bdot(t, stack_mask(p))

        u4 = _bdot(t, stack_mask(v4 * beta4))
        kc4 = _bdot(t, stack_mask(kb4 * eg4))

        sbd = s_scr[bi]
        r = _bdot(jnp.concatenate([kc4, q4 * eg4], axis=0), sbd)
        vnew = u4 - r[:C]
        o4 = r[C:] + _bdot(attn4, stack_mask(vnew))
        kd4 = k4 * kfac
        upd = lax.dot_general(kd4.astype(BF16), vnew.astype(BF16), (((0,), (0,)), ((), ())),
                              preferred_element_type=F32)
        s_scr[bi] = sbd * gtot + jnp.where(bdmask, upd, 0.0)

        zz = z_ref[bi]
        sso = _bdot(o4 * o4, ones_bd)
        y_ref[bi] = o4 * lax.rsqrt(sso * (1.0 / DN_DK) + EPS) * dnw_ref[...] * (zz * _sigmoid(zz))

    @pl.when(n == n_chunks - 1)
    def _fin():
        for bi in range(bt):
            for h in range(DN_HEADS):
                sout_ref[bi, h] = _dot3_t_rhs_exact(s_scr[bi, h * 64:(h + 1) * 64, :], place[h])


def _gdn(d3, z3, b3, a3, conv0, s0, cw, alog4, dtb4, dnw4, l_valid, bt):
    nb, length, _ = d3.shape
    n_chunks = length // CHUNK
    tok = lambda w: pl.BlockSpec((bt, CHUNK, w), lambda i, n: (i, n, 0))
    const = lambda i, n: (0, 0)
    return pl.pallas_call(
        functools.partial(_gdn_kernel, bt=bt, n_chunks=n_chunks, l_valid=l_valid),
        grid=(nb // bt, n_chunks),
        in_specs=[tok(3 * DN_W), tok(DN_W), tok(DN_W), tok(DN_W),
                  pl.BlockSpec((bt, CONV_W - 1, 3 * DN_W), lambda i, n: (i, 0, 0)),
                  pl.BlockSpec((bt, DN_HEADS, DN_DK, DN_DK), lambda i, n: (i, 0, 0, 0)),
                  pl.BlockSpec((CONV_W, 3 * DN_W), const),
                  pl.BlockSpec((1, DN_W), const), pl.BlockSpec((1, DN_W), const),
                  pl.BlockSpec((1, DN_W), const)],
        out_specs=[tok(DN_W),
                   pl.BlockSpec((bt, CONV_W - 1, 3 * DN_W), lambda i, n: (i, 0, 0)),
                   pl.BlockSpec((bt, DN_HEADS, DN_DK, DN_DK), lambda i, n: (i, 0, 0, 0))],
        out_shape=[jax.ShapeDtypeStruct((nb, length, DN_W), F32),
                   jax.ShapeDtypeStruct((nb, CONV_W - 1, 3 * DN_W), F32),
                   jax.ShapeDtypeStruct((nb, DN_HEADS, DN_DK, DN_DK), F32)],
        scratch_shapes=[pltpu.VMEM((bt, 8 + CHUNK, 3 * DN_W), F32),
                        pltpu.VMEM((bt, DN_W, DN_W), F32)],
        compiler_params=_cparams("arbitrary", "arbitrary"),
        name="gated_delta",
    )(d3, z3, b3, a3, conv0, s0, cw, alog4, dtb4, dnw4)


FF_CHUNK = D_FF // 2


def _out_kernel(x_ref, yp_ref, ya_ref, yd_ref, g1_ref, sh2_ref, sc2_ref, g2_ref, n2_ref,
                wo_ref, wg_ref, wu_ref, wd_ref, o_ref):
    mix = jnp.concatenate([yp_ref[...], ya_ref[...], yd_ref[...]], axis=1).astype(BF16)
    x1 = x_ref[...] + g1_ref[...] * jnp.dot(mix, wo_ref[...], preferred_element_type=F32)
    ms = jnp.mean(x1 * x1, axis=-1, keepdims=True)
    h2 = ((x1 * lax.rsqrt(ms + EPS) * n2_ref[...]) * (1.0 + sc2_ref[...]) + sh2_ref[...]).astype(BF16)
    acc = None
    for c in range(D_FF // FF_CHUNK):
        c0 = c * FF_CHUNK
        gt = jnp.dot(h2, wg_ref[:, c0:c0 + FF_CHUNK], preferred_element_type=F32)
        up = jnp.dot(h2, wu_ref[:, c0:c0 + FF_CHUNK], preferred_element_type=F32)
        act = (gt * _sigmoid(gt) * up).astype(BF16)
        part = jnp.dot(act, wd_ref[c0:c0 + FF_CHUNK, :], preferred_element_type=F32)
        acc = part if acc is None else acc + part
    o_ref[...] = x1 + g2_ref[...] * acc


def _out_ffn(x3, yp, ya, yd, mod, l, per_token, n2, wo, wg, wu, wd, tm):
    nb, length, _ = x3.shape
    tok = lambda w: pl.BlockSpec((None, tm, w), lambda b, t: (b, t, 0))
    const = lambda b, t: (0, 0)
    single = pl.Buffered(1)
    return pl.pallas_call(
        _out_kernel,
        grid=(nb, length // tm),
        in_specs=[tok(D_MODEL), tok(POOL_W), tok(ATT_W), tok(DN_W),
                  _mod_spec(mod, l, 2, per_token), _mod_spec(mod, l, 3, per_token),
                  _mod_spec(mod, l, 4, per_token), _mod_spec(mod, l, 5, per_token),
                  pl.BlockSpec((1, D_MODEL), const),
                  pl.BlockSpec((D_MODEL, D_MODEL), const, pipeline_mode=single),
                  pl.BlockSpec((D_MODEL, D_FF), const, pipeline_mode=single),
                  pl.BlockSpec((D_MODEL, D_FF), const, pipeline_mode=single),
                  pl.BlockSpec((D_FF, D_MODEL), const, pipeline_mode=single)],
        out_specs=tok(D_MODEL),
        out_shape=jax.ShapeDtypeStruct((nb, length, D_MODEL), F32),
        compiler_params=_cparams("arbitrary", "arbitrary"),
        name="out_ffn",
    )(x3, yp, ya, yd, mod, mod, mod, mod, n2, wo, wg, wu, wd)


def kernel(x_prompt, x_sample, cache_k, cache_v, state_pool, state_conv, state_delta, page_table,
           c_prompt, c_sample, norm1_w, ada_w, ada_b, w_in, pool_w, pool_scale, q_norm_w, k_norm_w,
           conv_w, a_log, dt_bias, dn_norm_w, w_out, norm2_w, w_gate, w_up, w_down):
    depth = w_in.shape[0]
    nbp, seq, _ = x_prompt.shape
    nbs = x_sample.shape[0]
    past_len = page_table.shape[1] * PAGE

    n_main = C_Z[1]
    w_ext = jnp.concatenate(
        [w_in[:, :, :n_main],
         jnp.repeat(w_in[:, :, n_main:n_main + DN_HEADS], DN_DK, axis=-1),
         jnp.repeat(w_in[:, :, n_main + DN_HEADS:], DN_DK, axis=-1)], axis=-1).astype(BF16)
    wo_b, wg_b, wu_b, wd_b = (w.astype(BF16) for w in (w_out, w_gate, w_up, w_down))
    eye_g = jnp.eye(len(POOL_WINDOWS), dtype=F32)
    pw_bd = (eye_g[None, :, None, :, None] * pool_w[:, :, :, None, :]).reshape(depth, POOL_W, POOL_W).astype(BF16)
    ones_att = jnp.kron(jnp.eye(ATT_HEADS, dtype=F32), jnp.ones((HEAD_DIM, HEAD_DIM), F32)).astype(BF16)
    qn4 = jnp.tile(q_norm_w, (1, ATT_HEADS)).reshape(depth, 1, ATT_W)
    kn4 = jnp.tile(k_norm_w, (1, ATT_HEADS)).reshape(depth, 1, ATT_W)
    dnw4 = jnp.tile(dn_norm_w, (1, DN_HEADS)).reshape(depth, 1, DN_W)
    alog4 = jnp.repeat(a_log, DN_DK, axis=-1).reshape(depth, 1, DN_W)
    dtb4 = jnp.repeat(dt_bias, DN_DK, axis=-1).reshape(depth, 1, DN_W)

    mod = _modulation(jnp.concatenate([c_prompt, c_sample], axis=0), ada_w, ada_b)
    mod_p = mod[:, :nbp].reshape(depth, nbp, N_MOD, 1, D_MODEL)
    mod_s = mod[:, nbp:].reshape(depth, nbs, N_MOD, D_MODEL).transpose(0, 2, 1, 3)

    xp = x_prompt
    xs = x_sample.reshape(1, nbs, D_MODEL)
    zero_pool = jnp.zeros((nbp, POOL_BUF, POOL_W), F32)
    zero_pool_s = jnp.zeros((nbs, POOL_BUF, POOL_W), F32)
    zero_conv = jnp.zeros((nbp, CONV_W - 1, 3 * DN_W), F32)
    zero_state = jnp.zeros((nbp, DN_HEADS, DN_DK, DN_DK), F32)
    pad_rows = lambda a: jnp.pad(a.reshape(nbs, 1, a.shape[-1]), ((0, 0), (0, CHUNK - 1), (0, 0)))

    outs = {k: [] for k in ("kp", "vp", "ks", "vs", "pp", "ps", "cp", "cs", "sp", "ss")}
    for l in range(depth):
        n1 = norm1_w[l].reshape(1, D_MODEL)
        n2 = norm2_w[l].reshape(1, D_MODEL)
        ps = pool_scale[l].reshape(1, POOL_W)

        u, q, k, v, d, z, b4, a4 = _in_proj(xp, mod_p, l, False, n1, w_ext[l], qn4[l], kn4[l], ones_att, 512)
        y_pool, pool_new = _pool(u, zero_pool, pw_bd[l], ps, 0)
        y_att = _attn_prompt(q, k, v)
        y_dn, conv_new, s_new = _gdn(d, z, b4, a4, zero_conv, zero_state, conv_w[l], alog4[l], dtb4[l],
                                     dnw4[l], seq, 4)
        xp = _out_ffn(xp, y_pool, y_att, y_dn, mod_p, l, False, n2, wo_b[l], wg_b[l], wu_b[l], wd_b[l], 512)
        outs["kp"].append(k.reshape(nbp, seq, ATT_HEADS, HEAD_DIM))
        outs["vp"].append(v.reshape(nbp, seq, ATT_HEADS, HEAD_DIM))
        outs["pp"].append(pool_new)
        outs["cp"].append(conv_new)
        outs["sp"].append(s_new)

        u, q, k, v, d, z, b4, a4 = _in_proj(xs, mod_s, l, True, n1, w_ext[l], qn4[l], kn4[l], ones_att, nbs)
        ext = jnp.concatenate([state_pool[l], u.reshape(nbs, 1, POOL_W)], axis=1)
        y_pool16, pool_new = _pool(ext, zero_pool_s, pw_bd[l], ps, past_len - POOL_BUF)
        y_pool = y_pool16[:, POOL_BUF:].reshape(1, nbs, POOL_W)
        o_att = _attn_sample(q.reshape(nbs, ATT_HEADS, HEAD_DIM).astype(F32),
                             k.reshape(nbs, ATT_HEADS, HEAD_DIM), v.reshape(nbs, ATT_HEADS, HEAD_DIM),
                             cache_k, cache_v, page_table, l)
        y_att = o_att.reshape(1, nbs, ATT_W)
        y_dn64, conv_new, s_new = _gdn(pad_rows(d), pad_rows(z), pad_rows(b4), pad_rows(a4),
                                       state_conv[l], state_delta[l], conv_w[l], alog4[l], dtb4[l],
                                       dnw4[l], 1, 8)
        y_dn = y_dn64[:, 0].reshape(1, nbs, DN_W)
        xs = _out_ffn(xs, y_pool, y_att, y_dn, mod_s, l, True, n2, wo_b[l], wg_b[l], wu_b[l], wd_b[l], nbs)
        outs["ks"].append(k.reshape(nbs, 1, ATT_HEADS, HEAD_DIM))
        outs["vs"].append(v.reshape(nbs, 1, ATT_HEADS, HEAD_DIM))
        outs["ps"].append(pool_new)
        outs["cs"].append(conv_new)
        outs["ss"].append(s_new)

    st = lambda name: jnp.stack(outs[name])
    return (xp, xs.reshape(nbs, 1, D_MODEL), st("kp"), st("vp"), st("ks"), st("vs"),
            st("pp"), st("ps"), st("cp"), st("cs"), st("sp"), st("ss"))
```

```python
import functools

import numpy as np
import jax
import jax.numpy as jnp
from jax import lax
from jax.experimental import pallas as pl
from jax.experimental.pallas import tpu as pltpu

F32 = jnp.float32
BF16 = jnp.bfloat16

D_MODEL = 1024
PAGE = 128
POOL_WINDOWS = (2, 4, 8, 16)
POOL_GDIM = 64
POOL_W = 256
POOL_BUF = 15
HEAD_DIM = 64
ATT_HEADS = 8
ATT_W = 512
BLK = 256
TOPK = 3
DN_HEADS = 4
DN_DK = 64
DN_W = 256
CHUNK = 64
CONV_W = 4
D_FF = 2816
N_MOD = 6
EPS = 1e-6
NEG = -1e30

C_U = (0, 256)
C_Q = (256, 768)
C_K = (768, 1280)
C_V = (1280, 1792)
C_D = (1792, 2560)
C_Z = (2560, 2816)
C_B = (2816, 3072)
C_A = (3072, 3328)
IN_EXT = 3328

VMEM_LIMIT = 56 * 1024 * 1024


def _cparams(*sem):
    return pltpu.CompilerParams(dimension_semantics=sem, vmem_limit_bytes=VMEM_LIMIT)


def _sigmoid(x):
    return 1.0 / (1.0 + jnp.exp(-x))


def _bdot(a, b):
    return jnp.dot(a.astype(BF16), b.astype(BF16), preferred_element_type=F32)


def _bdot_t(a, b):
    return lax.dot_general(a.astype(BF16), b.astype(BF16), (((1,), (1,)), ((), ())),
                           preferred_element_type=F32)


def _split3(a):
    hi = a.astype(BF16)
    r1 = a - hi.astype(F32)
    mid = r1.astype(BF16)
    lo = (r1 - mid.astype(F32)).astype(BF16)
    return hi, mid, lo


def _dot3_rhs_exact(a, b01):
    return sum(jnp.dot(p, b01, preferred_element_type=F32) for p in _split3(a))


def _dot3_lhs_exact(a01, b):
    return sum(jnp.dot(a01, p, preferred_element_type=F32) for p in _split3(b))


def _dot3_t_rhs_exact(a, b01):
    return sum(lax.dot_general(p, b01, (((1,), (1,)), ((), ())), preferred_element_type=F32)
               for p in _split3(a))


def _dot3_t_lhs_exact(a01, b):
    return sum(lax.dot_general(a01, p, (((1,), (1,)), ((), ())), preferred_element_type=F32)
               for p in _split3(b))


def _mod_kernel(c_ref, w_ref, b_ref, o_ref):
    c = c_ref[...]
    a = (c * _sigmoid(c)).astype(BF16)
    o_ref[...] = jnp.dot(a, w_ref[...].astype(BF16), preferred_element_type=F32) + b_ref[...]


def _modulation(c_all, ada_w, ada_b):
    depth, _, ncol = ada_w.shape
    nseq = c_all.shape[0]
    tn = 1536
    return pl.pallas_call(
        _mod_kernel,
        grid=(depth, ncol // tn),
        in_specs=[pl.BlockSpec((nseq, D_MODEL), lambda l, j: (0, 0)),
                  pl.BlockSpec((None, D_MODEL, tn), lambda l, j: (l, 0, j)),
                  pl.BlockSpec((None, 1, tn), lambda l, j: (l, 0, j))],
        out_specs=pl.BlockSpec((None, nseq, tn), lambda l, j: (l, 0, j)),
        out_shape=jax.ShapeDtypeStruct((depth, nseq, ncol), F32),
        compiler_params=_cparams("arbitrary", "arbitrary"),
        name="modulation",
    )(c_all, ada_w, ada_b.reshape(depth, 1, ncol))


def _mod_spec(mod, l, k, per_token):
    if per_token:
        return pl.BlockSpec((None, None, mod.shape[2], D_MODEL), lambda b, t: (l, k, 0, 0))
    return pl.BlockSpec((None, None, None, 1, D_MODEL), lambda b, t: (l, b, k, 0, 0))


def _in_kernel(x_ref, sh_ref, sc_ref, n1_ref, w_ref, qn_ref, kn_ref, ones_ref,
               u_ref, q_ref, k_ref, v_ref, d_ref, z_ref, b_ref, a_ref):
    x = x_ref[...]
    ms = jnp.mean(x * x, axis=-1, keepdims=True)
    h = (x * lax.rsqrt(ms + EPS) * n1_ref[...]) * (1.0 + sc_ref[...]) + sh_ref[...]
    hb = h.astype(BF16)

    def proj(c):
        return jnp.dot(hb, w_ref[:, c[0]:c[1]], preferred_element_type=F32)

    def head_norm(t, w4):
        ss = jnp.dot((t * t).astype(BF16), ones_ref[...], preferred_element_type=F32)
        return t * lax.rsqrt(ss * (1.0 / HEAD_DIM) + EPS) * w4

    u_ref[...] = proj(C_U)
    q_ref[...] = (head_norm(proj(C_Q), qn_ref[...]) * (HEAD_DIM ** -0.5)).astype(BF16)
    k_ref[...] = head_norm(proj(C_K), kn_ref[...])
    v_ref[...] = proj(C_V)
    d_ref[...] = proj(C_D)
    z_ref[...] = proj(C_Z)
    b_ref[...] = proj(C_B)
    a_ref[...] = proj(C_A)


def _in_proj(x3, mod, l, per_token, n1, w_ext, qn4, kn4, ones_att, tm):
    nb, length, _ = x3.shape
    widths = (POOL_W, ATT_W, ATT_W, ATT_W, 3 * DN_W, DN_W, DN_W, DN_W)
    dtypes = (F32, BF16, F32, F32, F32, F32, F32, F32)
    const = lambda b, t: (0, 0)
    tok = lambda w: pl.BlockSpec((None, tm, w), lambda b, t: (b, t, 0))
    return pl.pallas_call(
        _in_kernel,
        grid=(nb, length // tm),
        in_specs=[tok(D_MODEL), _mod_spec(mod, l, 0, per_token), _mod_spec(mod, l, 1, per_token),
                  pl.BlockSpec((1, D_MODEL), const),
                  pl.BlockSpec((D_MODEL, IN_EXT), const),
                  pl.BlockSpec((1, ATT_W), const), pl.BlockSpec((1, ATT_W), const),
                  pl.BlockSpec((ATT_W, ATT_W), const)],
        out_specs=[tok(w) for w in widths],
        out_shape=[jax.ShapeDtypeStruct((nb, length, w), dt) for w, dt in zip(widths, dtypes)],
        compiler_params=_cparams("arbitrary", "arbitrary"),
        name="in_proj",
    )(x3, mod, mod, n1, w_ext, qn4, kn4, ones_att)


def _pool_kernel(u_ref, buf_ref, pw_ref, ps_ref, y_ref, new_ref, ext_ref, *, length, pos0):
    ext_ref[pl.ds(1, POOL_BUF), :] = buf_ref[...]
    ext_ref[pl.ds(16, length), :] = u_ref[...]
    lane = lax.broadcasted_iota(jnp.int32, (1, POOL_W), 1)
    grp = jnp.right_shift(lane, 6)
    wl = jnp.where(grp == 0, 2, jnp.where(grp == 1, 4, jnp.where(grp == 2, 8, 16)))
    ch = min(length, 256)
    row = lax.broadcasted_iota(jnp.int32, (ch, 1), 0)
    for c in range(length // ch):
        base = 16 + c * ch
        cur = ext_ref[pl.ds(base, ch), :]
        acc = cur
        sums = {}
        for i in range(1, 16):
            acc = acc + ext_ref[pl.ds(base - i, ch), :]
            if i + 1 in POOL_WINDOWS:
                sums[i + 1] = acc
        wsum = jnp.where(grp == 0, sums[2], jnp.where(grp == 1, sums[4],
                                                      jnp.where(grp == 2, sums[8], sums[16])))
        cnt = jnp.minimum(wl, row + (pos0 + c * ch + 1)).astype(F32)
        d = wsum / cnt - cur
        y = jnp.dot(d.astype(BF16), pw_ref[...], preferred_element_type=F32) * ps_ref[...]
        y_ref[pl.ds(c * ch, ch), :] = y
    new_ref[...] = ext_ref[pl.ds(length + 1, POOL_BUF), :]


def _pool(u3, buf3, pw_bd, ps, pos0):
    nb, length, _ = u3.shape
    const = lambda b: (0, 0)
    return pl.pallas_call(
        functools.partial(_pool_kernel, length=length, pos0=pos0),
        grid=(nb,),
        in_specs=[pl.BlockSpec((None, length, POOL_W), lambda b: (b, 0, 0)),
                  pl.BlockSpec((None, POOL_BUF, POOL_W), lambda b: (b, 0, 0)),
                  pl.BlockSpec((POOL_W, POOL_W), const), pl.BlockSpec((1, POOL_W), const)],
        out_specs=[pl.BlockSpec((None, length, POOL_W), lambda b: (b, 0, 0)),
                   pl.BlockSpec((None, POOL_BUF, POOL_W), lambda b: (b, 0, 0))],
        out_shape=[jax.ShapeDtypeStruct((nb, length, POOL_W), F32),
                   jax.ShapeDtypeStruct((nb, POOL_BUF, POOL_W), F32)],
        scratch_shapes=[pltpu.VMEM((16 + length, POOL_W), F32)],
        compiler_params=_cparams("arbitrary"),
        name="pool_mixer",
    )(u3, buf3, pw_bd, ps)


def _attn_kernel(q_ref, k_ref, v_ref, o_ref, km_scr, vt_scr, ot_scr, s_scr, *, length):
    nb = length // BLK
    lane = lax.broadcasted_iota(jnp.int32, (1, 128), 1)
    k2 = k_ref[...]
    kmean = jnp.sum(k2.reshape(nb, BLK, 128), axis=1) * (1.0 / BLK)
    vt = v_ref[...].T
    for j in range(nb):
        vt_scr[j] = vt[:, j * BLK:(j + 1) * BLK].astype(BF16)
    krow = lax.broadcasted_iota(jnp.int32, (BLK, BLK), 0)
    qcol = lax.broadcasted_iota(jnp.int32, (BLK, BLK), 1)
    causal = krow <= qcol
    blkrow = lax.broadcasted_iota(jnp.int32, (nb, BLK), 0)
    pad_rows = jnp.zeros((16 - nb, 128), F32) if nb < 16 else None

    def fold8(t):
        return t.reshape(BLK // 8, 8, BLK)

    for hh in range(2):
        hm = jnp.right_shift(lane, 6) == hh
        km_scr[...] = jnp.where(hm, k2, 0.0).astype(BF16).reshape(nb, BLK, 128)
        kmh = jnp.where(hm, kmean, 0.0)
        if pad_rows is not None:
            kmh = jnp.concatenate([kmh, pad_rows], axis=0)
        for i in range(nb):
            qi = q_ref[pl.ds(i * BLK, BLK), :]
            bias = None
            if i > 0:
                gate = _bdot_t(kmh, qi)[:nb]
                past = blkrow < i
                gate = jnp.where(past, gate, -jnp.inf)
                cnt = jnp.zeros((nb, BLK), jnp.int32)
                for jp in range(i):
                    gj = gate[jp:jp + 1, :]
                    cnt = cnt + jnp.where(gj > gate, 1,
                                          jnp.where(gj == gate, (jp < blkrow).astype(jnp.int32), 0))
                sel = jnp.where(past, cnt, TOPK) < TOPK
                bias = jnp.where(sel, 0.0, NEG)
            slot = i % 2
            m8 = None
            for j in range(i + 1):
                s = _bdot_t(km_scr[j], qi)
                s = jnp.where(causal, s, NEG) if j == i else s + bias[j:j + 1, :]
                s_scr[slot, j] = s
                t = jnp.max(fold8(s), axis=0)
                m8 = t if m8 is None else jnp.maximum(m8, t)
            m = jnp.max(m8, axis=0, keepdims=True)
            l8 = None
            acc = None
            for j in range(i + 1):
                p = jnp.exp(s_scr[slot, j] - m)
                t = jnp.sum(fold8(p), axis=0)
                l8 = t if l8 is None else l8 + t
                pv = jnp.dot(vt_scr[j], p.astype(BF16), preferred_element_type=F32)
                acc = pv if acc is None else acc + pv
            l = jnp.sum(l8, axis=0, keepdims=True)
            ot_scr[hh * 64:(hh + 1) * 64, i * BLK:(i + 1) * BLK] = acc[hh * 64:(hh + 1) * 64, :] / l
    o_ref[...] = ot_scr[...].T


def _attn_prompt(q3, k3, v3):
    nb, length, _ = q3.shape
    nblk = length // BLK
    spec = pl.BlockSpec((None, length, 128), lambda b, h: (b, 0, h))
    return pl.pallas_call(
        functools.partial(_attn_kernel, length=length),
        grid=(nb, ATT_W // 128),
        in_specs=[spec, spec, spec],
        out_specs=spec,
        out_shape=jax.ShapeDtypeStruct((nb, length, ATT_W), F32),
        scratch_shapes=[pltpu.VMEM((nblk, BLK, 128), BF16),
                        pltpu.VMEM((nblk, 128, BLK), BF16),
                        pltpu.VMEM((128, length), F32),
                        pltpu.VMEM((2, nblk, BLK, BLK), F32)],
        compiler_params=_cparams("arbitrary", "arbitrary"),
        name="moba_prompt",
    )(q3, k3, v3)


def _sa_kernel(pt_ref, qbc_ref, q_ref, kn_ref, vn_ref, *refs, gpages, n_steps):
    kp = refs[:gpages]
    vp = refs[gpages:2 * gpages]
    o_ref = refs[2 * gpages]
    g_scr, m_scr, l_scr, o_scr, s_scr, w_scr = refs[2 * gpages + 1:]
    step = pl.program_id(1)
    nblk_step = gpages // 2
    full = (ATT_HEADS, PAGE)

    for pg in range(gpages):
        for h in range(ATT_HEADS):
            s_scr[pg, pl.ds(h, 1), :] = jnp.sum(kp[pg][h] * qbc_ref[h], axis=0, keepdims=True)

    for blk in range(nblk_step):
        s0 = s_scr[2 * blk]
        s1 = s_scr[2 * blk + 1]
        gsum = jnp.sum(s0, axis=-1, keepdims=True) + jnp.sum(s1, axis=-1, keepdims=True)
        mb = jnp.maximum(jnp.max(s0, axis=-1, keepdims=True), jnp.max(s1, axis=-1, keepdims=True))
        p0 = jnp.exp(s0 - mb)
        p1 = jnp.exp(s1 - mb)
        lb = jnp.sum(p0, axis=-1, keepdims=True) + jnp.sum(p1, axis=-1, keepdims=True)
        s_scr[2 * blk] = p0
        s_scr[2 * blk + 1] = p1
        idx = step * nblk_step + blk
        g_scr[idx] = jnp.broadcast_to(gsum, full)
        m_scr[idx] = jnp.broadcast_to(mb, full)
        l_scr[idx] = jnp.broadcast_to(lb, full)
        for h in range(ATT_HEADS):
            o_scr[idx, h] = (vp[2 * blk][h] * s_scr[2 * blk, pl.ds(h, 1), :]
                             + vp[2 * blk + 1][h] * s_scr[2 * blk + 1, pl.ds(h, 1), :])

    @pl.when(step == n_steps - 1)
    def _merge():
        nblk = n_steps * nblk_step
        gates = g_scr[...]
        bidx = lax.broadcasted_iota(jnp.int32, gates.shape, 0)
        cnt = jnp.zeros(gates.shape, jnp.int32)
        for jp in range(nblk):
            gj = gates[jp][None]
            cnt = cnt + jnp.where(gj > gates, 1, jnp.where(gj == gates, (jp < bidx).astype(jnp.int32), 0))
        sel = cnt < TOPK
        q = q_ref[...]
        sself = jnp.broadcast_to(jnp.sum(q * kn_ref[...], axis=-1, keepdims=True), full)
        mb = m_scr[...]
        mtot = jnp.maximum(jnp.max(jnp.where(sel, mb, -jnp.inf), axis=0), sself)
        w = jnp.where(sel, jnp.exp(jnp.where(sel, mb - mtot[None], 0.0)), 0.0)
        wself = jnp.exp(sself - mtot)
        denom = jnp.sum(w * l_scr[...], axis=0) + wself
        w_scr[...] = w
        ones8 = jnp.ones((8, PAGE), BF16)
        sub = lax.broadcasted_iota(jnp.int32, (ATT_HEADS, HEAD_DIM), 0)
        o_acc = jnp.zeros((ATT_HEADS, HEAD_DIM), F32)
        for h in range(ATT_HEADS):
            tot = w_scr[0, pl.ds(h, 1), :] * o_scr[0, h]
            for b in range(1, nblk):
                tot = tot + w_scr[b, pl.ds(h, 1), :] * o_scr[b, h]
            r = _dot3_t_lhs_exact(ones8, tot)
            o_acc = jnp.where(sub == h, r, o_acc)
        o_ref[...] = (o_acc + wself[:, :HEAD_DIM] * vn_ref[...]) / denom[:, :HEAD_DIM]


def _attn_sample(q3, kn3, vn3, cache_kt, cache_vt, page_table, l, gpages=8):
    nb = q3.shape[0]
    n_pages = page_table.shape[1]
    n_steps = n_pages // gpages
    nblk = n_pages // 2
    qbc = jnp.broadcast_to(q3[..., None], (nb, ATT_HEADS, HEAD_DIM, PAGE))
    tok = lambda: pl.BlockSpec((None, ATT_HEADS, HEAD_DIM), lambda b, s, pt: (b, 0, 0))

    def page_spec(i):
        return pl.BlockSpec((None, None, ATT_HEADS, HEAD_DIM, PAGE),
                            lambda b, s, pt: (l, pt[b, s * gpages + i], 0, 0, 0))

    grid_spec = pltpu.PrefetchScalarGridSpec(
        num_scalar_prefetch=1,
        grid=(nb, n_steps),
        in_specs=[pl.BlockSpec((None, ATT_HEADS, HEAD_DIM, PAGE), lambda b, s, pt: (b, 0, 0, 0)),
                  tok(), tok(), tok()]
                 + [page_spec(i) for i in range(gpages)] + [page_spec(i) for i in range(gpages)],
        out_specs=tok(),
        scratch_shapes=[pltpu.VMEM((nblk, ATT_HEADS, PAGE), F32),
                        pltpu.VMEM((nblk, ATT_HEADS, PAGE), F32),
                        pltpu.VMEM((nblk, ATT_HEADS, PAGE), F32),
                        pltpu.VMEM((nblk, ATT_HEADS, HEAD_DIM, PAGE), F32),
                        pltpu.VMEM((gpages, ATT_HEADS, PAGE), F32),
                        pltpu.VMEM((nblk, ATT_HEADS, PAGE), F32)],
    )
    return pl.pallas_call(
        functools.partial(_sa_kernel, gpages=gpages, n_steps=n_steps),
        grid_spec=grid_spec,
        out_shape=jax.ShapeDtypeStruct((nb, ATT_HEADS, HEAD_DIM), F32),
        compiler_params=_cparams("arbitrary", "arbitrary"),
        name="moba_sample",
    )(page_table, qbc, q3, kn3, vn3, *([cache_kt] * gpages), *([cache_vt] * gpages))


def _gdn_constants():
    C = CHUNK
    lane = np.arange(DN_W)
    lane_j = lane % 64
    row = np.arange(C)[:, None]
    tri = np.stack([lane_j[None] <= row, lane_j[None] < row, lane_j[None] == row, row <= lane_j[None]])
    l_incl = np.arange(C)[None, :] <= np.arange(C)[:, None]
    blk = np.arange(DN_W) // 64
    bd = blk[:, None] == blk[None, :]
    place = np.stack([np.arange(DN_W)[None, :] == (np.arange(DN_DK)[:, None] + 64 * h) for h in range(DN_HEADS)])
    half = np.stack([np.broadcast_to((np.arange(128) // 64) == s, (C, 128)) for s in range(2)])
    return (jnp.asarray(tri, F32), jnp.asarray(l_incl, BF16), jnp.asarray(bd, BF16), jnp.asarray(bd, F32),
            jnp.asarray(place, BF16), jnp.asarray(half, BF16))


def _gdn_kernel(d_ref, z_ref, b_ref, a_ref, c0_ref, s0_ref, cw_ref, alog_ref, dtb_ref, dnw_ref,
                tri_ref, lincl_ref, onesbd_ref, bdmask_ref, place_ref, half_ref,
                y_ref, cout_ref, sout_ref, ext_scr, s_scr, *, bt, n_chunks, l_valid):
    n = pl.program_id(1)
    C = CHUNK
    incl4 = tri_ref[0] > 0.5
    strict4 = tri_ref[1] > 0.5
    eye4 = tri_ref[2]
    ut4 = tri_ref[3]
    l_incl = lincl_ref[...]
    ones_bd = onesbd_ref[...]
    zero_half = jnp.zeros((C, 128), BF16)

    def stack_mask(a4):
        ab = a4.astype(BF16)
        lo = ab[:, :128]
        hi = ab[:, 128:]
        return jnp.concatenate(
            [jnp.concatenate([lo * half_ref[0], zero_half], axis=1),
             jnp.concatenate([lo * half_ref[1], zero_half], axis=1),
             jnp.concatenate([zero_half, hi * half_ref[0]], axis=1),
             jnp.concatenate([zero_half, hi * half_ref[1]], axis=1)], axis=0)

    @pl.when(n == 0)
    def _init():
        for bi in range(bt):
            ext_scr[bi, pl.ds(5, CONV_W - 1), :] = c0_ref[bi]
            s_scr[bi] = jnp.concatenate(
                [_dot3_rhs_exact(s0_ref[bi, h], place_ref[h]) for h in range(DN_HEADS)], axis=0)

    last_valid = l_valid - (n_chunks - 1) * C
    masked = last_valid != C
    if masked:
        row = lax.broadcasted_iota(jnp.int32, (C, 1), 0)
        valid = (n * C + row) < l_valid

    for bi in range(bt):
        ext_scr[bi, pl.ds(8, C), :] = d_ref[bi]
        cw = cw_ref[...]
        yc = (cw[0:1] * ext_scr[bi, pl.ds(5, C), :] + cw[1:2] * ext_scr[bi, pl.ds(6, C), :]
              + cw[2:3] * ext_scr[bi, pl.ds(7, C), :] + cw[3:4] * ext_scr[bi, pl.ds(8, C), :])
        act = yc * _sigmoid(yc)
        cout_ref[bi] = ext_scr[bi, pl.ds(last_valid + 5, CONV_W - 1), :]
        ext_scr[bi, pl.ds(5, CONV_W - 1), :] = ext_scr[bi, pl.ds(8 + C - (CONV_W - 1), CONV_W - 1), :]

        q_raw = act[:, 0:DN_W]
        k_raw = act[:, DN_W:2 * DN_W]
        v4 = act[:, 2 * DN_W:3 * DN_W]
        q4 = q_raw * lax.rsqrt(_bdot(q_raw * q_raw, ones_bd) + EPS) * (DN_DK ** -0.5)
        k4 = k_raw * lax.rsqrt(_bdot(k_raw * k_raw, ones_bd) + EPS)
        beta4 = _sigmoid(b_ref[bi])
        xg = a_ref[bi] + dtb_ref[...]
        g4 = -jnp.exp(alog_ref[...]) * (jnp.maximum(xg, 0.0) + jnp.log1p(jnp.exp(-jnp.abs(xg))))
        if masked:
            q4 = jnp.where(valid, q4, 0.0)
            k4 = jnp.where(valid, k4, 0.0)
            v4 = jnp.where(valid, v4, 0.0)
            beta4 = jnp.where(valid, beta4, 0.0)
            g4 = jnp.where(valid, g4, 0.0)

        gi = _dot3_lhs_exact(l_incl, g4)
        gj = jnp.sum(g4 * ut4, axis=0, keepdims=True)
        decay4 = jnp.where(incl4, jnp.exp(jnp.where(incl4, gi - gj, 0.0)), 0.0)
        eg4 = jnp.exp(gi)
        glast = gi[C - 1:C, :]
        kfac = jnp.exp(glast - gi)
        gtot = jnp.exp(glast)

        kb4 = k4 * beta4
        mk = _bdot_t(jnp.concatenate([kb4, q4], axis=0), stack_mask(k4))
        m4 = jnp.where(strict4, mk[:C] * decay4, 0.0)
        attn4 = mk[C:] * decay4

        p = -m4
        t = eye4 + p
        p = _bdot(p, stack_mask(p))
        for _ in range(4):
            r = _bdot(jnp.concatenate([t, p], axis=0), stack_mask(p))
            t = t + r[:C]
            p = r[C:]
        t = t + _bdot(t, stack_mask(p))

        u4 = _bdot(t, stack_mask(v4 * beta4))
        kc4 = _bdot(t, stack_mask(kb4 * eg4))

        sbd = s_scr[bi]
        r = _bdot(jnp.concatenate([kc4, q4 * eg4], axis=0), sbd)
        vnew = u4 - r[:C]
        o4 = r[C:] + _bdot(attn4, stack_mask(vnew))
        kd4 = k4 * kfac
        upd = lax.dot_general(kd4.astype(BF16), vnew.astype(BF16), (((0,), (0,)), ((), ())),
                              preferred_element_type=F32)
        s_scr[bi] = sbd * gtot + upd * bdmask_ref[...]

        zz = z_ref[bi]
        sso = _bdot(o4 * o4, ones_bd)
        y_ref[bi] = o4 * lax.rsqrt(sso * (1.0 / DN_DK) + EPS) * dnw_ref[...] * (zz * _sigmoid(zz))

    @pl.when(n == n_chunks - 1)
    def _fin():
        for bi in range(bt):
            for h in range(DN_HEADS):
                sout_ref[bi, h] = _dot3_t_rhs_exact(s_scr[bi, h * 64:(h + 1) * 64, :], place_ref[h])


def _gdn(d3, z3, b3, a3, conv0, s0, cw, alog4, dtb4, dnw4, consts, l_valid, bt):
    nb, length, _ = d3.shape
    n_chunks = length // CHUNK
    tok = lambda w: pl.BlockSpec((bt, CHUNK, w), lambda i, n: (i, n, 0))
    const = lambda i, n: (0, 0)
    const3 = lambda i, n: (0, 0, 0)
    return pl.pallas_call(
        functools.partial(_gdn_kernel, bt=bt, n_chunks=n_chunks, l_valid=l_valid),
        grid=(nb // bt, n_chunks),
        in_specs=[tok(3 * DN_W), tok(DN_W), tok(DN_W), tok(DN_W),
                  pl.BlockSpec((bt, CONV_W - 1, 3 * DN_W), lambda i, n: (i, 0, 0)),
                  pl.BlockSpec((bt, DN_HEADS, DN_DK, DN_DK), lambda i, n: (i, 0, 0, 0)),
                  pl.BlockSpec((CONV_W, 3 * DN_W), const),
                  pl.BlockSpec((1, DN_W), const), pl.BlockSpec((1, DN_W), const),
                  pl.BlockSpec((1, DN_W), const),
                  pl.BlockSpec((4, CHUNK, DN_W), const3),
                  pl.BlockSpec((CHUNK, CHUNK), const),
                  pl.BlockSpec((DN_W, DN_W), const), pl.BlockSpec((DN_W, DN_W), const),
                  pl.BlockSpec((DN_HEADS, DN_DK, DN_W), const3),
                  pl.BlockSpec((2, CHUNK, 128), const3)],
        out_specs=[tok(DN_W),
                   pl.BlockSpec((bt, CONV_W - 1, 3 * DN_W), lambda i, n: (i, 0, 0)),
                   pl.BlockSpec((bt, DN_HEADS, DN_DK, DN_DK), lambda i, n: (i, 0, 0, 0))],
        out_shape=[jax.ShapeDtypeStruct((nb, length, DN_W), F32),
                   jax.ShapeDtypeStruct((nb, CONV_W - 1, 3 * DN_W), F32),
                   jax.ShapeDtypeStruct((nb, DN_HEADS, DN_DK, DN_DK), F32)],
        scratch_shapes=[pltpu.VMEM((bt, 8 + CHUNK, 3 * DN_W), F32),
                        pltpu.VMEM((bt, DN_W, DN_W), F32)],
        compiler_params=_cparams("arbitrary", "arbitrary"),
        name="gated_delta",
    )(d3, z3, b3, a3, conv0, s0, cw, alog4, dtb4, dnw4, *consts)


FF_CHUNK = D_FF // 2


def _out_kernel(x_ref, yp_ref, ya_ref, yd_ref, g1_ref, sh2_ref, sc2_ref, g2_ref, n2_ref,
                wo_ref, wg_ref, wu_ref, wd_ref, o_ref):
    mix = jnp.concatenate([yp_ref[...], ya_ref[...], yd_ref[...]], axis=1).astype(BF16)
    x1 = x_ref[...] + g1_ref[...] * jnp.dot(mix, wo_ref[...], preferred_element_type=F32)
    ms = jnp.mean(x1 * x1, axis=-1, keepdims=True)
    h2 = ((x1 * lax.rsqrt(ms + EPS) * n2_ref[...]) * (1.0 + sc2_ref[...]) + sh2_ref[...]).astype(BF16)
    acc = None
    for c in range(D_FF // FF_CHUNK):
        c0 = c * FF_CHUNK
        gt = jnp.dot(h2, wg_ref[:, c0:c0 + FF_CHUNK], preferred_element_type=F32)
        up = jnp.dot(h2, wu_ref[:, c0:c0 + FF_CHUNK], preferred_element_type=F32)
        act = (gt * _sigmoid(gt) * up).astype(BF16)
        part = jnp.dot(act, wd_ref[c0:c0 + FF_CHUNK, :], preferred_element_type=F32)
        acc = part if acc is None else acc + part
    o_ref[...] = x1 + g2_ref[...] * acc


def _out_ffn(x3, yp, ya, yd, mod, l, per_token, n2, wo, wg, wu, wd, tm):
    nb, length, _ = x3.shape
    tok = lambda w: pl.BlockSpec((None, tm, w), lambda b, t: (b, t, 0))
    const = lambda b, t: (0, 0)
    single = pl.Buffered(1)
    return pl.pallas_call(
        _out_kernel,
        grid=(nb, length // tm),
        in_specs=[tok(D_MODEL), tok(POOL_W), tok(ATT_W), tok(DN_W),
                  _mod_spec(mod, l, 2, per_token), _mod_spec(mod, l, 3, per_token),
                  _mod_spec(mod, l, 4, per_token), _mod_spec(mod, l, 5, per_token),
                  pl.BlockSpec((1, D_MODEL), const),
                  pl.BlockSpec((D_MODEL, D_MODEL), const, pipeline_mode=single),
                  pl.BlockSpec((D_MODEL, D_FF), const, pipeline_mode=single),
                  pl.BlockSpec((D_MODEL, D_FF), const, pipeline_mode=single),
                  pl.BlockSpec((D_FF, D_MODEL), const, pipeline_mode=single)],
        out_specs=tok(D_MODEL),
        out_shape=jax.ShapeDtypeStruct((nb, length, D_MODEL), F32),
        compiler_params=_cparams("arbitrary", "arbitrary"),
        name="out_ffn",
    )(x3, yp, ya, yd, mod, mod, mod, mod, n2, wo, wg, wu, wd)


def kernel(x_prompt, x_sample, cache_k, cache_v, state_pool, state_conv, state_delta, page_table,
           c_prompt, c_sample, norm1_w, ada_w, ada_b, w_in, pool_w, pool_scale, q_norm_w, k_norm_w,
           conv_w, a_log, dt_bias, dn_norm_w, w_out, norm2_w, w_gate, w_up, w_down):
    depth = w_in.shape[0]
    nbp, seq, _ = x_prompt.shape
    nbs = x_sample.shape[0]
    past_len = page_table.shape[1] * PAGE

    n_main = C_Z[1]
    w_ext = jnp.concatenate(
        [w_in[:, :, :n_main],
         jnp.repeat(w_in[:, :, n_main:n_main + DN_HEADS], DN_DK, axis=-1),
         jnp.repeat(w_in[:, :, n_main + DN_HEADS:], DN_DK, axis=-1)], axis=-1).astype(BF16)
    wo_b, wg_b, wu_b, wd_b = (w.astype(BF16) for w in (w_out, w_gate, w_up, w_down))
    eye_g = jnp.eye(len(POOL_WINDOWS), dtype=F32)
    pw_bd = (eye_g[None, :, None, :, None] * pool_w[:, :, :, None, :]).reshape(depth, POOL_W, POOL_W).astype(BF16)
    ones_att = jnp.kron(jnp.eye(ATT_HEADS, dtype=F32), jnp.ones((HEAD_DIM, HEAD_DIM), F32)).astype(BF16)
    qn4 = jnp.tile(q_norm_w, (1, ATT_HEADS)).reshape(depth, 1, ATT_W)
    kn4 = jnp.tile(k_norm_w, (1, ATT_HEADS)).reshape(depth, 1, ATT_W)
    dnw4 = jnp.tile(dn_norm_w, (1, DN_HEADS)).reshape(depth, 1, DN_W)
    alog4 = jnp.repeat(a_log, DN_DK, axis=-1).reshape(depth, 1, DN_W)
    dtb4 = jnp.repeat(dt_bias, DN_DK, axis=-1).reshape(depth, 1, DN_W)
    gdn_consts = _gdn_constants()
    cache_kt = jnp.transpose(cache_k, (0, 1, 3, 4, 2))
    cache_vt = jnp.transpose(cache_v, (0, 1, 3, 4, 2))

    mod = _modulation(jnp.concatenate([c_prompt, c_sample], axis=0), ada_w, ada_b)
    mod_p = mod[:, :nbp].reshape(depth, nbp, N_MOD, 1, D_MODEL)
    mod_s = mod[:, nbp:].reshape(depth, nbs, N_MOD, D_MODEL).transpose(0, 2, 1, 3)

    xp = x_prompt
    xs = x_sample.reshape(1, nbs, D_MODEL)
    zero_pool = jnp.zeros((nbp, POOL_BUF, POOL_W), F32)
    zero_pool_s = jnp.zeros((nbs, POOL_BUF, POOL_W), F32)
    zero_conv = jnp.zeros((nbp, CONV_W - 1, 3 * DN_W), F32)
    zero_state = jnp.zeros((nbp, DN_HEADS, DN_DK, DN_DK), F32)
    pad_rows = lambda a: jnp.pad(a.reshape(nbs, 1, a.shape[-1]), ((0, 0), (0, CHUNK - 1), (0, 0)))

    outs = {k: [] for k in ("kp", "vp", "ks", "vs", "pp", "ps", "cp", "cs", "sp", "ss")}
    for l in range(depth):
        n1 = norm1_w[l].reshape(1, D_MODEL)
        n2 = norm2_w[l].reshape(1, D_MODEL)
        ps = pool_scale[l].reshape(1, POOL_W)

        u, q, k, v, d, z, b4, a4 = _in_proj(xp, mod_p, l, False, n1, w_ext[l], qn4[l], kn4[l], ones_att, 512)
        y_pool, pool_new = _pool(u, zero_pool, pw_bd[l], ps, 0)
        y_att = _attn_prompt(q, k, v)
        y_dn, conv_new, s_new = _gdn(d, z, b4, a4, zero_conv, zero_state, conv_w[l], alog4[l], dtb4[l],
                                     dnw4[l], gdn_consts, seq, 8)
        xp = _out_ffn(xp, y_pool, y_att, y_dn, mod_p, l, False, n2, wo_b[l], wg_b[l], wu_b[l], wd_b[l], 512)
        outs["kp"].append(k.reshape(nbp, seq, ATT_HEADS, HEAD_DIM))
        outs["vp"].append(v.reshape(nbp, seq, ATT_HEADS, HEAD_DIM))
        outs["pp"].append(pool_new)
        outs["cp"].append(conv_new)
        outs["sp"].append(s_new)

        u, q, k, v, d, z, b4, a4 = _in_proj(xs, mod_s, l, True, n1, w_ext[l], qn4[l], kn4[l], ones_att, nbs)
        ext = jnp.concatenate([state_pool[l], u.reshape(nbs, 1, POOL_W)], axis=1)
        y_pool16, pool_new = _pool(ext, zero_pool_s, pw_bd[l], ps, past_len - POOL_BUF)
        y_pool = y_pool16[:, POOL_BUF:].reshape(1, nbs, POOL_W)
        o_att = _attn_sample(q.reshape(nbs, ATT_HEADS, HEAD_DIM).astype(F32),
                             k.reshape(nbs, ATT_HEADS, HEAD_DIM), v.reshape(nbs, ATT_HEADS, HEAD_DIM),
                             cache_kt, cache_vt, page_table, l)
        y_att = o_att.reshape(1, nbs, ATT_W)
        y_dn64, conv_new, s_new = _gdn(pad_rows(d), pad_rows(z), pad_rows(b4), pad_rows(a4),
                                       state_conv[l], state_delta[l], conv_w[l], alog4[l], dtb4[l],
                                       dnw4[l], gdn_consts, 1, 8)
        y_dn = y_dn64[:, 0].reshape(1, nbs, DN_W)
        xs = _out_ffn(xs, y_pool, y_att, y_dn, mod_s, l, True, n2, wo_b[l], wg_b[l], wu_b[l], wd_b[l], nbs)
        outs["ks"].append(k.reshape(nbs, 1, ATT_HEADS, HEAD_DIM))
        outs["vs"].append(v.reshape(nbs, 1, ATT_HEADS, HEAD_DIM))
        outs["ps"].append(pool_new)
        outs["cs"].append(conv_new)
        outs["ss"].append(s_new)

    st = lambda name: jnp.stack(outs[name])
    return (xp, xs.reshape(nbs, 1, D_MODEL), st("kp"), st("vp"), st("ks"), st("vs"),
            st("pp"), st("ps"), st("cp"), st("cs"), st("sp"), st("ss"))
```

```python
import functools

import numpy as np
import jax
import jax.numpy as jnp
from jax import lax
from jax.experimental import pallas as pl
from jax.experimental.pallas import tpu as pltpu

F32 = jnp.float32
BF16 = jnp.bfloat16

D_MODEL = 1024
PAGE = 128
POOL_WINDOWS = (2, 4, 8, 16)
POOL_GDIM = 64
POOL_W = 256
POOL_BUF = 15
HEAD_DIM = 64
ATT_HEADS = 8
ATT_W = 512
BLK = 256
TOPK = 3
DN_HEADS = 4
DN_DK = 64
DN_W = 256
CHUNK = 64
CONV_W = 4
D_FF = 2816
N_MOD = 6
EPS = 1e-6
NEG = -1e30
LOG2E = 1.4426950408889634

C_U = (0, 256)
C_Q = (256, 768)
C_K = (768, 1280)
C_V = (1280, 1792)
C_D = (1792, 2560)
C_Z = (2560, 2816)
C_B = (2816, 3072)
C_A = (3072, 3328)
IN_EXT = 3328

VMEM_LIMIT = 56 * 1024 * 1024


def _cparams(*sem):
    return pltpu.CompilerParams(dimension_semantics=sem, vmem_limit_bytes=VMEM_LIMIT)


def _sigmoid(x):
    return 1.0 / (1.0 + jnp.exp(-x))


def _bdot(a, b):
    return jnp.dot(a.astype(BF16), b.astype(BF16), preferred_element_type=F32)


def _bdot_t(a, b):
    return lax.dot_general(a.astype(BF16), b.astype(BF16), (((1,), (1,)), ((), ())),
                           preferred_element_type=F32)


def _split3(a):
    hi = a.astype(BF16)
    r1 = a - hi.astype(F32)
    mid = r1.astype(BF16)
    lo = (r1 - mid.astype(F32)).astype(BF16)
    return hi, mid, lo


def _dot3_rhs_exact(a, b01):
    return sum(jnp.dot(p, b01, preferred_element_type=F32) for p in _split3(a))


def _dot3_lhs_exact(a01, b):
    return sum(jnp.dot(a01, p, preferred_element_type=F32) for p in _split3(b))


def _dot3_t_rhs_exact(a, b01):
    return sum(lax.dot_general(p, b01, (((1,), (1,)), ((), ())), preferred_element_type=F32)
               for p in _split3(a))


def _dot3_t_lhs_exact(a01, b):
    return sum(lax.dot_general(a01, p, (((1,), (1,)), ((), ())), preferred_element_type=F32)
               for p in _split3(b))


def _mod_kernel(c_ref, w_ref, b_ref, o_ref):
    c = c_ref[...]
    a = (c * _sigmoid(c)).astype(BF16)
    o_ref[...] = jnp.dot(a, w_ref[...].astype(BF16), preferred_element_type=F32) + b_ref[...]


def _modulation(c_all, ada_w, ada_b):
    depth, _, ncol = ada_w.shape
    nseq = c_all.shape[0]
    tn = 1536
    return pl.pallas_call(
        _mod_kernel,
        grid=(depth, ncol // tn),
        in_specs=[pl.BlockSpec((nseq, D_MODEL), lambda l, j: (0, 0)),
                  pl.BlockSpec((None, D_MODEL, tn), lambda l, j: (l, 0, j)),
                  pl.BlockSpec((None, 1, tn), lambda l, j: (l, 0, j))],
        out_specs=pl.BlockSpec((None, nseq, tn), lambda l, j: (l, 0, j)),
        out_shape=jax.ShapeDtypeStruct((depth, nseq, ncol), F32),
        compiler_params=_cparams("arbitrary", "arbitrary"),
        name="modulation",
    )(c_all, ada_w, ada_b.reshape(depth, 1, ncol))


def _mod_spec(mod, l, k, per_token):
    if per_token:
        return pl.BlockSpec((None, None, mod.shape[2], D_MODEL), lambda b, t: (l, k, 0, 0))
    return pl.BlockSpec((None, None, None, 1, D_MODEL), lambda b, t: (l, b, k, 0, 0))


def _in_kernel(x_ref, sh_ref, sc_ref, n1_ref, w_ref, qn_ref, kn_ref, ones_ref,
               u_ref, q_ref, k_ref, v_ref, d_ref, z_ref, b_ref, a_ref):
    x = x_ref[...]
    ms = jnp.mean(x * x, axis=-1, keepdims=True)
    h = (x * lax.rsqrt(ms + EPS) * n1_ref[...]) * (1.0 + sc_ref[...]) + sh_ref[...]
    hb = h.astype(BF16)

    def proj(c):
        return jnp.dot(hb, w_ref[:, c[0]:c[1]], preferred_element_type=F32)

    def head_norm(t, w4):
        ss = jnp.dot((t * t).astype(BF16), ones_ref[...], preferred_element_type=F32)
        return t * lax.rsqrt(ss * (1.0 / HEAD_DIM) + EPS) * w4

    u_ref[...] = proj(C_U)
    q_ref[...] = (head_norm(proj(C_Q), qn_ref[...]) * (HEAD_DIM ** -0.5 * LOG2E)).astype(BF16)
    k_ref[...] = head_norm(proj(C_K), kn_ref[...])
    v_ref[...] = proj(C_V)
    d_ref[...] = proj(C_D)
    z_ref[...] = proj(C_Z)
    b_ref[...] = proj(C_B)
    a_ref[...] = proj(C_A)


def _in_proj(x3, mod, l, per_token, n1, w_ext, qn4, kn4, ones_att, tm):
    nb, length, _ = x3.shape
    widths = (POOL_W, ATT_W, ATT_W, ATT_W, 3 * DN_W, DN_W, DN_W, DN_W)
    dtypes = (F32, BF16, F32, F32, F32, F32, F32, F32)
    const = lambda b, t: (0, 0)
    tok = lambda w: pl.BlockSpec((None, tm, w), lambda b, t: (b, t, 0))
    return pl.pallas_call(
        _in_kernel,
        grid=(nb, length // tm),
        in_specs=[tok(D_MODEL), _mod_spec(mod, l, 0, per_token), _mod_spec(mod, l, 1, per_token),
                  pl.BlockSpec((1, D_MODEL), const),
                  pl.BlockSpec((D_MODEL, IN_EXT), const),
                  pl.BlockSpec((1, ATT_W), const), pl.BlockSpec((1, ATT_W), const),
                  pl.BlockSpec((ATT_W, ATT_W), const)],
        out_specs=[tok(w) for w in widths],
        out_shape=[jax.ShapeDtypeStruct((nb, length, w), dt) for w, dt in zip(widths, dtypes)],
        compiler_params=_cparams("arbitrary", "arbitrary"),
        name="in_proj",
    )(x3, mod, mod, n1, w_ext, qn4, kn4, ones_att)


def _pool_kernel(u_ref, buf_ref, pw_ref, ps_ref, y_ref, new_ref, ext_ref, *, length, pos0):
    ext_ref[pl.ds(1, POOL_BUF), :] = buf_ref[...]
    ext_ref[pl.ds(16, length), :] = u_ref[...]
    lane = lax.broadcasted_iota(jnp.int32, (1, POOL_W), 1)
    grp = jnp.right_shift(lane, 6)
    wl = jnp.where(grp == 0, 2, jnp.where(grp == 1, 4, jnp.where(grp == 2, 8, 16)))
    ch = min(length, 256)
    row = lax.broadcasted_iota(jnp.int32, (ch, 1), 0)
    for c in range(length // ch):
        base = 16 + c * ch
        cur = ext_ref[pl.ds(base, ch), :]
        acc = cur
        sums = {}
        for i in range(1, 16):
            acc = acc + ext_ref[pl.ds(base - i, ch), :]
            if i + 1 in POOL_WINDOWS:
                sums[i + 1] = acc
        wsum = jnp.where(grp == 0, sums[2], jnp.where(grp == 1, sums[4],
                                                      jnp.where(grp == 2, sums[8], sums[16])))
        cnt = jnp.minimum(wl, row + (pos0 + c * ch + 1)).astype(F32)
        d = wsum / cnt - cur
        y = jnp.dot(d.astype(BF16), pw_ref[...], preferred_element_type=F32) * ps_ref[...]
        y_ref[pl.ds(c * ch, ch), :] = y
    new_ref[...] = ext_ref[pl.ds(length + 1, POOL_BUF), :]


def _pool(u3, buf3, pw_bd, ps, pos0):
    nb, length, _ = u3.shape
    const = lambda b: (0, 0)
    return pl.pallas_call(
        functools.partial(_pool_kernel, length=length, pos0=pos0),
        grid=(nb,),
        in_specs=[pl.BlockSpec((None, length, POOL_W), lambda b: (b, 0, 0)),
                  pl.BlockSpec((None, POOL_BUF, POOL_W), lambda b: (b, 0, 0)),
                  pl.BlockSpec((POOL_W, POOL_W), const), pl.BlockSpec((1, POOL_W), const)],
        out_specs=[pl.BlockSpec((None, length, POOL_W), lambda b: (b, 0, 0)),
                   pl.BlockSpec((None, POOL_BUF, POOL_W), lambda b: (b, 0, 0))],
        out_shape=[jax.ShapeDtypeStruct((nb, length, POOL_W), F32),
                   jax.ShapeDtypeStruct((nb, POOL_BUF, POOL_W), F32)],
        scratch_shapes=[pltpu.VMEM((16 + length, POOL_W), F32)],
        compiler_params=_cparams("arbitrary"),
        name="pool_mixer",
    )(u3, buf3, pw_bd, ps)


def _attn_kernel(q_ref, k_ref, v_ref, o_ref, km_scr, vt_scr, ot_scr, s_scr, *, length):
    nb = length // BLK
    lane = lax.broadcasted_iota(jnp.int32, (1, 128), 1)
    k2 = k_ref[...]
    kmean = jnp.sum(k2.reshape(nb, BLK, 128), axis=1) * (1.0 / BLK)
    vt = v_ref[...].T
    for j in range(nb):
        vt_scr[j] = vt[:, j * BLK:(j + 1) * BLK].astype(BF16)
    krow = lax.broadcasted_iota(jnp.int32, (BLK, BLK), 0)
    qcol = lax.broadcasted_iota(jnp.int32, (BLK, BLK), 1)
    causal = krow <= qcol
    blkrow = lax.broadcasted_iota(jnp.int32, (nb, BLK), 0)
    pad_rows = jnp.zeros((16 - nb, 128), F32) if nb < 16 else None
    kmh = []
    for hh in range(2):
        hm = jnp.right_shift(lane, 6) == hh
        km_scr[hh] = jnp.where(hm, k2, 0.0).astype(BF16).reshape(nb, BLK, 128)
        t = jnp.where(hm, kmean, 0.0)
        kmh.append(t if pad_rows is None else jnp.concatenate([t, pad_rows], axis=0))

    def fold8(t):
        return t.reshape(BLK // 8, 8, BLK)

    def scores(hh, i, slot):
        qi = q_ref[pl.ds(i * BLK, BLK), :]
        bias = None
        if i > 0:
            gate = _bdot_t(kmh[hh], qi)[:nb]
            past = blkrow < i
            gate = jnp.where(past, gate, -jnp.inf)
            cnt = jnp.zeros((nb, BLK), jnp.int32)
            for jp in range(i):
                gj = gate[jp:jp + 1, :]
                cnt = cnt + jnp.where(gj > gate, 1,
                                      jnp.where(gj == gate, (jp < blkrow).astype(jnp.int32), 0))
            sel = jnp.where(past, cnt, TOPK) < TOPK
            bias = jnp.where(sel, 0.0, NEG)
        m8 = None
        for j in range(i + 1):
            s = _bdot_t(km_scr[hh, j], qi)
            s = jnp.where(causal, s, NEG) if j == i else s + bias[j:j + 1, :]
            s_scr[slot, j] = s
            t = jnp.max(fold8(s), axis=0)
            m8 = t if m8 is None else jnp.maximum(m8, t)
        return jnp.max(m8, axis=0, keepdims=True)

    def weighted(hh, i, slot, m):
        l8 = None
        acc = None
        for j in range(i + 1):
            p = jnp.exp2(s_scr[slot, j] - m)
            t = jnp.sum(fold8(p), axis=0)
            l8 = t if l8 is None else l8 + t
            pv = jnp.dot(vt_scr[j], p.astype(BF16), preferred_element_type=F32)
            acc = pv if acc is None else acc + pv
        l = jnp.sum(l8, axis=0, keepdims=True)
        ot_scr[hh * 64:(hh + 1) * 64, i * BLK:(i + 1) * BLK] = acc[hh * 64:(hh + 1) * 64, :] / l

    items = [(hh, i) for hh in range(2) for i in range(nb)]
    pending = None
    for n, (hh, i) in enumerate(items):
        m = scores(hh, i, n % 2)
        if pending is not None:
            weighted(*pending)
        pending = (hh, i, n % 2, m)
    weighted(*pending)
    o_ref[...] = ot_scr[...].T


def _attn_prompt(q3, k3, v3):
    nb, length, _ = q3.shape
    nblk = length // BLK
    spec = pl.BlockSpec((None, length, 128), lambda b, h: (b, 0, h))
    return pl.pallas_call(
        functools.partial(_attn_kernel, length=length),
        grid=(nb, ATT_W // 128),
        in_specs=[spec, spec, spec],
        out_specs=spec,
        out_shape=jax.ShapeDtypeStruct((nb, length, ATT_W), F32),
        scratch_shapes=[pltpu.VMEM((2, nblk, BLK, 128), BF16),
                        pltpu.VMEM((nblk, 128, BLK), BF16),
                        pltpu.VMEM((128, length), F32),
                        pltpu.VMEM((2, nblk, BLK, BLK), F32)],
        compiler_params=_cparams("arbitrary", "arbitrary"),
        name="moba_prompt",
    )(q3, k3, v3)


def _sa_kernel(pt_ref, qbc_ref, q_ref, kn_ref, vn_ref, *refs, gpages, n_steps):
    kp = refs[:gpages]
    vp = refs[gpages:2 * gpages]
    o_ref = refs[2 * gpages]
    g_scr, m_scr, l_scr, o_scr, s_scr, w_scr = refs[2 * gpages + 1:]
    step = pl.program_id(1)
    nblk_step = gpages // 2
    full = (ATT_HEADS, PAGE)

    for pg in range(gpages):
        for h in range(ATT_HEADS):
            s_scr[pg, pl.ds(h, 1), :] = jnp.sum(kp[pg][h] * qbc_ref[h], axis=0, keepdims=True)

    for blk in range(nblk_step):
        s0 = s_scr[2 * blk]
        s1 = s_scr[2 * blk + 1]
        gsum = jnp.sum(s0, axis=-1, keepdims=True) + jnp.sum(s1, axis=-1, keepdims=True)
        mb = jnp.maximum(jnp.max(s0, axis=-1, keepdims=True), jnp.max(s1, axis=-1, keepdims=True))
        p0 = jnp.exp2(s0 - mb)
        p1 = jnp.exp2(s1 - mb)
        lb = jnp.sum(p0, axis=-1, keepdims=True) + jnp.sum(p1, axis=-1, keepdims=True)
        s_scr[2 * blk] = p0
        s_scr[2 * blk + 1] = p1
        idx = step * nblk_step + blk
        g_scr[idx] = jnp.broadcast_to(gsum, full)
        m_scr[idx] = jnp.broadcast_to(mb, full)
        l_scr[idx] = jnp.broadcast_to(lb, full)
        for h in range(ATT_HEADS):
            o_scr[idx, h] = (vp[2 * blk][h] * s_scr[2 * blk, pl.ds(h, 1), :]
                             + vp[2 * blk + 1][h] * s_scr[2 * blk + 1, pl.ds(h, 1), :])

    @pl.when(step == n_steps - 1)
    def _merge():
        nblk = n_steps * nblk_step
        gates = g_scr[...]
        if nblk > TOPK:
            lane = lax.broadcasted_iota(jnp.int32, full, 1)
            gates_c = jnp.zeros(full, F32)
            for b in range(nblk):
                gates_c = jnp.where(lane == b, gates[b], gates_c)
            cnt = jnp.zeros(full, jnp.int32)
            for b in range(nblk):
                cnt = cnt + jnp.where(gates[b] > gates_c, 1,
                                      jnp.where(gates[b] == gates_c, (b < lane).astype(jnp.int32), 0))
            last = jnp.where(lane < nblk, cnt, -1) == TOPK - 1
            g3 = jnp.max(jnp.where(last, gates_c, -jnp.inf), axis=-1, keepdims=True)
            b3 = jnp.max(jnp.where(last, lane.astype(F32), -1.0), axis=-1, keepdims=True)
            bidx = lax.broadcasted_iota(jnp.int32, gates.shape, 0).astype(F32)
            sel = (gates > g3[None]) | ((gates == g3[None]) & (bidx <= b3[None]))
        else:
            sel = jnp.full(gates.shape, True)
        q = q_ref[...]
        sself = jnp.broadcast_to(jnp.sum(q * kn_ref[...], axis=-1, keepdims=True), full)
        mb = m_scr[...]
        mtot = jnp.maximum(jnp.max(jnp.where(sel, mb, -jnp.inf), axis=0), sself)
        w = jnp.where(sel, jnp.exp2(jnp.where(sel, mb - mtot[None], 0.0)), 0.0)
        wself = jnp.exp2(sself - mtot)
        denom = jnp.sum(w * l_scr[...], axis=0) + wself
        w_scr[...] = w
        ones8 = jnp.ones((8, PAGE), BF16)
        sub = lax.broadcasted_iota(jnp.int32, (ATT_HEADS, HEAD_DIM), 0)
        o_acc = jnp.zeros((ATT_HEADS, HEAD_DIM), F32)
        for h in range(ATT_HEADS):
            tot = w_scr[0, pl.ds(h, 1), :] * o_scr[0, h]
            for b in range(1, nblk):
                tot = tot + w_scr[b, pl.ds(h, 1), :] * o_scr[b, h]
            r = _dot3_t_lhs_exact(ones8, tot)
            o_acc = jnp.where(sub == h, r, o_acc)
        o_ref[...] = (o_acc + wself[:, :HEAD_DIM] * vn_ref[...]) / denom[:, :HEAD_DIM]


def _attn_sample(q3, kn3, vn3, cache_kt, cache_vt, page_table, l, gpages=32):
    nb = q3.shape[0]
    n_pages = page_table.shape[1]
    n_steps = n_pages // gpages
    nblk = n_pages // 2
    qbc = jnp.broadcast_to(q3[..., None], (nb, ATT_HEADS, HEAD_DIM, PAGE))
    tok = lambda: pl.BlockSpec((None, ATT_HEADS, HEAD_DIM), lambda b, s, pt: (b, 0, 0))

    def page_spec(i):
        return pl.BlockSpec((None, None, ATT_HEADS, HEAD_DIM, PAGE),
                            lambda b, s, pt: (l, pt[b, s * gpages + i], 0, 0, 0))

    grid_spec = pltpu.PrefetchScalarGridSpec(
        num_scalar_prefetch=1,
        grid=(nb, n_steps),
        in_specs=[pl.BlockSpec((None, ATT_HEADS, HEAD_DIM, PAGE), lambda b, s, pt: (b, 0, 0, 0)),
                  tok(), tok(), tok()]
                 + [page_spec(i) for i in range(gpages)] + [page_spec(i) for i in range(gpages)],
        out_specs=tok(),
        scratch_shapes=[pltpu.VMEM((nblk, ATT_HEADS, PAGE), F32),
                        pltpu.VMEM((nblk, ATT_HEADS, PAGE), F32),
                        pltpu.VMEM((nblk, ATT_HEADS, PAGE), F32),
                        pltpu.VMEM((nblk, ATT_HEADS, HEAD_DIM, PAGE), F32),
                        pltpu.VMEM((gpages, ATT_HEADS, PAGE), F32),
                        pltpu.VMEM((nblk, ATT_HEADS, PAGE), F32)],
    )
    return pl.pallas_call(
        functools.partial(_sa_kernel, gpages=gpages, n_steps=n_steps),
        grid_spec=grid_spec,
        out_shape=jax.ShapeDtypeStruct((nb, ATT_HEADS, HEAD_DIM), F32),
        compiler_params=_cparams("arbitrary", "arbitrary"),
        name="moba_sample",
    )(page_table, qbc, q3, kn3, vn3, *([cache_kt] * gpages), *([cache_vt] * gpages))


def _gdn_constants():
    C = CHUNK
    lane = np.arange(DN_W)
    lane_j = lane % 64
    row = np.arange(C)[:, None]
    tri = np.stack([lane_j[None] <= row, lane_j[None] < row, lane_j[None] == row, row <= lane_j[None]])
    l_incl = np.arange(C)[None, :] <= np.arange(C)[:, None]
    blk = np.arange(DN_W) // 64
    bd = blk[:, None] == blk[None, :]
    place = np.stack([np.arange(DN_W)[None, :] == (np.arange(DN_DK)[:, None] + 64 * h) for h in range(DN_HEADS)])
    half = np.stack([np.broadcast_to((np.arange(128) // 64) == s, (C, 128)) for s in range(2)])
    return (jnp.asarray(tri, F32), jnp.asarray(l_incl, BF16), jnp.asarray(bd, BF16), jnp.asarray(bd, F32),
            jnp.asarray(place, BF16), jnp.asarray(half, F32))


def _gdn_kernel(d_ref, z_ref, b_ref, a_ref, c0_ref, s0_ref, cw_ref, alog_ref, dtb_ref, dnw_ref,
                tri_ref, lincl_ref, onesbd_ref, bdmask_ref, place_ref, half_ref,
                y_ref, cout_ref, sout_ref, ext_scr, s_scr, *, bt, n_chunks, l_valid):
    n = pl.program_id(1)
    C = CHUNK
    incl4 = tri_ref[0] > 0.5
    strict4 = tri_ref[1] > 0.5
    eye4 = tri_ref[2]
    ut4 = tri_ref[3]
    l_incl = lincl_ref[...]
    ones_bd = onesbd_ref[...]
    zero_half = jnp.zeros((C, 128), F32)
    left = half_ref[0] > 0.5
    right = half_ref[1] > 0.5

    def stack_mask(a4):
        lo = a4[:, :128]
        hi = a4[:, 128:]
        return jnp.concatenate(
            [jnp.concatenate([jnp.where(left, lo, 0.0), zero_half], axis=1),
             jnp.concatenate([jnp.where(right, lo, 0.0), zero_half], axis=1),
             jnp.concatenate([zero_half, jnp.where(left, hi, 0.0)], axis=1),
             jnp.concatenate([zero_half, jnp.where(right, hi, 0.0)], axis=1)], axis=0).astype(BF16)

    @pl.when(n == 0)
    def _init():
        for bi in range(bt):
            ext_scr[bi, pl.ds(5, CONV_W - 1), :] = c0_ref[bi]
            s_scr[bi] = jnp.concatenate(
                [_dot3_rhs_exact(s0_ref[bi, h], place_ref[h]) for h in range(DN_HEADS)], axis=0)

    last_valid = l_valid - (n_chunks - 1) * C
    masked = last_valid != C
    if masked:
        row = lax.broadcasted_iota(jnp.int32, (C, 1), 0)
        valid = (n * C + row) < l_valid

    seqs = range(bt)
    each = lambda f, *cols: [f(*xs) for xs in zip(*cols)]
    cw = cw_ref[...]
    act = []
    for bi in seqs:
        ext_scr[bi, pl.ds(8, C), :] = d_ref[bi]
        yc = (cw[0:1] * ext_scr[bi, pl.ds(5, C), :] + cw[1:2] * ext_scr[bi, pl.ds(6, C), :]
              + cw[2:3] * ext_scr[bi, pl.ds(7, C), :] + cw[3:4] * ext_scr[bi, pl.ds(8, C), :])
        act.append(yc * _sigmoid(yc))
        cout_ref[bi] = ext_scr[bi, pl.ds(last_valid + 5, CONV_W - 1), :]
        ext_scr[bi, pl.ds(5, CONV_W - 1), :] = ext_scr[bi, pl.ds(8 + C - (CONV_W - 1), CONV_W - 1), :]

    q_raw = [a[:, 0:DN_W] for a in act]
    k_raw = [a[:, DN_W:2 * DN_W] for a in act]
    v4 = [a[:, 2 * DN_W:3 * DN_W] for a in act]
    ssq = each(lambda x: _bdot(x * x, ones_bd), q_raw)
    ssk = each(lambda x: _bdot(x * x, ones_bd), k_raw)
    q4 = each(lambda x, s: x * lax.rsqrt(s + EPS) * (DN_DK ** -0.5), q_raw, ssq)
    k4 = each(lambda x, s: x * lax.rsqrt(s + EPS), k_raw, ssk)
    beta4 = [_sigmoid(b_ref[bi]) for bi in seqs]

    def log_decay(bi):
        xg = a_ref[bi] + dtb_ref[...]
        return -jnp.exp(alog_ref[...]) * (jnp.maximum(xg, 0.0) + jnp.log1p(jnp.exp(-jnp.abs(xg))))

    g4 = [log_decay(bi) for bi in seqs]
    if masked:
        zero_pad = lambda x: jnp.where(valid, x, 0.0)
        q4, k4, v4, beta4, g4 = (each(zero_pad, c) for c in (q4, k4, v4, beta4, g4))

    gi = each(lambda g: _dot3_lhs_exact(l_incl, g), g4)
    gj = each(lambda g: jnp.sum(g * ut4, axis=0, keepdims=True), g4)
    decay4 = each(lambda a, b: jnp.where(incl4, jnp.exp(jnp.where(incl4, a - b, 0.0)), 0.0), gi, gj)
    eg4 = each(jnp.exp, gi)
    glast = [g[C - 1:C, :] for g in gi]
    kfac = each(lambda a, b: jnp.exp(a - b), glast, gi)
    gtot = each(jnp.exp, glast)

    kb4 = each(lambda a, b: a * b, k4, beta4)
    mk = each(lambda kb, q, k: _bdot_t(jnp.concatenate([kb, q], axis=0), stack_mask(k)), kb4, q4, k4)
    attn4 = each(lambda r, dc: r[C:] * dc, mk, decay4)

    p = each(lambda r, dc: -jnp.where(strict4, r[:C] * dc, 0.0), mk, decay4)
    t = each(lambda x: eye4 + x, p)
    p = each(lambda x: _bdot(x, stack_mask(x)), p)
    for _ in range(4):
        r = each(lambda a, b: _bdot(jnp.concatenate([a, b], axis=0), stack_mask(b)), t, p)
        t = each(lambda a, b: a + b[:C], t, r)
        p = [x[C:] for x in r]
    t = each(lambda a, b: a + _bdot(a, stack_mask(b)), t, p)

    u4 = each(lambda a, v, b: _bdot(a, stack_mask(v * b)), t, v4, beta4)
    kc4 = each(lambda a, kb, e: _bdot(a, stack_mask(kb * e)), t, kb4, eg4)

    sbd = [s_scr[bi] for bi in seqs]
    r = each(lambda kc, q, e, s: _bdot(jnp.concatenate([kc, q * e], axis=0), s), kc4, q4, eg4, sbd)
    vnew = each(lambda u, x: u - x[:C], u4, r)
    o4 = each(lambda x, a, v: x[C:] + _bdot(a, stack_mask(v)), r, attn4, vnew)
    upd = each(lambda k, f, v: lax.dot_general((k * f).astype(BF16), v.astype(BF16), (((0,), (0,)), ((), ())),
                                               preferred_element_type=F32), k4, kfac, vnew)
    sso = each(lambda o: _bdot(o * o, ones_bd), o4)
    for bi in seqs:
        s_scr[bi] = sbd[bi] * gtot[bi] + upd[bi] * bdmask_ref[...]
        zz = z_ref[bi]
        y_ref[bi] = (o4[bi] * lax.rsqrt(sso[bi] * (1.0 / DN_DK) + EPS) * dnw_ref[...]
                     * (zz * _sigmoid(zz)))

    @pl.when(n == n_chunks - 1)
    def _fin():
        for bi in range(bt):
            for h in range(DN_HEADS):
                sout_ref[bi, h] = _dot3_t_rhs_exact(s_scr[bi, h * 64:(h + 1) * 64, :], place_ref[h])


def _gdn(d3, z3, b3, a3, conv0, s0, cw, alog4, dtb4, dnw4, consts, l_valid, bt):
    nb, length, _ = d3.shape
    n_chunks = length // CHUNK
    tok = lambda w: pl.BlockSpec((bt, CHUNK, w), lambda i, n: (i, n, 0))
    const = lambda i, n: (0, 0)
    const3 = lambda i, n: (0, 0, 0)
    return pl.pallas_call(
        functools.partial(_gdn_kernel, bt=bt, n_chunks=n_chunks, l_valid=l_valid),
        grid=(nb // bt, n_chunks),
        in_specs=[tok(3 * DN_W), tok(DN_W), tok(DN_W), tok(DN_W),
                  pl.BlockSpec((bt, CONV_W - 1, 3 * DN_W), lambda i, n: (i, 0, 0)),
                  pl.BlockSpec((bt, DN_HEADS, DN_DK, DN_DK), lambda i, n: (i, 0, 0, 0)),
                  pl.BlockSpec((CONV_W, 3 * DN_W), const),
                  pl.BlockSpec((1, DN_W), const), pl.BlockSpec((1, DN_W), const),
                  pl.BlockSpec((1, DN_W), const),
                  pl.BlockSpec((4, CHUNK, DN_W), const3),
                  pl.BlockSpec((CHUNK, CHUNK), const),
                  pl.BlockSpec((DN_W, DN_W), const), pl.BlockSpec((DN_W, DN_W), const),
                  pl.BlockSpec((DN_HEADS, DN_DK, DN_W), const3),
                  pl.BlockSpec((2, CHUNK, 128), const3)],
        out_specs=[tok(DN_W),
                   pl.BlockSpec((bt, CONV_W - 1, 3 * DN_W), lambda i, n: (i, 0, 0)),
                   pl.BlockSpec((bt, DN_HEADS, DN_DK, DN_DK), lambda i, n: (i, 0, 0, 0))],
        out_shape=[jax.ShapeDtypeStruct((nb, length, DN_W), F32),
                   jax.ShapeDtypeStruct((nb, CONV_W - 1, 3 * DN_W), F32),
                   jax.ShapeDtypeStruct((nb, DN_HEADS, DN_DK, DN_DK), F32)],
        scratch_shapes=[pltpu.VMEM((bt, 8 + CHUNK, 3 * DN_W), F32),
                        pltpu.VMEM((bt, DN_W, DN_W), F32)],
        compiler_params=_cparams("arbitrary", "arbitrary"),
        name="gated_delta",
    )(d3, z3, b3, a3, conv0, s0, cw, alog4, dtb4, dnw4, *consts)


FF_CHUNK = D_FF // 2


def _out_kernel(x_ref, yp_ref, ya_ref, yd_ref, g1_ref, sh2_ref, sc2_ref, g2_ref, n2_ref,
                wo_ref, wg_ref, wu_ref, wd_ref, o_ref):
    mix = jnp.concatenate([yp_ref[...], ya_ref[...], yd_ref[...]], axis=1).astype(BF16)
    x1 = x_ref[...] + g1_ref[...] * jnp.dot(mix, wo_ref[...], preferred_element_type=F32)
    ms = jnp.mean(x1 * x1, axis=-1, keepdims=True)
    h2 = ((x1 * lax.rsqrt(ms + EPS) * n2_ref[...]) * (1.0 + sc2_ref[...]) + sh2_ref[...]).astype(BF16)
    acc = None
    for c in range(D_FF // FF_CHUNK):
        c0 = c * FF_CHUNK
        gt = jnp.dot(h2, wg_ref[:, c0:c0 + FF_CHUNK], preferred_element_type=F32)
        up = jnp.dot(h2, wu_ref[:, c0:c0 + FF_CHUNK], preferred_element_type=F32)
        act = (gt * _sigmoid(gt) * up).astype(BF16)
        part = jnp.dot(act, wd_ref[c0:c0 + FF_CHUNK, :], preferred_element_type=F32)
        acc = part if acc is None else acc + part
    o_ref[...] = x1 + g2_ref[...] * acc


def _out_ffn(x3, yp, ya, yd, mod, l, per_token, n2, wo, wg, wu, wd, tm):
    nb, length, _ = x3.shape
    tok = lambda w: pl.BlockSpec((None, tm, w), lambda b, t: (b, t, 0))
    const = lambda b, t: (0, 0)
    single = pl.Buffered(1)
    return pl.pallas_call(
        _out_kernel,
        grid=(nb, length // tm),
        in_specs=[tok(D_MODEL), tok(POOL_W), tok(ATT_W), tok(DN_W),
                  _mod_spec(mod, l, 2, per_token), _mod_spec(mod, l, 3, per_token),
                  _mod_spec(mod, l, 4, per_token), _mod_spec(mod, l, 5, per_token),
                  pl.BlockSpec((1, D_MODEL), const),
                  pl.BlockSpec((D_MODEL, D_MODEL), const, pipeline_mode=single),
                  pl.BlockSpec((D_MODEL, D_FF), const, pipeline_mode=single),
                  pl.BlockSpec((D_MODEL, D_FF), const, pipeline_mode=single),
                  pl.BlockSpec((D_FF, D_MODEL), const, pipeline_mode=single)],
        out_specs=tok(D_MODEL),
        out_shape=jax.ShapeDtypeStruct((nb, length, D_MODEL), F32),
        compiler_params=_cparams("arbitrary", "arbitrary"),
        name="out_ffn",
    )(x3, yp, ya, yd, mod, mod, mod, mod, n2, wo, wg, wu, wd)


def kernel(x_prompt, x_sample, cache_k, cache_v, state_pool, state_conv, state_delta, page_table,
           c_prompt, c_sample, norm1_w, ada_w, ada_b, w_in, pool_w, pool_scale, q_norm_w, k_norm_w,
           conv_w, a_log, dt_bias, dn_norm_w, w_out, norm2_w, w_gate, w_up, w_down):
    depth = w_in.shape[0]
    nbp, seq, _ = x_prompt.shape
    nbs = x_sample.shape[0]
    past_len = page_table.shape[1] * PAGE

    n_main = C_Z[1]
    w_ext = jnp.concatenate(
        [w_in[:, :, :n_main],
         jnp.repeat(w_in[:, :, n_main:n_main + DN_HEADS], DN_DK, axis=-1),
         jnp.repeat(w_in[:, :, n_main + DN_HEADS:], DN_DK, axis=-1)], axis=-1).astype(BF16)
    wo_b, wg_b, wu_b, wd_b = (w.astype(BF16) for w in (w_out, w_gate, w_up, w_down))
    eye_g = jnp.eye(len(POOL_WINDOWS), dtype=F32)
    pw_bd = (eye_g[None, :, None, :, None] * pool_w[:, :, :, None, :]).reshape(depth, POOL_W, POOL_W).astype(BF16)
    ones_att = jnp.kron(jnp.eye(ATT_HEADS, dtype=F32), jnp.ones((HEAD_DIM, HEAD_DIM), F32)).astype(BF16)
    qn4 = jnp.tile(q_norm_w, (1, ATT_HEADS)).reshape(depth, 1, ATT_W)
    kn4 = jnp.tile(k_norm_w, (1, ATT_HEADS)).reshape(depth, 1, ATT_W)
    dnw4 = jnp.tile(dn_norm_w, (1, DN_HEADS)).reshape(depth, 1, DN_W)
    alog4 = jnp.repeat(a_log, DN_DK, axis=-1).reshape(depth, 1, DN_W)
    dtb4 = jnp.repeat(dt_bias, DN_DK, axis=-1).reshape(depth, 1, DN_W)
    gdn_consts = _gdn_constants()
    cache_kt = jnp.transpose(cache_k, (0, 1, 3, 4, 2))
    cache_vt = jnp.transpose(cache_v, (0, 1, 3, 4, 2))

    mod = _modulation(jnp.concatenate([c_prompt, c_sample], axis=0), ada_w, ada_b)
    mod_p = mod[:, :nbp].reshape(depth, nbp, N_MOD, 1, D_MODEL)
    mod_s = mod[:, nbp:].reshape(depth, nbs, N_MOD, D_MODEL).transpose(0, 2, 1, 3)

    xp = x_prompt
    xs = x_sample.reshape(1, nbs, D_MODEL)
    zero_pool = jnp.zeros((nbp, POOL_BUF, POOL_W), F32)
    zero_pool_s = jnp.zeros((nbs, POOL_BUF, POOL_W), F32)
    zero_conv = jnp.zeros((nbp, CONV_W - 1, 3 * DN_W), F32)
    zero_state = jnp.zeros((nbp, DN_HEADS, DN_DK, DN_DK), F32)
    pad_rows = lambda a: jnp.pad(a.reshape(nbs, 1, a.shape[-1]), ((0, 0), (0, CHUNK - 1), (0, 0)))

    outs = {k: [] for k in ("kp", "vp", "ks", "vs", "pp", "ps", "cp", "cs", "sp", "ss")}
    for l in range(depth):
        n1 = norm1_w[l].reshape(1, D_MODEL)
        n2 = norm2_w[l].reshape(1, D_MODEL)
        ps = pool_scale[l].reshape(1, POOL_W)

        u, q, k, v, d, z, b4, a4 = _in_proj(xp, mod_p, l, False, n1, w_ext[l], qn4[l], kn4[l], ones_att, 512)
        y_pool, pool_new = _pool(u, zero_pool, pw_bd[l], ps, 0)
        y_att = _attn_prompt(q, k, v)
        y_dn, conv_new, s_new = _gdn(d, z, b4, a4, zero_conv, zero_state, conv_w[l], alog4[l], dtb4[l],
                                     dnw4[l], gdn_consts, seq, 8)
        xp = _out_ffn(xp, y_pool, y_att, y_dn, mod_p, l, False, n2, wo_b[l], wg_b[l], wu_b[l], wd_b[l], 512)
        outs["kp"].append(k.reshape(nbp, seq, ATT_HEADS, HEAD_DIM))
        outs["vp"].append(v.reshape(nbp, seq, ATT_HEADS, HEAD_DIM))
        outs["pp"].append(pool_new)
        outs["cp"].append(conv_new)
        outs["sp"].append(s_new)

        u, q, k, v, d, z, b4, a4 = _in_proj(xs, mod_s, l, True, n1, w_ext[l], qn4[l], kn4[l], ones_att, nbs)
        ext = jnp.concatenate([state_pool[l], u.reshape(nbs, 1, POOL_W)], axis=1)
        y_pool16, pool_new = _pool(ext, zero_pool_s, pw_bd[l], ps, past_len - POOL_BUF)
        y_pool = y_pool16[:, POOL_BUF:].reshape(1, nbs, POOL_W)
        o_att = _attn_sample(q.reshape(nbs, ATT_HEADS, HEAD_DIM).astype(F32),
                             k.reshape(nbs, ATT_HEADS, HEAD_DIM), v.reshape(nbs, ATT_HEADS, HEAD_DIM),
                             cache_kt, cache_vt, page_table, l)
        y_att = o_att.reshape(1, nbs, ATT_W)
        y_dn64, conv_new, s_new = _gdn(pad_rows(d), pad_rows(z), pad_rows(b4), pad_rows(a4),
                                       state_conv[l], state_delta[l], conv_w[l], alog4[l], dtb4[l],
                                       dnw4[l], gdn_consts, 1, 8)
        y_dn = y_dn64[:, 0].reshape(1, nbs, DN_W)
        xs = _out_ffn(xs, y_pool, y_att, y_dn, mod_s, l, True, n2, wo_b[l], wg_b[l], wu_b[l], wd_b[l], nbs)
        outs["ks"].append(k.reshape(nbs, 1, ATT_HEADS, HEAD_DIM))
        outs["vs"].append(v.reshape(nbs, 1, ATT_HEADS, HEAD_DIM))
        outs["ps"].append(pool_new)
        outs["cs"].append(conv_new)
        outs["ss"].append(s_new)

    st = lambda name: jnp.stack(outs[name])
    return (xp, xs.reshape(nbs, 1, D_MODEL), st("kp"), st("vp"), st("ks"), st("vs"),
            st("pp"), st("ps"), st("cp"), st("cs"), st("sp"), st("ss"))
```

```python
import functools

import numpy as np
import jax
import jax.numpy as jnp
from jax import lax
from jax.experimental import pallas as pl
from jax.experimental.pallas import tpu as pltpu

F32 = jnp.float32
BF16 = jnp.bfloat16

D_MODEL = 1024
PAGE = 128
POOL_WINDOWS = (2, 4, 8, 16)
POOL_GDIM = 64
POOL_W = 256
POOL_BUF = 15
HEAD_DIM = 64
ATT_HEADS = 8
ATT_W = 512
BLK = 256
TOPK = 3
DN_HEADS = 4
DN_DK = 64
DN_W = 256
CHUNK = 64
CONV_W = 4
D_FF = 2816
N_MOD = 6
EPS = 1e-6
NEG = -1e30
LOG2E = 1.4426950408889634

C_U = (0, 256)
C_Q = (256, 768)
C_K = (768, 1280)
C_V = (1280, 1792)
C_D = (1792, 2560)
C_Z = (2560, 2816)
C_B = (2816, 3072)
C_A = (3072, 3328)
IN_EXT = 3328

TM = 512
GDN_BT = 8
SA_PAGES_PER_STEP = 32
VMEM_LIMIT = 56 * 1024 * 1024


def _cparams(*sem):
    return pltpu.CompilerParams(dimension_semantics=sem, vmem_limit_bytes=VMEM_LIMIT)


def _sigmoid(x):
    return 1.0 / (1.0 + jnp.exp(-x))


def _bdot(a, b):
    return jnp.dot(a.astype(BF16), b.astype(BF16), preferred_element_type=F32)


def _bdot_t(a, b):
    return lax.dot_general(a.astype(BF16), b.astype(BF16), (((1,), (1,)), ((), ())),
                           preferred_element_type=F32)


def _split3(a):
    hi = a.astype(BF16)
    r1 = a - hi.astype(F32)
    mid = r1.astype(BF16)
    lo = (r1 - mid.astype(F32)).astype(BF16)
    return hi, mid, lo


def _dot3_rhs_exact(a, b01):
    return sum(jnp.dot(p, b01, preferred_element_type=F32) for p in _split3(a))


def _dot3_lhs_exact(a01, b):
    return sum(jnp.dot(a01, p, preferred_element_type=F32) for p in _split3(b))


def _dot3_t_rhs_exact(a, b01):
    return sum(lax.dot_general(p, b01, (((1,), (1,)), ((), ())), preferred_element_type=F32)
               for p in _split3(a))


def _dot3_t_lhs_exact(a01, b):
    return sum(lax.dot_general(a01, p, (((1,), (1,)), ((), ())), preferred_element_type=F32)
               for p in _split3(b))


def _mod_kernel(c_ref, w_ref, b_ref, o_ref):
    c = c_ref[...]
    a = (c * _sigmoid(c)).astype(BF16)
    o_ref[...] = jnp.dot(a, w_ref[...].astype(BF16), preferred_element_type=F32) + b_ref[...]


def _modulation(c_all, ada_w, ada_b):
    depth, _, ncol = ada_w.shape
    nseq = c_all.shape[0]
    tn = 1536
    return pl.pallas_call(
        _mod_kernel,
        grid=(depth, ncol // tn),
        in_specs=[pl.BlockSpec((nseq, D_MODEL), lambda l, j: (0, 0)),
                  pl.BlockSpec((None, D_MODEL, tn), lambda l, j: (l, 0, j)),
                  pl.BlockSpec((None, 1, tn), lambda l, j: (l, 0, j))],
        out_specs=pl.BlockSpec((None, nseq, tn), lambda l, j: (l, 0, j)),
        out_shape=jax.ShapeDtypeStruct((depth, nseq, ncol), F32),
        compiler_params=_cparams("arbitrary", "arbitrary"),
        name="modulation",
    )(c_all, ada_w, ada_b.reshape(depth, 1, ncol))


def _mod_spec(mod, l, k, per_token):
    if per_token:
        return pl.BlockSpec((None, None, mod.shape[2], D_MODEL), lambda *g: (l, k, 0, 0))
    return pl.BlockSpec((None, None, None, 1, D_MODEL), lambda *g: (l, g[0], k, 0, 0))


def _in_kernel(x_ref, sh_ref, sc_ref, n1_ref, w_ref, qn_ref, kn_ref, ones_ref, *rest):
    u_ref, q_ref, k_ref, v_ref, d_ref, z_ref, b_ref, a_ref = rest[-8:]
    x = x_ref[...]
    ms = jnp.mean(x * x, axis=-1, keepdims=True)
    h = (x * lax.rsqrt(ms + EPS) * n1_ref[...]) * (1.0 + sc_ref[...]) + sh_ref[...]
    hb = h.astype(BF16)

    def proj(c):
        return jnp.dot(hb, w_ref[:, c[0]:c[1]], preferred_element_type=F32)

    def head_norm(t, w4):
        ss = jnp.dot((t * t).astype(BF16), ones_ref[...], preferred_element_type=F32)
        return t * lax.rsqrt(ss * (1.0 / HEAD_DIM) + EPS) * w4

    u_ref[...] = proj(C_U)
    q_ref[...] = (head_norm(proj(C_Q), qn_ref[...]) * (HEAD_DIM ** -0.5 * LOG2E)).astype(BF16)
    k_ref[...] = head_norm(proj(C_K), kn_ref[...])
    v_ref[...] = proj(C_V)
    d_ref[...] = proj(C_D)
    z_ref[...] = proj(C_Z)
    b_ref[...] = proj(C_B)
    a_ref[...] = proj(C_A)


def _in_proj(x3, mod, l, per_token, n1, w_ext, qn4, kn4, ones_att, tm, kv_depth=None, kv_prev=None):
    nb, length, _ = x3.shape
    widths = (POOL_W, ATT_W, ATT_W, ATT_W, 3 * DN_W, DN_W, DN_W, DN_W)
    dtypes = (F32, BF16, F32, F32, F32, F32, F32, F32)
    const = lambda b, t: (0, 0)
    tok = lambda w: pl.BlockSpec((None, tm, w), lambda b, t: (b, t, 0))
    out_specs = [tok(w) for w in widths]
    out_shape = [jax.ShapeDtypeStruct((nb, length, w), dt) for w, dt in zip(widths, dtypes)]
    operands = [x3, mod, mod, n1, w_ext, qn4, kn4, ones_att]
    in_specs = [tok(D_MODEL), _mod_spec(mod, l, 0, per_token), _mod_spec(mod, l, 1, per_token),
                pl.BlockSpec((1, D_MODEL), const),
                pl.BlockSpec((D_MODEL, IN_EXT), const),
                pl.BlockSpec((1, ATT_W), const), pl.BlockSpec((1, ATT_W), const),
                pl.BlockSpec((ATT_W, ATT_W), const)]
    aliases = {}
    if kv_depth is not None:
        for o in (2, 3):
            out_specs[o] = pl.BlockSpec((None, None, tm, ATT_W), lambda b, t: (l, b, t, 0))
            out_shape[o] = jax.ShapeDtypeStruct((kv_depth, nb, length, ATT_W), F32)
        if kv_prev is not None:
            aliases = {len(operands): 2, len(operands) + 1: 3}
            operands += list(kv_prev)
            in_specs += [pl.BlockSpec(memory_space=pl.ANY)] * 2
    return pl.pallas_call(
        _in_kernel,
        grid=(nb, length // tm),
        in_specs=in_specs,
        out_specs=out_specs,
        out_shape=out_shape,
        input_output_aliases=aliases,
        compiler_params=_cparams("arbitrary", "arbitrary"),
        name="in_proj",
    )(*operands)


def _pool_kernel(u_ref, buf_ref, pw_ref, ps_ref, y_ref, new_ref, ext_ref, *, length, pos0):
    ext_ref[pl.ds(1, POOL_BUF), :] = buf_ref[...]
    ext_ref[pl.ds(16, length), :] = u_ref[...]
    lane = lax.broadcasted_iota(jnp.int32, (1, POOL_W), 1)
    grp = jnp.right_shift(lane, 6)
    wl = jnp.where(grp == 0, 2, jnp.where(grp == 1, 4, jnp.where(grp == 2, 8, 16)))
    ch = min(length, 256)
    row = lax.broadcasted_iota(jnp.int32, (ch, 1), 0)
    for c in range(length // ch):
        base = 16 + c * ch
        cur = ext_ref[pl.ds(base, ch), :]
        acc = cur
        sums = {}
        for i in range(1, 16):
            acc = acc + ext_ref[pl.ds(base - i, ch), :]
            if i + 1 in POOL_WINDOWS:
                sums[i + 1] = acc
        wsum = jnp.where(grp == 0, sums[2], jnp.where(grp == 1, sums[4],
                                                      jnp.where(grp == 2, sums[8], sums[16])))
        cnt = jnp.minimum(wl, row + (pos0 + c * ch + 1)).astype(F32)
        d = wsum / cnt - cur
        y = jnp.dot(d.astype(BF16), pw_ref[...], preferred_element_type=F32) * ps_ref[...]
        y_ref[pl.ds(c * ch, ch), :] = y.astype(y_ref.dtype)
    new_ref[...] = ext_ref[pl.ds(length + 1, POOL_BUF), :]


def _pool(u3, buf3, pw_bd, ps, pos0):
    nb, length, _ = u3.shape
    const = lambda b: (0, 0)
    return pl.pallas_call(
        functools.partial(_pool_kernel, length=length, pos0=pos0),
        grid=(nb,),
        in_specs=[pl.BlockSpec((None, length, POOL_W), lambda b: (b, 0, 0)),
                  pl.BlockSpec((None, POOL_BUF, POOL_W), lambda b: (b, 0, 0)),
                  pl.BlockSpec((POOL_W, POOL_W), const), pl.BlockSpec((1, POOL_W), const)],
        out_specs=[pl.BlockSpec((None, length, POOL_W), lambda b: (b, 0, 0)),
                   pl.BlockSpec((None, POOL_BUF, POOL_W), lambda b: (b, 0, 0))],
        out_shape=[jax.ShapeDtypeStruct((nb, length, POOL_W), BF16),
                   jax.ShapeDtypeStruct((nb, POOL_BUF, POOL_W), F32)],
        scratch_shapes=[pltpu.VMEM((16 + length, POOL_W), F32)],
        compiler_params=_cparams("arbitrary"),
        name="pool_mixer",
    )(u3, buf3, pw_bd, ps)


def _attn_kernel(q_ref, k_ref, v_ref, o_ref, km_scr, vt_scr, ot_scr, s_scr, p_scr, *, length):
    nb = length // BLK
    lane = lax.broadcasted_iota(jnp.int32, (1, 128), 1)
    k2 = k_ref[...]
    kmean = jnp.sum(k2.reshape(nb, BLK, 128), axis=1) * (1.0 / BLK)
    vt_scr[...] = v_ref[...].T.astype(BF16)
    krow = lax.broadcasted_iota(jnp.int32, (BLK, BLK), 0)
    qcol = lax.broadcasted_iota(jnp.int32, (BLK, BLK), 1)
    causal = krow <= qcol
    blkrow = lax.broadcasted_iota(jnp.int32, (nb, BLK), 0)
    pad_rows = jnp.zeros((16 - nb, 128), F32) if nb < 16 else None
    kmh = []
    for hh in range(2):
        hm = jnp.right_shift(lane, 6) == hh
        km_scr[hh] = jnp.where(hm, k2, 0.0).astype(BF16)
        t = jnp.where(hm, kmean, 0.0)
        kmh.append(t if pad_rows is None else jnp.concatenate([t, pad_rows], axis=0))

    def fold8(t):
        return t.reshape(BLK // 8, 8, BLK)

    def scores(hh, i, slot):
        qi = q_ref[pl.ds(i * BLK, BLK), :]
        bias = None
        if i > 0:
            gate = _bdot_t(kmh[hh], qi)[:nb]
            past = blkrow < i
            gate = jnp.where(past, gate, -jnp.inf)
            cnt = jnp.zeros((nb, BLK), jnp.int32)
            for jp in range(i):
                gj = gate[jp:jp + 1, :]
                cnt = cnt + jnp.where(gj > gate, 1,
                                      jnp.where(gj == gate, (jp < blkrow).astype(jnp.int32), 0))
            sel = jnp.where(past, cnt, TOPK) < TOPK
            bias = jnp.where(sel, 0.0, NEG)
        m8 = None
        for j in range(i + 1):
            s = _bdot_t(km_scr[hh, j * BLK:(j + 1) * BLK, :], qi)
            s = jnp.where(causal, s, NEG) if j == i else s + bias[j:j + 1, :]
            s_scr[slot, j] = s
            t = jnp.max(fold8(s), axis=0)
            m8 = t if m8 is None else jnp.maximum(m8, t)
        return jnp.max(m8, axis=0, keepdims=True)

    def weighted(hh, i, slot, m):
        l8 = None
        for j in range(i + 1):
            p = jnp.exp2(s_scr[slot, j] - m)
            t = jnp.sum(fold8(p), axis=0)
            l8 = t if l8 is None else l8 + t
            p_scr[slot, j * BLK:(j + 1) * BLK, :] = p.astype(BF16)
        acc = jnp.dot(vt_scr[hh * 64:(hh + 1) * 64, 0:(i + 1) * BLK], p_scr[slot, 0:(i + 1) * BLK, :],
                      preferred_element_type=F32)
        l = jnp.sum(l8, axis=0, keepdims=True)
        ot_scr[hh * 64:(hh + 1) * 64, i * BLK:(i + 1) * BLK] = acc / l

    items = [(hh, i) for hh in range(2) for i in range(nb)]
    pending = None
    for n, (hh, i) in enumerate(items):
        m = scores(hh, i, n % 2)
        if pending is not None:
            weighted(*pending)
        pending = (hh, i, n % 2, m)
    weighted(*pending)
    o_ref[...] = ot_scr[...].T.astype(o_ref.dtype)


def _attn_prompt(q3, k_all, v_all, l):
    nb, length, _ = q3.shape
    nblk = length // BLK
    spec = pl.BlockSpec((None, length, 128), lambda b, h: (b, 0, h))
    kv_spec = pl.BlockSpec((None, None, length, 128), lambda b, h: (l, b, 0, h))
    return pl.pallas_call(
        functools.partial(_attn_kernel, length=length),
        grid=(nb, ATT_W // 128),
        in_specs=[spec, kv_spec, kv_spec],
        out_specs=spec,
        out_shape=jax.ShapeDtypeStruct((nb, length, ATT_W), BF16),
        scratch_shapes=[pltpu.VMEM((2, length, 128), BF16),
                        pltpu.VMEM((128, length), BF16),
                        pltpu.VMEM((128, length), F32),
                        pltpu.VMEM((2, nblk, BLK, BLK), F32),
                        pltpu.VMEM((2, length, BLK), BF16)],
        compiler_params=_cparams("arbitrary", "arbitrary"),
        name="moba_prompt",
    )(q3, k_all, v_all)


def _sa_partials(kp, vp, qbc_ref, s_scr, g_scr, m_scr, l_scr, o_scr, first_blk):
    full = (ATT_HEADS, PAGE)
    for pg in range(len(kp)):
        for h in range(ATT_HEADS):
            s_scr[pg, pl.ds(h, 1), :] = jnp.sum(kp[pg][h] * qbc_ref[h], axis=0, keepdims=True)
    for blk in range(len(kp) // 2):
        s0 = s_scr[2 * blk]
        s1 = s_scr[2 * blk + 1]
        gsum = jnp.sum(s0, axis=-1, keepdims=True) + jnp.sum(s1, axis=-1, keepdims=True)
        mb = jnp.maximum(jnp.max(s0, axis=-1, keepdims=True), jnp.max(s1, axis=-1, keepdims=True))
        p0 = jnp.exp2(s0 - mb)
        p1 = jnp.exp2(s1 - mb)
        lb = jnp.sum(p0, axis=-1, keepdims=True) + jnp.sum(p1, axis=-1, keepdims=True)
        s_scr[2 * blk] = p0
        s_scr[2 * blk + 1] = p1
        idx = first_blk + blk
        g_scr[idx] = jnp.broadcast_to(gsum, full)
        m_scr[idx] = jnp.broadcast_to(mb, full)
        l_scr[idx] = jnp.broadcast_to(lb, full)
        for h in range(ATT_HEADS):
            o_scr[idx, h] = (vp[2 * blk][h] * s_scr[2 * blk, pl.ds(h, 1), :]
                             + vp[2 * blk + 1][h] * s_scr[2 * blk + 1, pl.ds(h, 1), :])


def _sa_merge(q_ref, kn_ref, vn_ref, o_ref, g_scr, m_scr, l_scr, o_scr, w_scr, nblk):
    full = (ATT_HEADS, PAGE)
    gates = g_scr[pl.ds(0, nblk)]
    if nblk > TOPK:
        lane = lax.broadcasted_iota(jnp.int32, full, 1)
        gates_c = jnp.zeros(full, F32)
        for b in range(nblk):
            gates_c = jnp.where(lane == b, gates[b], gates_c)
        cnt = jnp.zeros(full, jnp.int32)
        for b in range(nblk):
            cnt = cnt + jnp.where(gates[b] > gates_c, 1,
                                  jnp.where(gates[b] == gates_c, (b < lane).astype(jnp.int32), 0))
        last = jnp.where(lane < nblk, cnt, -1) == TOPK - 1
        g3 = jnp.max(jnp.where(last, gates_c, -jnp.inf), axis=-1, keepdims=True)
        b3 = jnp.max(jnp.where(last, lane.astype(F32), -1.0), axis=-1, keepdims=True)
        bidx = lax.broadcasted_iota(jnp.int32, gates.shape, 0).astype(F32)
        sel = (gates > g3[None]) | ((gates == g3[None]) & (bidx <= b3[None]))
    else:
        sel = jnp.full(gates.shape, True)
    sself = jnp.broadcast_to(jnp.sum(q_ref[...] * kn_ref[...], axis=-1, keepdims=True), full)
    mb = m_scr[pl.ds(0, nblk)]
    mtot = jnp.maximum(jnp.max(jnp.where(sel, mb, -jnp.inf), axis=0), sself)
    w = jnp.where(sel, jnp.exp2(jnp.where(sel, mb - mtot[None], 0.0)), 0.0)
    wself = jnp.exp2(sself - mtot)
    denom = jnp.sum(w * l_scr[pl.ds(0, nblk)], axis=0) + wself
    w_scr[...] = w
    ones8 = jnp.ones((8, PAGE), BF16)
    sub = lax.broadcasted_iota(jnp.int32, (ATT_HEADS, HEAD_DIM), 0)
    o_acc = jnp.zeros((ATT_HEADS, HEAD_DIM), F32)
    for h in range(ATT_HEADS):
        tot = w_scr[0, pl.ds(h, 1), :] * o_scr[0, h]
        for b in range(1, nblk):
            tot = tot + w_scr[b, pl.ds(h, 1), :] * o_scr[b, h]
        r = _dot3_t_lhs_exact(ones8, tot)
        o_acc = jnp.where(sub == h, r, o_acc)
    o_ref[...] = (o_acc + wself[:, :HEAD_DIM] * vn_ref[...]) / denom[:, :HEAD_DIM]


def _sa_kernel(pt_ref, qbc_ref, q_ref, kn_ref, vn_ref, *refs, gpages, n_steps):
    kp = refs[:gpages]
    vp = refs[gpages:2 * gpages]
    o_ref = refs[2 * gpages]
    g_scr, m_scr, l_scr, o_scr, s_scr, w_scr = refs[2 * gpages + 1:]
    step = pl.program_id(1)
    _sa_partials(kp, vp, qbc_ref, s_scr, g_scr, m_scr, l_scr, o_scr, step * (gpages // 2))

    @pl.when(step == n_steps - 1)
    def _merge():
        _sa_merge(q_ref, kn_ref, vn_ref, o_ref, g_scr, m_scr, l_scr, o_scr, w_scr, n_steps * (gpages // 2))


def _attn_sample(q3, kn3, vn3, cache_kt, cache_vt, page_table, l):
    nb = q3.shape[0]
    n_pages = page_table.shape[1]
    gpages = SA_PAGES_PER_STEP
    n_steps = n_pages // gpages
    nblk = n_pages // 2
    qbc = jnp.broadcast_to(q3[..., None], (nb, ATT_HEADS, HEAD_DIM, PAGE))
    tok = lambda: pl.BlockSpec((None, ATT_HEADS, HEAD_DIM), lambda b, s, pt: (b, 0, 0))

    def page_spec(i):
        return pl.BlockSpec((None, None, ATT_HEADS, HEAD_DIM, PAGE),
                            lambda b, s, pt: (l, pt[b, s * gpages + i], 0, 0, 0))

    grid_spec = pltpu.PrefetchScalarGridSpec(
        num_scalar_prefetch=1,
        grid=(nb, n_steps),
        in_specs=[pl.BlockSpec((None, ATT_HEADS, HEAD_DIM, PAGE), lambda b, s, pt: (b, 0, 0, 0)),
                  tok(), tok(), tok()]
                 + [page_spec(i) for i in range(gpages)] + [page_spec(i) for i in range(gpages)],
        out_specs=tok(),
        scratch_shapes=[pltpu.VMEM((nblk, ATT_HEADS, PAGE), F32),
                        pltpu.VMEM((nblk, ATT_HEADS, PAGE), F32),
                        pltpu.VMEM((nblk, ATT_HEADS, PAGE), F32),
                        pltpu.VMEM((nblk, ATT_HEADS, HEAD_DIM, PAGE), F32),
                        pltpu.VMEM((gpages, ATT_HEADS, PAGE), F32),
                        pltpu.VMEM((nblk, ATT_HEADS, PAGE), F32)],
    )
    return pl.pallas_call(
        functools.partial(_sa_kernel, gpages=gpages, n_steps=n_steps),
        grid_spec=grid_spec,
        out_shape=jax.ShapeDtypeStruct((nb, ATT_HEADS, HEAD_DIM), F32),
        compiler_params=_cparams("arbitrary", "arbitrary"),
        name="moba_sample",
    )(page_table, qbc, q3, kn3, vn3, *([cache_kt] * gpages), *([cache_vt] * gpages))


def _gdn_constants():
    C = CHUNK
    lane = np.arange(DN_W)
    lane_j = lane % 64
    row = np.arange(C)[:, None]
    tri = np.stack([lane_j[None] <= row, lane_j[None] < row, lane_j[None] == row, row <= lane_j[None]])
    l_incl = np.arange(C)[None, :] <= np.arange(C)[:, None]
    blk = np.arange(DN_W) // 64
    bd = blk[:, None] == blk[None, :]
    place = np.stack([np.arange(DN_W)[None, :] == (np.arange(DN_DK)[:, None] + 64 * h) for h in range(DN_HEADS)])
    half = np.stack([np.broadcast_to((np.arange(128) // 64) == s, (C, 128)) for s in range(2)])
    return (jnp.asarray(tri, F32), jnp.asarray(l_incl, BF16), jnp.asarray(bd, BF16), jnp.asarray(bd, F32),
            jnp.asarray(place, BF16), jnp.asarray(half, F32))


def _gdn_kernel(d_ref, z_ref, b_ref, a_ref, c0_ref, s0_ref, cw_ref, alog_ref, dtb_ref, dnw_ref,
                tri_ref, lincl_ref, onesbd_ref, bdmask_ref, place_ref, half_ref,
                y_ref, cout_ref, sout_ref, ext_scr, s_scr, *, bt, n_chunks, l_valid):
    n = pl.program_id(1)
    C = CHUNK
    incl4 = tri_ref[0] > 0.5
    strict4 = tri_ref[1] > 0.5
    eye4 = tri_ref[2]
    ut4 = tri_ref[3]
    l_incl = lincl_ref[...]
    ones_bd = onesbd_ref[...]
    zero_half = jnp.zeros((C, 128), F32)
    left = half_ref[0] > 0.5
    right = half_ref[1] > 0.5

    def stack_mask(a4):
        lo = a4[:, :128]
        hi = a4[:, 128:]
        return jnp.concatenate(
            [jnp.concatenate([jnp.where(left, lo, 0.0), zero_half], axis=1),
             jnp.concatenate([jnp.where(right, lo, 0.0), zero_half], axis=1),
             jnp.concatenate([zero_half, jnp.where(left, hi, 0.0)], axis=1),
             jnp.concatenate([zero_half, jnp.where(right, hi, 0.0)], axis=1)], axis=0).astype(BF16)

    @pl.when(n == 0)
    def _init():
        for bi in range(bt):
            ext_scr[bi, pl.ds(5, CONV_W - 1), :] = c0_ref[bi]
            s_scr[bi] = jnp.concatenate(
                [_dot3_rhs_exact(s0_ref[bi, h], place_ref[h]) for h in range(DN_HEADS)], axis=0)

    last_valid = l_valid - (n_chunks - 1) * C
    masked = last_valid != C
    if masked:
        row = lax.broadcasted_iota(jnp.int32, (C, 1), 0)
        valid = (n * C + row) < l_valid

    seqs = range(bt)
    each = lambda f, *cols: [f(*xs) for xs in zip(*cols)]
    cw = cw_ref[...]
    act = []
    for bi in seqs:
        ext_scr[bi, pl.ds(8, C), :] = d_ref[bi]
        yc = (cw[0:1] * ext_scr[bi, pl.ds(5, C), :] + cw[1:2] * ext_scr[bi, pl.ds(6, C), :]
              + cw[2:3] * ext_scr[bi, pl.ds(7, C), :] + cw[3:4] * ext_scr[bi, pl.ds(8, C), :])
        act.append(yc * _sigmoid(yc))
        cout_ref[bi] = ext_scr[bi, pl.ds(last_valid + 5, CONV_W - 1), :]
        ext_scr[bi, pl.ds(5, CONV_W - 1), :] = ext_scr[bi, pl.ds(8 + C - (CONV_W - 1), CONV_W - 1), :]

    q_raw = [a[:, 0:DN_W] for a in act]
    k_raw = [a[:, DN_W:2 * DN_W] for a in act]
    v4 = [a[:, 2 * DN_W:3 * DN_W] for a in act]
    ssq = each(lambda x: _bdot(x * x, ones_bd), q_raw)
    ssk = each(lambda x: _bdot(x * x, ones_bd), k_raw)
    q4 = each(lambda x, s: x * lax.rsqrt(s + EPS) * (DN_DK ** -0.5), q_raw, ssq)
    k4 = each(lambda x, s: x * lax.rsqrt(s + EPS), k_raw, ssk)
    beta4 = [_sigmoid(b_ref[bi]) for bi in seqs]

    def log_decay(bi):
        xg = a_ref[bi] + dtb_ref[...]
        return -jnp.exp(alog_ref[...]) * (jnp.maximum(xg, 0.0) + jnp.log1p(jnp.exp(-jnp.abs(xg))))

    g4 = [log_decay(bi) for bi in seqs]
    if masked:
        zero_pad = lambda x: jnp.where(valid, x, 0.0)
        q4, k4, v4, beta4, g4 = (each(zero_pad, c) for c in (q4, k4, v4, beta4, g4))

    gi = each(lambda g: _dot3_lhs_exact(l_incl, g), g4)
    gj = each(lambda g: jnp.sum(g * ut4, axis=0, keepdims=True), g4)
    decay4 = each(lambda a, b: jnp.where(incl4, jnp.exp(jnp.where(incl4, a - b, 0.0)), 0.0), gi, gj)
    eg4 = each(jnp.exp, gi)
    glast = [g[C - 1:C, :] for g in gi]
    kfac = each(lambda a, b: jnp.exp(a - b), glast, gi)
    gtot = each(jnp.exp, glast)

    kb4 = each(lambda a, b: a * b, k4, beta4)
    mk = each(lambda kb, q, k: _bdot_t(jnp.concatenate([kb, q], axis=0), stack_mask(k)), kb4, q4, k4)
    attn4 = each(lambda r, dc: r[C:] * dc, mk, decay4)

    p = each(lambda r, dc: -jnp.where(strict4, r[:C] * dc, 0.0), mk, decay4)
    t = each(lambda x: eye4 + x, p)
    p = each(lambda x: _bdot(x, stack_mask(x)), p)
    for _ in range(4):
        r = each(lambda a, b: _bdot(jnp.concatenate([a, b], axis=0), stack_mask(b)), t, p)
        t = each(lambda a, b: a + b[:C], t, r)
        p = [x[C:] for x in r]
    t = each(lambda a, b: a + _bdot(a, stack_mask(b)), t, p)

    u4 = each(lambda a, v, b: _bdot(a, stack_mask(v * b)), t, v4, beta4)
    kc4 = each(lambda a, kb, e: _bdot(a, stack_mask(kb * e)), t, kb4, eg4)

    sbd = [s_scr[bi] for bi in seqs]
    r = each(lambda kc, q, e, s: _bdot(jnp.concatenate([kc, q * e], axis=0), s), kc4, q4, eg4, sbd)
    vnew = each(lambda u, x: u - x[:C], u4, r)
    o4 = each(lambda x, a, v: x[C:] + _bdot(a, stack_mask(v)), r, attn4, vnew)
    upd = each(lambda k, f, v: lax.dot_general((k * f).astype(BF16), v.astype(BF16), (((0,), (0,)), ((), ())),
                                               preferred_element_type=F32), k4, kfac, vnew)
    sso = each(lambda o: _bdot(o * o, ones_bd), o4)
    for bi in seqs:
        s_scr[bi] = sbd[bi] * gtot[bi] + upd[bi] * bdmask_ref[...]
        zz = z_ref[bi]
        y_ref[bi] = (o4[bi] * lax.rsqrt(sso[bi] * (1.0 / DN_DK) + EPS) * dnw_ref[...]
                     * (zz * _sigmoid(zz))).astype(y_ref.dtype)

    @pl.when(n == n_chunks - 1)
    def _fin():
        for bi in range(bt):
            for h in range(DN_HEADS):
                sout_ref[bi, h] = _dot3_t_rhs_exact(s_scr[bi, h * 64:(h + 1) * 64, :], place_ref[h])


def _gdn(d3, z3, b3, a3, conv0, s0, cw, alog4, dtb4, dnw4, consts, l_valid, bt):
    nb, length, _ = d3.shape
    n_chunks = length // CHUNK
    tok = lambda w: pl.BlockSpec((bt, CHUNK, w), lambda i, n: (i, n, 0))
    const = lambda i, n: (0, 0)
    const3 = lambda i, n: (0, 0, 0)
    return pl.pallas_call(
        functools.partial(_gdn_kernel, bt=bt, n_chunks=n_chunks, l_valid=l_valid),
        grid=(nb // bt, n_chunks),
        in_specs=[tok(3 * DN_W), tok(DN_W), tok(DN_W), tok(DN_W),
                  pl.BlockSpec((bt, CONV_W - 1, 3 * DN_W), lambda i, n: (i, 0, 0)),
                  pl.BlockSpec((bt, DN_HEADS, DN_DK, DN_DK), lambda i, n: (i, 0, 0, 0)),
                  pl.BlockSpec((CONV_W, 3 * DN_W), const),
                  pl.BlockSpec((1, DN_W), const), pl.BlockSpec((1, DN_W), const),
                  pl.BlockSpec((1, DN_W), const),
                  pl.BlockSpec((4, CHUNK, DN_W), const3),
                  pl.BlockSpec((CHUNK, CHUNK), const),
                  pl.BlockSpec((DN_W, DN_W), const), pl.BlockSpec((DN_W, DN_W), const),
                  pl.BlockSpec((DN_HEADS, DN_DK, DN_W), const3),
                  pl.BlockSpec((2, CHUNK, 128), const3)],
        out_specs=[tok(DN_W),
                   pl.BlockSpec((bt, CONV_W - 1, 3 * DN_W), lambda i, n: (i, 0, 0)),
                   pl.BlockSpec((bt, DN_HEADS, DN_DK, DN_DK), lambda i, n: (i, 0, 0, 0))],
        out_shape=[jax.ShapeDtypeStruct((nb, length, DN_W), BF16),
                   jax.ShapeDtypeStruct((nb, CONV_W - 1, 3 * DN_W), F32),
                   jax.ShapeDtypeStruct((nb, DN_HEADS, DN_DK, DN_DK), F32)],
        scratch_shapes=[pltpu.VMEM((bt, 8 + CHUNK, 3 * DN_W), F32),
                        pltpu.VMEM((bt, DN_W, DN_W), F32)],
        compiler_params=_cparams("arbitrary", "arbitrary"),
        name="gated_delta",
    )(d3, z3, b3, a3, conv0, s0, cw, alog4, dtb4, dnw4, *consts)


FF_CHUNK = D_FF // 2


def _mix_residual(x_ref, yp_ref, ya_ref, yd_ref, g1_ref, sh2_ref, sc2_ref, n2_ref, wo_ref):
    mix = jnp.concatenate([yp_ref[...].astype(BF16), ya_ref[...].astype(BF16), yd_ref[...].astype(BF16)],
                          axis=1)
    x1 = x_ref[...] + g1_ref[...] * jnp.dot(mix, wo_ref[...], preferred_element_type=F32)
    ms = jnp.mean(x1 * x1, axis=-1, keepdims=True)
    h2 = ((x1 * lax.rsqrt(ms + EPS) * n2_ref[...]) * (1.0 + sc2_ref[...]) + sh2_ref[...]).astype(BF16)
    return x1, h2


def _out_kernel(x_ref, yp_ref, ya_ref, yd_ref, g1_ref, sh2_ref, sc2_ref, g2_ref, n2_ref,
                wo_ref, wg_ref, wu_ref, wd_ref, o_ref):
    x1, h2 = _mix_residual(x_ref, yp_ref, ya_ref, yd_ref, g1_ref, sh2_ref, sc2_ref, n2_ref, wo_ref)
    acc = None
    for c in range(D_FF // FF_CHUNK):
        c0 = c * FF_CHUNK
        gt = jnp.dot(h2, wg_ref[:, c0:c0 + FF_CHUNK], preferred_element_type=F32)
        up = jnp.dot(h2, wu_ref[:, c0:c0 + FF_CHUNK], preferred_element_type=F32)
        act = (gt * _sigmoid(gt) * up).astype(BF16)
        part = jnp.dot(act, wd_ref[c0:c0 + FF_CHUNK, :], preferred_element_type=F32)
        acc = part if acc is None else acc + part
    o_ref[...] = x1 + g2_ref[...] * acc


def _out_ffn(x3, yp, ya, yd, mod, l, per_token, n2, wo, wg, wu, wd, tm):
    nb, length, _ = x3.shape
    tok = lambda w: pl.BlockSpec((None, tm, w), lambda b, t: (b, t, 0))
    const = lambda b, t: (0, 0)
    single = pl.Buffered(1)
    return pl.pallas_call(
        _out_kernel,
        grid=(nb, length // tm),
        in_specs=[tok(D_MODEL), tok(POOL_W), tok(ATT_W), tok(DN_W),
                  _mod_spec(mod, l, 2, per_token), _mod_spec(mod, l, 3, per_token),
                  _mod_spec(mod, l, 4, per_token), _mod_spec(mod, l, 5, per_token),
                  pl.BlockSpec((1, D_MODEL), const),
                  pl.BlockSpec((D_MODEL, D_MODEL), const, pipeline_mode=single),
                  pl.BlockSpec((D_MODEL, D_FF), const, pipeline_mode=single),
                  pl.BlockSpec((D_MODEL, D_FF), const, pipeline_mode=single),
                  pl.BlockSpec((D_FF, D_MODEL), const, pipeline_mode=single)],
        out_specs=tok(D_MODEL),
        out_shape=jax.ShapeDtypeStruct((nb, length, D_MODEL), F32),
        compiler_params=_cparams("arbitrary", "arbitrary"),
        name="out_ffn",
    )(x3, yp, ya, yd, mod, mod, mod, mod, n2, wo, wg, wu, wd)


def kernel(x_prompt, x_sample, cache_k, cache_v, state_pool, state_conv, state_delta, page_table,
           c_prompt, c_sample, norm1_w, ada_w, ada_b, w_in, pool_w, pool_scale, q_norm_w, k_norm_w,
           conv_w, a_log, dt_bias, dn_norm_w, w_out, norm2_w, w_gate, w_up, w_down):
    depth = w_in.shape[0]
    nbp, seq, _ = x_prompt.shape
    nbs = x_sample.shape[0]
    past_len = page_table.shape[1] * PAGE

    n_main = C_Z[1]
    w_ext = jnp.concatenate(
        [w_in[:, :, :n_main],
         jnp.repeat(w_in[:, :, n_main:n_main + DN_HEADS], DN_DK, axis=-1),
         jnp.repeat(w_in[:, :, n_main + DN_HEADS:], DN_DK, axis=-1)], axis=-1).astype(BF16)
    wo_b, wg_b, wu_b, wd_b = (w.astype(BF16) for w in (w_out, w_gate, w_up, w_down))
    eye_g = jnp.eye(len(POOL_WINDOWS), dtype=F32)
    pw_bd = (eye_g[None, :, None, :, None] * pool_w[:, :, :, None, :]).reshape(depth, POOL_W, POOL_W).astype(BF16)
    ones_att = jnp.kron(jnp.eye(ATT_HEADS, dtype=F32), jnp.ones((HEAD_DIM, HEAD_DIM), F32)).astype(BF16)
    qn4 = jnp.tile(q_norm_w, (1, ATT_HEADS)).reshape(depth, 1, ATT_W)
    kn4 = jnp.tile(k_norm_w, (1, ATT_HEADS)).reshape(depth, 1, ATT_W)
    dnw4 = jnp.tile(dn_norm_w, (1, DN_HEADS)).reshape(depth, 1, DN_W)
    alog4 = jnp.repeat(a_log, DN_DK, axis=-1).reshape(depth, 1, DN_W)
    dtb4 = jnp.repeat(dt_bias, DN_DK, axis=-1).reshape(depth, 1, DN_W)
    gdn_consts = _gdn_constants()
    cache_kt = jnp.transpose(cache_k, (0, 1, 3, 4, 2))
    cache_vt = jnp.transpose(cache_v, (0, 1, 3, 4, 2))

    mod = _modulation(jnp.concatenate([c_prompt, c_sample], axis=0), ada_w, ada_b)
    mod_p = mod[:, :nbp].reshape(depth, nbp, N_MOD, 1, D_MODEL)
    mod_s = mod[:, nbp:].reshape(depth, nbs, N_MOD, D_MODEL).transpose(0, 2, 1, 3)

    xp = x_prompt
    xs = x_sample.reshape(1, nbs, D_MODEL)
    zero_pool = jnp.zeros((nbp, POOL_BUF, POOL_W), F32)
    zero_pool_s = jnp.zeros((nbs, POOL_BUF, POOL_W), F32)
    zero_conv = jnp.zeros((nbp, CONV_W - 1, 3 * DN_W), F32)
    zero_state = jnp.zeros((nbp, DN_HEADS, DN_DK, DN_DK), F32)
    pad_rows = lambda a: jnp.pad(a.reshape(nbs, 1, a.shape[-1]), ((0, 0), (0, CHUNK - 1), (0, 0)))

    outs = {k: [] for k in ("ks", "vs", "pp", "ps", "cp", "cs", "sp", "ss")}
    kv_all = None
    for l in range(depth):
        n1 = norm1_w[l].reshape(1, D_MODEL)
        n2 = norm2_w[l].reshape(1, D_MODEL)
        ps = pool_scale[l].reshape(1, POOL_W)

        u, q, k_all, v_all, d, z, b4, a4 = _in_proj(xp, mod_p, l, False, n1, w_ext[l], qn4[l], kn4[l], ones_att,
                                                    2 * TM, kv_depth=depth, kv_prev=kv_all)
        kv_all = (k_all, v_all)
        y_pool, pool_new = _pool(u, zero_pool, pw_bd[l], ps, 0)
        y_att = _attn_prompt(q, k_all, v_all, l)
        y_dn, conv_new, s_new = _gdn(d, z, b4, a4, zero_conv, zero_state, conv_w[l], alog4[l], dtb4[l],
                                     dnw4[l], gdn_consts, seq, GDN_BT)
        xp = _out_ffn(xp, y_pool, y_att, y_dn, mod_p, l, False, n2, wo_b[l], wg_b[l], wu_b[l], wd_b[l], TM)
        outs["pp"].append(pool_new)
        outs["cp"].append(conv_new)
        outs["sp"].append(s_new)

        us, qs, ks, vs, ds, zs, b4s, a4s = _in_proj(xs, mod_s, l, True, n1, w_ext[l], qn4[l], kn4[l],
                                                    ones_att, nbs)
        ext = jnp.concatenate([state_pool[l], us.reshape(nbs, 1, POOL_W)], axis=1)
        y_pool16, pool_new = _pool(ext, zero_pool_s, pw_bd[l], ps, past_len - POOL_BUF)
        y_pool = y_pool16[:, POOL_BUF:].reshape(1, nbs, POOL_W)
        o_att = _attn_sample(qs.reshape(nbs, ATT_HEADS, HEAD_DIM).astype(F32),
                             ks.reshape(nbs, ATT_HEADS, HEAD_DIM), vs.reshape(nbs, ATT_HEADS, HEAD_DIM),
                             cache_kt, cache_vt, page_table, l)
        y_att = o_att.reshape(1, nbs, ATT_W)
        y_dn64, conv_new, s_new = _gdn(pad_rows(ds), pad_rows(zs), pad_rows(b4s), pad_rows(a4s),
                                       state_conv[l], state_delta[l], conv_w[l], alog4[l], dtb4[l],
                                       dnw4[l], gdn_consts, 1, GDN_BT)
        y_dn = y_dn64[:, 0].reshape(1, nbs, DN_W)
        xs = _out_ffn(xs, y_pool, y_att, y_dn, mod_s, l, True, n2, wo_b[l], wg_b[l], wu_b[l], wd_b[l], nbs)
        outs["ks"].append(ks.reshape(nbs, 1, ATT_HEADS, HEAD_DIM))
        outs["vs"].append(vs.reshape(nbs, 1, ATT_HEADS, HEAD_DIM))
        outs["ps"].append(pool_new)
        outs["cs"].append(conv_new)
        outs["ss"].append(s_new)

    st = lambda name: jnp.stack(outs[name])
    k_prompt, v_prompt = (a.reshape(depth, nbp, seq, ATT_HEADS, HEAD_DIM) for a in kv_all)
    return (xp, xs.reshape(nbs, 1, D_MODEL), k_prompt, v_prompt, st("ks"), st("vs"),
            st("pp"), st("ps"), st("cp"), st("cs"), st("sp"), st("ss"))
```

```python
import functools

import numpy as np
import jax
import jax.numpy as jnp
from jax import lax
from jax.experimental import pallas as pl
from jax.experimental.pallas import tpu as pltpu

F32 = jnp.float32
BF16 = jnp.bfloat16

D_MODEL = 1024
PAGE = 128
POOL_WINDOWS = (2, 4, 8, 16)
POOL_GDIM = 64
POOL_W = 256
POOL_BUF = 15
HEAD_DIM = 64
ATT_HEADS = 8
ATT_W = 512
BLK = 256
TOPK = 3
DN_HEADS = 4
DN_DK = 64
DN_W = 256
CHUNK = 64
SAMPLE_CHUNK = 16
CONV_W = 4
D_FF = 2816
N_MOD = 6
EPS = 1e-6
NEG = -1e30
LOG2E = 1.4426950408889634

C_U = (0, 256)
C_Q = (256, 768)
C_K = (768, 1280)
C_V = (1280, 1792)
C_D = (1792, 2560)
C_Z = (2560, 2816)
C_B = (2816, 3072)
C_A = (3072, 3328)
IN_EXT = 3328

TM = 512
GDN_BT = 8
SA_PAGES_PER_STEP = 32
VMEM_LIMIT = 56 * 1024 * 1024


def _cparams(*sem):
    return pltpu.CompilerParams(dimension_semantics=sem, vmem_limit_bytes=VMEM_LIMIT)


def _sigmoid(x):
    return 1.0 / (1.0 + jnp.exp(-x))


def _bdot(a, b):
    return jnp.dot(a.astype(BF16), b.astype(BF16), preferred_element_type=F32)


def _bdot_t(a, b):
    return lax.dot_general(a.astype(BF16), b.astype(BF16), (((1,), (1,)), ((), ())),
                           preferred_element_type=F32)


def _split3(a):
    hi = a.astype(BF16)
    r1 = a - hi.astype(F32)
    mid = r1.astype(BF16)
    lo = (r1 - mid.astype(F32)).astype(BF16)
    return hi, mid, lo


def _dot3_rhs_exact(a, b01):
    return sum(jnp.dot(p, b01, preferred_element_type=F32) for p in _split3(a))


def _dot3_lhs_exact(a01, b):
    return sum(jnp.dot(a01, p, preferred_element_type=F32) for p in _split3(b))


def _dot3_t_rhs_exact(a, b01):
    return sum(lax.dot_general(p, b01, (((1,), (1,)), ((), ())), preferred_element_type=F32)
               for p in _split3(a))


def _dot3_t_lhs_exact(a01, b):
    return sum(lax.dot_general(a01, p, (((1,), (1,)), ((), ())), preferred_element_type=F32)
               for p in _split3(b))


def _mod_kernel(c_ref, w_ref, b_ref, o_ref):
    c = c_ref[...]
    a = (c * _sigmoid(c)).astype(BF16)
    o_ref[...] = jnp.dot(a, w_ref[...].astype(BF16), preferred_element_type=F32) + b_ref[...]


def _modulation(c_all, ada_w, ada_b):
    depth, _, ncol = ada_w.shape
    nseq = c_all.shape[0]
    tn = 1536
    return pl.pallas_call(
        _mod_kernel,
        grid=(depth, ncol // tn),
        in_specs=[pl.BlockSpec((nseq, D_MODEL), lambda l, j: (0, 0)),
                  pl.BlockSpec((None, D_MODEL, tn), lambda l, j: (l, 0, j)),
                  pl.BlockSpec((None, 1, tn), lambda l, j: (l, 0, j))],
        out_specs=pl.BlockSpec((None, nseq, tn), lambda l, j: (l, 0, j)),
        out_shape=jax.ShapeDtypeStruct((depth, nseq, ncol), F32),
        compiler_params=_cparams("arbitrary", "arbitrary"),
        name="modulation",
    )(c_all, ada_w, ada_b.reshape(depth, 1, ncol))


def _layer_spec(stacked, l, **kw):
    zeros = (0,) * (stacked.ndim - 1)
    return pl.BlockSpec((None,) + stacked.shape[1:], lambda *g: (l,) + zeros, **kw)


def _mod_spec(mod, l, k, per_token):
    if per_token:
        return pl.BlockSpec((None, None, mod.shape[2], D_MODEL), lambda *g: (l, k, 0, 0))
    return pl.BlockSpec((None, None, None, 1, D_MODEL), lambda *g: (l, g[0], k, 0, 0))


def _in_kernel(x_ref, sh_ref, sc_ref, n1_ref, w_ref, qn_ref, kn_ref, ones_ref, *rest):
    u_ref, q_ref, k_ref, v_ref, d_ref, z_ref, b_ref, a_ref = rest[-8:]
    x = x_ref[...]
    ms = jnp.mean(x * x, axis=-1, keepdims=True)
    h = (x * lax.rsqrt(ms + EPS) * n1_ref[...]) * (1.0 + sc_ref[...]) + sh_ref[...]
    hb = h.astype(BF16)

    def proj(c):
        return jnp.dot(hb, w_ref[:, c[0]:c[1]], preferred_element_type=F32)

    def head_norm(t, w4):
        ss = jnp.dot((t * t).astype(BF16), ones_ref[...], preferred_element_type=F32)
        return t * lax.rsqrt(ss * (1.0 / HEAD_DIM) + EPS) * w4

    u_ref[...] = proj(C_U)
    q_ref[...] = (head_norm(proj(C_Q), qn_ref[...]) * (HEAD_DIM ** -0.5 * LOG2E)).astype(BF16)
    k_ref[...] = head_norm(proj(C_K), kn_ref[...])
    v_ref[...] = proj(C_V)
    d_ref[...] = proj(C_D)
    z_ref[...] = proj(C_Z)
    b_ref[...] = proj(C_B)
    a_ref[...] = proj(C_A)


def _in_proj(x3, mod, l, per_token, n1, w_ext, qn4, kn4, ones_att, tm, kv_depth=None, kv_prev=None):
    nb, length, _ = x3.shape
    widths = (POOL_W, ATT_W, ATT_W, ATT_W, 3 * DN_W, DN_W, DN_W, DN_W)
    dtypes = (F32, BF16, F32, F32, F32, F32, F32, F32)
    const = lambda b, t: (0, 0)
    tok = lambda w: pl.BlockSpec((None, tm, w), lambda b, t: (b, t, 0))
    out_specs = [tok(w) for w in widths]
    out_shape = [jax.ShapeDtypeStruct((nb, length, w), dt) for w, dt in zip(widths, dtypes)]
    operands = [x3, mod, mod, n1, w_ext, qn4, kn4, ones_att]
    in_specs = [tok(D_MODEL), _mod_spec(mod, l, 0, per_token), _mod_spec(mod, l, 1, per_token),
                _layer_spec(n1, l), _layer_spec(w_ext, l), _layer_spec(qn4, l), _layer_spec(kn4, l),
                pl.BlockSpec((ATT_W, ATT_W), const)]
    aliases = {}
    if kv_depth is not None:
        for o in (2, 3):
            out_specs[o] = pl.BlockSpec((None, None, tm, ATT_W), lambda b, t: (l, b, t, 0))
            out_shape[o] = jax.ShapeDtypeStruct((kv_depth, nb, length, ATT_W), F32)
        if kv_prev is not None:
            aliases = {len(operands): 2, len(operands) + 1: 3}
            operands += list(kv_prev)
            in_specs += [pl.BlockSpec(memory_space=pl.ANY)] * 2
    return pl.pallas_call(
        _in_kernel,
        grid=(nb, length // tm),
        in_specs=in_specs,
        out_specs=out_specs,
        out_shape=out_shape,
        input_output_aliases=aliases,
        compiler_params=_cparams("arbitrary", "arbitrary"),
        name="in_proj",
    )(*operands)


def _pool_kernel(u_ref, buf_ref, pw_ref, ps_ref, y_ref, new_ref, ext_ref, *, length, pos0):
    ext_ref[pl.ds(1, POOL_BUF), :] = buf_ref[...]
    ext_ref[pl.ds(16, length), :] = u_ref[...]
    lane = lax.broadcasted_iota(jnp.int32, (1, POOL_W), 1)
    grp = jnp.right_shift(lane, 6)
    wl = jnp.where(grp == 0, 2, jnp.where(grp == 1, 4, jnp.where(grp == 2, 8, 16)))
    ch = min(length, 256)
    row = lax.broadcasted_iota(jnp.int32, (ch, 1), 0)
    for c in range(length // ch):
        base = 16 + c * ch
        cur = ext_ref[pl.ds(base, ch), :]
        acc = cur
        sums = {}
        for i in range(1, 16):
            acc = acc + ext_ref[pl.ds(base - i, ch), :]
            if i + 1 in POOL_WINDOWS:
                sums[i + 1] = acc
        wsum = jnp.where(grp == 0, sums[2], jnp.where(grp == 1, sums[4],
                                                      jnp.where(grp == 2, sums[8], sums[16])))
        cnt = jnp.minimum(wl, row + (pos0 + c * ch + 1)).astype(F32)
        d = wsum / cnt - cur
        y = jnp.dot(d.astype(BF16), pw_ref[...], preferred_element_type=F32) * ps_ref[...]
        y_ref[pl.ds(c * ch, ch), :] = y.astype(y_ref.dtype)
    new_ref[...] = ext_ref[pl.ds(length + 1, POOL_BUF), :]


def _pool(u3, buf3, pw_bd, ps, pos0, l):
    nb, length, _ = u3.shape
    return pl.pallas_call(
        functools.partial(_pool_kernel, length=length, pos0=pos0),
        grid=(nb,),
        in_specs=[pl.BlockSpec((None, length, POOL_W), lambda b: (b, 0, 0)),
                  pl.BlockSpec((None, POOL_BUF, POOL_W), lambda b: (b, 0, 0)),
                  _layer_spec(pw_bd, l), _layer_spec(ps, l)],
        out_specs=[pl.BlockSpec((None, length, POOL_W), lambda b: (b, 0, 0)),
                   pl.BlockSpec((None, POOL_BUF, POOL_W), lambda b: (b, 0, 0))],
        out_shape=[jax.ShapeDtypeStruct((nb, length, POOL_W), BF16),
                   jax.ShapeDtypeStruct((nb, POOL_BUF, POOL_W), F32)],
        scratch_shapes=[pltpu.VMEM((16 + length, POOL_W), F32)],
        compiler_params=_cparams("arbitrary"),
        name="pool_mixer",
    )(u3, buf3, pw_bd, ps)


def _attn_kernel(q_ref, k_ref, v_ref, o_ref, km_scr, vt_scr, ot_scr, s_scr, p_scr, *, length):
    nb = length // BLK
    lane = lax.broadcasted_iota(jnp.int32, (1, 128), 1)
    k2 = k_ref[...]
    kmean = jnp.sum(k2.reshape(nb, BLK, 128), axis=1) * (1.0 / BLK)
    vt_scr[...] = v_ref[...].T.astype(BF16)
    krow = lax.broadcasted_iota(jnp.int32, (BLK, BLK), 0)
    qcol = lax.broadcasted_iota(jnp.int32, (BLK, BLK), 1)
    causal = krow <= qcol
    blkrow = lax.broadcasted_iota(jnp.int32, (nb, BLK), 0)
    pad_rows = jnp.zeros((16 - nb, 128), F32) if nb < 16 else None
    kmh = []
    for hh in range(2):
        hm = jnp.right_shift(lane, 6) == hh
        km_scr[hh] = jnp.where(hm, k2, 0.0).astype(BF16)
        t = jnp.where(hm, kmean, 0.0)
        kmh.append(t if pad_rows is None else jnp.concatenate([t, pad_rows], axis=0))

    def fold8(t):
        return t.reshape(BLK // 8, 8, BLK)

    def scores(hh, i, slot):
        qi = q_ref[pl.ds(i * BLK, BLK), :]
        bias = None
        if i > 0:
            gate = _bdot_t(kmh[hh], qi)[:nb]
            past = blkrow < i
            gate = jnp.where(past, gate, -jnp.inf)
            cnt = jnp.zeros((nb, BLK), jnp.int32)
            for jp in range(i):
                gj = gate[jp:jp + 1, :]
                cnt = cnt + jnp.where(gj > gate, 1,
                                      jnp.where(gj == gate, (jp < blkrow).astype(jnp.int32), 0))
            sel = jnp.where(past, cnt, TOPK) < TOPK
            bias = jnp.where(sel, 0.0, NEG)
        m8 = None
        for j in range(i + 1):
            s = _bdot_t(km_scr[hh, j * BLK:(j + 1) * BLK, :], qi)
            s = jnp.where(causal, s, NEG) if j == i else s + bias[j:j + 1, :]
            s_scr[slot, j] = s
            t = jnp.max(fold8(s), axis=0)
            m8 = t if m8 is None else jnp.maximum(m8, t)
        return jnp.max(m8, axis=0, keepdims=True)

    def weighted(hh, i, slot, m):
        l8 = None
        for j in range(i + 1):
            p = jnp.exp2(s_scr[slot, j] - m)
            t = jnp.sum(fold8(p), axis=0)
            l8 = t if l8 is None else l8 + t
            p_scr[slot, j * BLK:(j + 1) * BLK, :] = p.astype(BF16)
        acc = jnp.dot(vt_scr[hh * 64:(hh + 1) * 64, 0:(i + 1) * BLK], p_scr[slot, 0:(i + 1) * BLK, :],
                      preferred_element_type=F32)
        l = jnp.sum(l8, axis=0, keepdims=True)
        ot_scr[hh * 64:(hh + 1) * 64, i * BLK:(i + 1) * BLK] = acc / l

    items = [(hh, i) for hh in range(2) for i in range(nb)]
    pending = None
    for n, (hh, i) in enumerate(items):
        m = scores(hh, i, n % 2)
        if pending is not None:
            weighted(*pending)
        pending = (hh, i, n % 2, m)
    weighted(*pending)
    o_ref[...] = ot_scr[...].T.astype(o_ref.dtype)


def _attn_prompt(q3, k_all, v_all, l):
    nb, length, _ = q3.shape
    nblk = length // BLK
    spec = pl.BlockSpec((None, length, 128), lambda b, h: (b, 0, h))
    kv_spec = pl.BlockSpec((None, None, length, 128), lambda b, h: (l, b, 0, h))
    return pl.pallas_call(
        functools.partial(_attn_kernel, length=length),
        grid=(nb, ATT_W // 128),
        in_specs=[spec, kv_spec, kv_spec],
        out_specs=spec,
        out_shape=jax.ShapeDtypeStruct((nb, length, ATT_W), BF16),
        scratch_shapes=[pltpu.VMEM((2, length, 128), BF16),
                        pltpu.VMEM((128, length), BF16),
                        pltpu.VMEM((128, length), F32),
                        pltpu.VMEM((2, nblk, BLK, BLK), F32),
                        pltpu.VMEM((2, length, BLK), BF16)],
        compiler_params=_cparams("arbitrary", "arbitrary"),
        name="moba_prompt",
    )(q3, k_all, v_all)


def _sa_partials(kp, vp, qbc_ref, s_scr, g_scr, m_scr, l_scr, o_scr, first_blk):
    full = (ATT_HEADS, PAGE)
    for pg in range(len(kp)):
        for h in range(ATT_HEADS):
            s_scr[pg, pl.ds(h, 1), :] = jnp.sum(kp[pg][h] * qbc_ref[h], axis=0, keepdims=True)
    for blk in range(len(kp) // 2):
        s0 = s_scr[2 * blk]
        s1 = s_scr[2 * blk + 1]
        gsum = jnp.sum(s0, axis=-1, keepdims=True) + jnp.sum(s1, axis=-1, keepdims=True)
        mb = jnp.maximum(jnp.max(s0, axis=-1, keepdims=True), jnp.max(s1, axis=-1, keepdims=True))
        p0 = jnp.exp2(s0 - mb)
        p1 = jnp.exp2(s1 - mb)
        lb = jnp.sum(p0, axis=-1, keepdims=True) + jnp.sum(p1, axis=-1, keepdims=True)
        s_scr[2 * blk] = p0
        s_scr[2 * blk + 1] = p1
        idx = first_blk + blk
        g_scr[idx] = jnp.broadcast_to(gsum, full)
        m_scr[idx] = jnp.broadcast_to(mb, full)
        l_scr[idx] = jnp.broadcast_to(lb, full)
        for h in range(ATT_HEADS):
            o_scr[idx, h] = (vp[2 * blk][h] * s_scr[2 * blk, pl.ds(h, 1), :]
                             + vp[2 * blk + 1][h] * s_scr[2 * blk + 1, pl.ds(h, 1), :])


def _sa_merge(q_ref, kn_ref, vn_ref, o_ref, g_scr, m_scr, l_scr, o_scr, w_scr, nblk):
    full = (ATT_HEADS, PAGE)
    gates = g_scr[pl.ds(0, nblk)]
    if nblk > TOPK:
        lane = lax.broadcasted_iota(jnp.int32, full, 1)
        gates_c = jnp.zeros(full, F32)
        for b in range(nblk):
            gates_c = jnp.where(lane == b, gates[b], gates_c)
        cnt = jnp.zeros(full, jnp.int32)
        for b in range(nblk):
            cnt = cnt + jnp.where(gates[b] > gates_c, 1,
                                  jnp.where(gates[b] == gates_c, (b < lane).astype(jnp.int32), 0))
        last = jnp.where(lane < nblk, cnt, -1) == TOPK - 1
        g3 = jnp.max(jnp.where(last, gates_c, -jnp.inf), axis=-1, keepdims=True)
        b3 = jnp.max(jnp.where(last, lane.astype(F32), -1.0), axis=-1, keepdims=True)
        bidx = lax.broadcasted_iota(jnp.int32, gates.shape, 0).astype(F32)
        sel = (gates > g3[None]) | ((gates == g3[None]) & (bidx <= b3[None]))
    else:
        sel = jnp.full(gates.shape, True)
    sself = jnp.broadcast_to(jnp.sum(q_ref[...] * kn_ref[...], axis=-1, keepdims=True), full)
    mb = m_scr[pl.ds(0, nblk)]
    mtot = jnp.maximum(jnp.max(jnp.where(sel, mb, -jnp.inf), axis=0), sself)
    w = jnp.where(sel, jnp.exp2(jnp.where(sel, mb - mtot[None], 0.0)), 0.0)
    wself = jnp.exp2(sself - mtot)
    denom = jnp.sum(w * l_scr[pl.ds(0, nblk)], axis=0) + wself
    w_scr[...] = w
    ones8 = jnp.ones((8, PAGE), BF16)
    sub = lax.broadcasted_iota(jnp.int32, (ATT_HEADS, HEAD_DIM), 0)
    o_acc = jnp.zeros((ATT_HEADS, HEAD_DIM), F32)
    for h in range(ATT_HEADS):
        tot = w_scr[0, pl.ds(h, 1), :] * o_scr[0, h]
        for b in range(1, nblk):
            tot = tot + w_scr[b, pl.ds(h, 1), :] * o_scr[b, h]
        r = _dot3_t_lhs_exact(ones8, tot)
        o_acc = jnp.where(sub == h, r, o_acc)
    o_ref[...] = (o_acc + wself[:, :HEAD_DIM] * vn_ref[...]) / denom[:, :HEAD_DIM]


def _sa_kernel(pt_ref, qbc_ref, q_ref, kn_ref, vn_ref, *refs, gpages, n_steps):
    kp = refs[:gpages]
    vp = refs[gpages:2 * gpages]
    o_ref = refs[2 * gpages]
    g_scr, m_scr, l_scr, o_scr, s_scr, w_scr = refs[2 * gpages + 1:]
    step = pl.program_id(1)
    _sa_partials(kp, vp, qbc_ref, s_scr, g_scr, m_scr, l_scr, o_scr, step * (gpages // 2))

    @pl.when(step == n_steps - 1)
    def _merge():
        _sa_merge(q_ref, kn_ref, vn_ref, o_ref, g_scr, m_scr, l_scr, o_scr, w_scr, n_steps * (gpages // 2))


def _attn_sample(q3, kn3, vn3, cache_kt, cache_vt, page_table, l):
    nb = q3.shape[0]
    n_pages = page_table.shape[1]
    gpages = SA_PAGES_PER_STEP
    n_steps = n_pages // gpages
    nblk = n_pages // 2
    qbc = jnp.broadcast_to(q3[..., None], (nb, ATT_HEADS, HEAD_DIM, PAGE))
    tok = lambda: pl.BlockSpec((None, ATT_HEADS, HEAD_DIM), lambda b, s, pt: (b, 0, 0))

    def page_spec(i):
        return pl.BlockSpec((None, None, ATT_HEADS, HEAD_DIM, PAGE),
                            lambda b, s, pt: (l, pt[b, s * gpages + i], 0, 0, 0))

    grid_spec = pltpu.PrefetchScalarGridSpec(
        num_scalar_prefetch=1,
        grid=(nb, n_steps),
        in_specs=[pl.BlockSpec((None, ATT_HEADS, HEAD_DIM, PAGE), lambda b, s, pt: (b, 0, 0, 0)),
                  tok(), tok(), tok()]
                 + [page_spec(i) for i in range(gpages)] + [page_spec(i) for i in range(gpages)],
        out_specs=tok(),
        scratch_shapes=[pltpu.VMEM((nblk, ATT_HEADS, PAGE), F32),
                        pltpu.VMEM((nblk, ATT_HEADS, PAGE), F32),
                        pltpu.VMEM((nblk, ATT_HEADS, PAGE), F32),
                        pltpu.VMEM((nblk, ATT_HEADS, HEAD_DIM, PAGE), F32),
                        pltpu.VMEM((gpages, ATT_HEADS, PAGE), F32),
                        pltpu.VMEM((nblk, ATT_HEADS, PAGE), F32)],
    )
    return pl.pallas_call(
        functools.partial(_sa_kernel, gpages=gpages, n_steps=n_steps),
        grid_spec=grid_spec,
        out_shape=jax.ShapeDtypeStruct((nb, ATT_HEADS, HEAD_DIM), F32),
        compiler_params=_cparams("arbitrary", "arbitrary"),
        name="moba_sample",
    )(page_table, qbc, q3, kn3, vn3, *([cache_kt] * gpages), *([cache_vt] * gpages))


def _gdn_constants(chunk):
    C = chunk
    lane = np.arange(DN_W)
    lane_j = lane % 64
    row = np.arange(C)[:, None]
    tri = np.stack([lane_j[None] <= row, lane_j[None] < row, lane_j[None] == row, row <= lane_j[None]])
    l_incl = np.arange(C)[None, :] <= np.arange(C)[:, None]
    blk = np.arange(DN_W) // 64
    bd = blk[:, None] == blk[None, :]
    place = np.stack([np.arange(DN_W)[None, :] == (np.arange(DN_DK)[:, None] + 64 * h) for h in range(DN_HEADS)])
    half = np.stack([np.broadcast_to((np.arange(128) // 64) == s, (C, 128)) for s in range(2)])
    return (jnp.asarray(tri, F32), jnp.asarray(l_incl, BF16), jnp.asarray(bd, BF16), jnp.asarray(bd, F32),
            jnp.asarray(place, BF16), jnp.asarray(half, F32))


def _gdn_kernel(d_ref, z_ref, b_ref, a_ref, c0_ref, s0_ref, cw_ref, alog_ref, dtb_ref, dnw_ref,
                tri_ref, lincl_ref, onesbd_ref, bdmask_ref, place_ref, half_ref,
                y_ref, cout_ref, sout_ref, ext_scr, s_scr, *, bt, n_chunks, l_valid, chunk):
    n = pl.program_id(1)
    C = chunk
    incl4 = tri_ref[0] > 0.5
    strict4 = tri_ref[1] > 0.5
    eye4 = tri_ref[2]
    ut4 = tri_ref[3]
    l_incl = lincl_ref[...]
    ones_bd = onesbd_ref[...]
    zero_half = jnp.zeros((C, 128), F32)
    left = half_ref[0] > 0.5
    right = half_ref[1] > 0.5

    def stack_mask(a4):
        lo = a4[:, :128]
        hi = a4[:, 128:]
        pieces = [jnp.concatenate([jnp.where(left, lo, 0.0), zero_half], axis=1),
                  jnp.concatenate([jnp.where(right, lo, 0.0), zero_half], axis=1),
                  jnp.concatenate([zero_half, jnp.where(left, hi, 0.0)], axis=1),
                  jnp.concatenate([zero_half, jnp.where(right, hi, 0.0)], axis=1)]
        if C < DN_DK:
            fill = jnp.zeros((DN_DK - C, DN_W), F32)
            pieces = [x for p in pieces for x in (p, fill)]
        return jnp.concatenate(pieces, axis=0).astype(BF16)

    @pl.when(n == 0)
    def _init():
        for bi in range(bt):
            ext_scr[bi, pl.ds(5, CONV_W - 1), :] = c0_ref[bi]
            s_scr[bi] = jnp.concatenate(
                [_dot3_rhs_exact(s0_ref[bi, h], place_ref[h]) for h in range(DN_HEADS)], axis=0)

    last_valid = l_valid - (n_chunks - 1) * C
    masked = last_valid != C
    if masked:
        row = lax.broadcasted_iota(jnp.int32, (C, 1), 0)
        valid = (n * C + row) < l_valid

    seqs = range(bt)
    each = lambda f, *cols: [f(*xs) for xs in zip(*cols)]
    cw = cw_ref[...]
    act = []
    for bi in seqs:
        ext_scr[bi, pl.ds(8, C), :] = d_ref[bi]
        yc = (cw[0:1] * ext_scr[bi, pl.ds(5, C), :] + cw[1:2] * ext_scr[bi, pl.ds(6, C), :]
              + cw[2:3] * ext_scr[bi, pl.ds(7, C), :] + cw[3:4] * ext_scr[bi, pl.ds(8, C), :])
        act.append(yc * _sigmoid(yc))
        cout_ref[bi] = ext_scr[bi, pl.ds(last_valid + 5, CONV_W - 1), :]
        ext_scr[bi, pl.ds(5, CONV_W - 1), :] = ext_scr[bi, pl.ds(8 + C - (CONV_W - 1), CONV_W - 1), :]

    q_raw = [a[:, 0:DN_W] for a in act]
    k_raw = [a[:, DN_W:2 * DN_W] for a in act]
    v4 = [a[:, 2 * DN_W:3 * DN_W] for a in act]
    ssq = each(lambda x: _bdot(x * x, ones_bd), q_raw)
    ssk = each(lambda x: _bdot(x * x, ones_bd), k_raw)
    q4 = each(lambda x, s: x * lax.rsqrt(s + EPS) * (DN_DK ** -0.5), q_raw, ssq)
    k4 = each(lambda x, s: x * lax.rsqrt(s + EPS), k_raw, ssk)
    beta4 = [_sigmoid(b_ref[bi]) for bi in seqs]

    def log_decay(bi):
        xg = a_ref[bi] + dtb_ref[...]
        return -jnp.exp(alog_ref[...]) * (jnp.maximum(xg, 0.0) + jnp.log1p(jnp.exp(-jnp.abs(xg))))

    g4 = [log_decay(bi) for bi in seqs]
    if masked:
        zero_pad = lambda x: jnp.where(valid, x, 0.0)
        q4, k4, v4, beta4, g4 = (each(zero_pad, c) for c in (q4, k4, v4, beta4, g4))

    gi = each(lambda g: _dot3_lhs_exact(l_incl, g), g4)
    gj = each(lambda g: jnp.sum(g * ut4, axis=0, keepdims=True), g4)
    decay4 = each(lambda a, b: jnp.where(incl4, jnp.exp(jnp.where(incl4, a - b, 0.0)), 0.0), gi, gj)
    eg4 = each(jnp.exp, gi)
    glast = [g[C - 1:C, :] for g in gi]
    kfac = each(lambda a, b: jnp.exp(a - b), glast, gi)
    gtot = each(jnp.exp, glast)

    kb4 = each(lambda a, b: a * b, k4, beta4)
    mk = each(lambda kb, q, k: _bdot_t(jnp.concatenate([kb, q], axis=0), stack_mask(k)), kb4, q4, k4)
    attn4 = each(lambda r, dc: r[C:] * dc, mk, decay4)

    p = each(lambda r, dc: -jnp.where(strict4, r[:C] * dc, 0.0), mk, decay4)
    t = each(lambda x: eye4 + x, p)
    p = each(lambda x: _bdot(x, stack_mask(x)), p)
    for _ in range(C.bit_length() - 3):
        r = each(lambda a, b: _bdot(jnp.concatenate([a, b], axis=0), stack_mask(b)), t, p)
        t = each(lambda a, b: a + b[:C], t, r)
        p = [x[C:] for x in r]
    t = each(lambda a, b: a + _bdot(a, stack_mask(b)), t, p)

    u4 = each(lambda a, v, b: _bdot(a, stack_mask(v * b)), t, v4, beta4)
    kc4 = each(lambda a, kb, e: _bdot(a, stack_mask(kb * e)), t, kb4, eg4)

    sbd = [s_scr[bi] for bi in seqs]
    r = each(lambda kc, q, e, s: _bdot(jnp.concatenate([kc, q * e], axis=0), s), kc4, q4, eg4, sbd)
    vnew = each(lambda u, x: u - x[:C], u4, r)
    o4 = each(lambda x, a, v: x[C:] + _bdot(a, stack_mask(v)), r, attn4, vnew)
    upd = each(lambda k, f, v: lax.dot_general((k * f).astype(BF16), v.astype(BF16), (((0,), (0,)), ((), ())),
                                               preferred_element_type=F32), k4, kfac, vnew)
    sso = each(lambda o: _bdot(o * o, ones_bd), o4)
    for bi in seqs:
        s_scr[bi] = sbd[bi] * gtot[bi] + upd[bi] * bdmask_ref[...]
        zz = z_ref[bi]
        y_ref[bi] = (o4[bi] * lax.rsqrt(sso[bi] * (1.0 / DN_DK) + EPS) * dnw_ref[...]
                     * (zz * _sigmoid(zz))).astype(y_ref.dtype)

    @pl.when(n == n_chunks - 1)
    def _fin():
        for bi in range(bt):
            for h in range(DN_HEADS):
                sout_ref[bi, h] = _dot3_t_rhs_exact(s_scr[bi, h * 64:(h + 1) * 64, :], place_ref[h])


def _gdn(d3, z3, b3, a3, conv0, s0, cw, alog4, dtb4, dnw4, l, l_valid, bt, chunk):
    nb, length, _ = d3.shape
    n_chunks = length // chunk
    tok = lambda w: pl.BlockSpec((bt, chunk, w), lambda i, n: (i, n, 0))
    const = lambda i, n: (0, 0)
    const3 = lambda i, n: (0, 0, 0)
    return pl.pallas_call(
        functools.partial(_gdn_kernel, bt=bt, n_chunks=n_chunks, l_valid=l_valid, chunk=chunk),
        grid=(nb // bt, n_chunks),
        in_specs=[tok(3 * DN_W), tok(DN_W), tok(DN_W), tok(DN_W),
                  pl.BlockSpec((bt, CONV_W - 1, 3 * DN_W), lambda i, n: (i, 0, 0)),
                  pl.BlockSpec((bt, DN_HEADS, DN_DK, DN_DK), lambda i, n: (i, 0, 0, 0)),
                  _layer_spec(cw, l), _layer_spec(alog4, l), _layer_spec(dtb4, l), _layer_spec(dnw4, l),
                  pl.BlockSpec((4, chunk, DN_W), const3),
                  pl.BlockSpec((chunk, chunk), const),
                  pl.BlockSpec((DN_W, DN_W), const), pl.BlockSpec((DN_W, DN_W), const),
                  pl.BlockSpec((DN_HEADS, DN_DK, DN_W), const3),
                  pl.BlockSpec((2, chunk, 128), const3)],
        out_specs=[tok(DN_W),
                   pl.BlockSpec((bt, CONV_W - 1, 3 * DN_W), lambda i, n: (i, 0, 0)),
                   pl.BlockSpec((bt, DN_HEADS, DN_DK, DN_DK), lambda i, n: (i, 0, 0, 0))],
        out_shape=[jax.ShapeDtypeStruct((nb, length, DN_W), BF16),
                   jax.ShapeDtypeStruct((nb, CONV_W - 1, 3 * DN_W), F32),
                   jax.ShapeDtypeStruct((nb, DN_HEADS, DN_DK, DN_DK), F32)],
        scratch_shapes=[pltpu.VMEM((bt, 8 + chunk, 3 * DN_W), F32),
                        pltpu.VMEM((bt, DN_W, DN_W), F32)],
        compiler_params=_cparams("arbitrary", "arbitrary"),
        name="gated_delta",
    )(d3, z3, b3, a3, conv0, s0, cw, alog4, dtb4, dnw4, *_gdn_constants(chunk))


FF_CHUNK = D_FF // 2


def _mix_residual(x_ref, yp_ref, ya_ref, yd_ref, g1_ref, sh2_ref, sc2_ref, n2_ref, wo_ref):
    mix = jnp.concatenate([yp_ref[...].astype(BF16), ya_ref[...].astype(BF16), yd_ref[...].astype(BF16)],
                          axis=1)
    x1 = x_ref[...] + g1_ref[...] * jnp.dot(mix, wo_ref[...], preferred_element_type=F32)
    ms = jnp.mean(x1 * x1, axis=-1, keepdims=True)
    h2 = ((x1 * lax.rsqrt(ms + EPS) * n2_ref[...]) * (1.0 + sc2_ref[...]) + sh2_ref[...]).astype(BF16)
    return x1, h2


def _out_kernel(x_ref, yp_ref, ya_ref, yd_ref, g1_ref, sh2_ref, sc2_ref, g2_ref, n2_ref,
                wo_ref, wg_ref, wu_ref, wd_ref, o_ref):
    x1, h2 = _mix_residual(x_ref, yp_ref, ya_ref, yd_ref, g1_ref, sh2_ref, sc2_ref, n2_ref, wo_ref)
    acc = None
    for c in range(D_FF // FF_CHUNK):
        c0 = c * FF_CHUNK
        gt = jnp.dot(h2, wg_ref[:, c0:c0 + FF_CHUNK], preferred_element_type=F32)
        up = jnp.dot(h2, wu_ref[:, c0:c0 + FF_CHUNK], preferred_element_type=F32)
        act = (gt * _sigmoid(gt) * up).astype(BF16)
        part = jnp.dot(act, wd_ref[c0:c0 + FF_CHUNK, :], preferred_element_type=F32)
        acc = part if acc is None else acc + part
    o_ref[...] = x1 + g2_ref[...] * acc


def _out_ffn(x3, yp, ya, yd, mod, l, per_token, n2, wo, wg, wu, wd, tm):
    nb, length, _ = x3.shape
    tok = lambda w: pl.BlockSpec((None, tm, w), lambda b, t: (b, t, 0))
    single = pl.Buffered(1)
    return pl.pallas_call(
        _out_kernel,
        grid=(nb, length // tm),
        in_specs=[tok(D_MODEL), tok(POOL_W), tok(ATT_W), tok(DN_W),
                  _mod_spec(mod, l, 2, per_token), _mod_spec(mod, l, 3, per_token),
                  _mod_spec(mod, l, 4, per_token), _mod_spec(mod, l, 5, per_token),
                  _layer_spec(n2, l),
                  _layer_spec(wo, l, pipeline_mode=single), _layer_spec(wg, l, pipeline_mode=single),
                  _layer_spec(wu, l, pipeline_mode=single), _layer_spec(wd, l, pipeline_mode=single)],
        out_specs=tok(D_MODEL),
        out_shape=jax.ShapeDtypeStruct((nb, length, D_MODEL), F32),
        compiler_params=_cparams("arbitrary", "arbitrary"),
        name="out_ffn",
    )(x3, yp, ya, yd, mod, mod, mod, mod, n2, wo, wg, wu, wd)


def kernel(x_prompt, x_sample, cache_k, cache_v, state_pool, state_conv, state_delta, page_table,
           c_prompt, c_sample, norm1_w, ada_w, ada_b, w_in, pool_w, pool_scale, q_norm_w, k_norm_w,
           conv_w, a_log, dt_bias, dn_norm_w, w_out, norm2_w, w_gate, w_up, w_down):
    depth = w_in.shape[0]
    nbp, seq, _ = x_prompt.shape
    nbs = x_sample.shape[0]
    past_len = page_table.shape[1] * PAGE

    n_main = C_Z[1]
    w_ext = jnp.concatenate(
        [w_in[:, :, :n_main],
         jnp.repeat(w_in[:, :, n_main:n_main + DN_HEADS], DN_DK, axis=-1),
         jnp.repeat(w_in[:, :, n_main + DN_HEADS:], DN_DK, axis=-1)], axis=-1).astype(BF16)
    wo_b, wg_b, wu_b, wd_b = (w.astype(BF16) for w in (w_out, w_gate, w_up, w_down))
    eye_g = jnp.eye(len(POOL_WINDOWS), dtype=F32)
    pw_bd = (eye_g[None, :, None, :, None] * pool_w[:, :, :, None, :]).reshape(depth, POOL_W, POOL_W).astype(BF16)
    ones_att = jnp.kron(jnp.eye(ATT_HEADS, dtype=F32), jnp.ones((HEAD_DIM, HEAD_DIM), F32)).astype(BF16)
    qn4 = jnp.tile(q_norm_w, (1, ATT_HEADS)).reshape(depth, 1, ATT_W)
    kn4 = jnp.tile(k_norm_w, (1, ATT_HEADS)).reshape(depth, 1, ATT_W)
    dnw4 = jnp.tile(dn_norm_w, (1, DN_HEADS)).reshape(depth, 1, DN_W)
    alog4 = jnp.repeat(a_log, DN_DK, axis=-1).reshape(depth, 1, DN_W)
    dtb4 = jnp.repeat(dt_bias, DN_DK, axis=-1).reshape(depth, 1, DN_W)
    n1 = norm1_w.reshape(depth, 1, D_MODEL)
    n2 = norm2_w.reshape(depth, 1, D_MODEL)
    ps = pool_scale.reshape(depth, 1, POOL_W)
    cache_kt = jnp.transpose(cache_k, (0, 1, 3, 4, 2))
    cache_vt = jnp.transpose(cache_v, (0, 1, 3, 4, 2))

    mod = _modulation(jnp.concatenate([c_prompt, c_sample], axis=0), ada_w, ada_b)
    mod_p = mod[:, :nbp].reshape(depth, nbp, N_MOD, 1, D_MODEL)
    mod_s = mod[:, nbp:].reshape(depth, nbs, N_MOD, D_MODEL).transpose(0, 2, 1, 3)

    xp = x_prompt
    xs = x_sample.reshape(1, nbs, D_MODEL)
    zero_pool = jnp.zeros((nbp, POOL_BUF, POOL_W), F32)
    zero_pool_s = jnp.zeros((nbs, POOL_BUF, POOL_W), F32)
    zero_conv = jnp.zeros((nbp, CONV_W - 1, 3 * DN_W), F32)
    zero_state = jnp.zeros((nbp, DN_HEADS, DN_DK, DN_DK), F32)
    pad_rows = lambda a: jnp.pad(a.reshape(nbs, 1, a.shape[-1]), ((0, 0), (0, SAMPLE_CHUNK - 1), (0, 0)))

    outs = {k: [] for k in ("ks", "vs", "pp", "ps", "cp", "cs", "sp", "ss")}
    kv_all = None
    for l in range(depth):
        u, q, k_all, v_all, d, z, b4, a4 = _in_proj(xp, mod_p, l, False, n1, w_ext, qn4, kn4, ones_att,
                                                    2 * TM, kv_depth=depth, kv_prev=kv_all)
        kv_all = (k_all, v_all)
        y_pool, pool_new = _pool(u, zero_pool, pw_bd, ps, 0, l)
        y_att = _attn_prompt(q, k_all, v_all, l)
        y_dn, conv_new, s_new = _gdn(d, z, b4, a4, zero_conv, zero_state, conv_w, alog4, dtb4, dnw4,
                                     l, seq, GDN_BT, CHUNK)
        xp = _out_ffn(xp, y_pool, y_att, y_dn, mod_p, l, False, n2, wo_b, wg_b, wu_b, wd_b, TM)
        outs["pp"].append(pool_new)
        outs["cp"].append(conv_new)
        outs["sp"].append(s_new)

        us, qs, ks, vs, ds, zs, b4s, a4s = _in_proj(xs, mod_s, l, True, n1, w_ext, qn4, kn4, ones_att, nbs)
        ext = jnp.concatenate([state_pool[l], us.reshape(nbs, 1, POOL_W)], axis=1)
        y_pool16, pool_new = _pool(ext, zero_pool_s, pw_bd, ps, past_len - POOL_BUF, l)
        y_pool = y_pool16[:, POOL_BUF:].reshape(1, nbs, POOL_W)
        o_att = _attn_sample(qs.reshape(nbs, ATT_HEADS, HEAD_DIM).astype(F32),
                             ks.reshape(nbs, ATT_HEADS, HEAD_DIM), vs.reshape(nbs, ATT_HEADS, HEAD_DIM),
                             cache_kt, cache_vt, page_table, l)
        y_att = o_att.reshape(1, nbs, ATT_W)
        y_dnc, conv_new, s_new = _gdn(pad_rows(ds), pad_rows(zs), pad_rows(b4s), pad_rows(a4s),
                                      state_conv[l], state_delta[l], conv_w, alog4, dtb4, dnw4,
                                      l, 1, GDN_BT, SAMPLE_CHUNK)
        y_dn = y_dnc[:, 0].reshape(1, nbs, DN_W)
        xs = _out_ffn(xs, y_pool, y_att, y_dn, mod_s, l, True, n2, wo_b, wg_b, wu_b, wd_b, nbs)
        outs["ks"].append(ks.reshape(nbs, 1, ATT_HEADS, HEAD_DIM))
        outs["vs"].append(vs.reshape(nbs, 1, ATT_HEADS, HEAD_DIM))
        outs["ps"].append(pool_new)
        outs["cs"].append(conv_new)
        outs["ss"].append(s_new)

    st = lambda name: jnp.stack(outs[name])
    k_prompt, v_prompt = (a.reshape(depth, nbp, seq, ATT_HEADS, HEAD_DIM) for a in kv_all)
    return (xp, xs.reshape(nbs, 1, D_MODEL), k_prompt, v_prompt, st("ks"), st("vs"),
            st("pp"), st("ps"), st("cp"), st("cs"), st("sp"), st("ss"))
```

```python
import functools

import numpy as np
import jax
import jax.numpy as jnp
from jax import lax
from jax.experimental import pallas as pl
from jax.experimental.pallas import tpu as pltpu

F32 = jnp.float32
BF16 = jnp.bfloat16

D_MODEL = 1024
PAGE = 128
POOL_WINDOWS = (2, 4, 8, 16)
POOL_GDIM = 64
POOL_W = 256
POOL_BUF = 15
HEAD_DIM = 64
ATT_HEADS = 8
ATT_W = 512
BLK = 256
TOPK = 3
DN_HEADS = 4
DN_DK = 64
DN_W = 256
CHUNK = 64
SAMPLE_CHUNK = 16
CONV_W = 4
D_FF = 2816
N_MOD = 6
EPS = 1e-6
NEG = -1e30
LOG2E = 1.4426950408889634

C_U = (0, 256)
C_Q = (256, 768)
C_K = (768, 1280)
C_V = (1280, 1792)
C_D = (1792, 2560)
C_Z = (2560, 2816)
C_B = (2816, 3072)
C_A = (3072, 3328)
IN_EXT = 3328

TM = 512
GDN_BT = 8
SA_PAGES_PER_STEP = 32
VMEM_LIMIT = 56 * 1024 * 1024


def _cparams(*sem):
    return pltpu.CompilerParams(dimension_semantics=sem, vmem_limit_bytes=VMEM_LIMIT)


def _sigmoid(x):
    return 1.0 / (1.0 + jnp.exp(-x))


def _bdot(a, b):
    return jnp.dot(a.astype(BF16), b.astype(BF16), preferred_element_type=F32)


def _bdot_t(a, b):
    return lax.dot_general(a.astype(BF16), b.astype(BF16), (((1,), (1,)), ((), ())),
                           preferred_element_type=F32)


def _split3(a):
    hi = a.astype(BF16)
    r1 = a - hi.astype(F32)
    mid = r1.astype(BF16)
    lo = (r1 - mid.astype(F32)).astype(BF16)
    return hi, mid, lo


def _dot3_rhs_exact(a, b01):
    return sum(jnp.dot(p, b01, preferred_element_type=F32) for p in _split3(a))


def _dot3_lhs_exact(a01, b):
    return sum(jnp.dot(a01, p, preferred_element_type=F32) for p in _split3(b))


def _dot3_t_rhs_exact(a, b01):
    return sum(lax.dot_general(p, b01, (((1,), (1,)), ((), ())), preferred_element_type=F32)
               for p in _split3(a))


def _dot3_t_lhs_exact(a01, b):
    return sum(lax.dot_general(a01, p, (((1,), (1,)), ((), ())), preferred_element_type=F32)
               for p in _split3(b))


def _mod_kernel(c_ref, w_ref, b_ref, o_ref):
    c = c_ref[...]
    a = (c * _sigmoid(c)).astype(BF16)
    o_ref[...] = jnp.dot(a, w_ref[...].astype(BF16), preferred_element_type=F32) + b_ref[...]


def _modulation(c_all, ada_w, ada_b):
    depth, _, ncol = ada_w.shape
    nseq = c_all.shape[0]
    tn = 1536
    return pl.pallas_call(
        _mod_kernel,
        grid=(depth, ncol // tn),
        in_specs=[pl.BlockSpec((nseq, D_MODEL), lambda l, j: (0, 0)),
                  pl.BlockSpec((None, D_MODEL, tn), lambda l, j: (l, 0, j)),
                  pl.BlockSpec((None, 1, tn), lambda l, j: (l, 0, j))],
        out_specs=pl.BlockSpec((None, nseq, tn), lambda l, j: (l, 0, j)),
        out_shape=jax.ShapeDtypeStruct((depth, nseq, ncol), F32),
        compiler_params=_cparams("arbitrary", "arbitrary"),
        name="modulation",
    )(c_all, ada_w, ada_b.reshape(depth, 1, ncol))


def _layer_spec(stacked, l, **kw):
    zeros = (0,) * (stacked.ndim - 1)
    return pl.BlockSpec((None,) + stacked.shape[1:], lambda *g: (l,) + zeros, **kw)


def _mod_spec(mod, l, k, per_token):
    if per_token:
        return pl.BlockSpec((None, None, mod.shape[2], D_MODEL), lambda *g: (l, k, 0, 0))
    return pl.BlockSpec((None, None, None, 1, D_MODEL), lambda *g: (l, g[0], k, 0, 0))


def _in_kernel(x_ref, sh_ref, sc_ref, n1_ref, w_ref, qn_ref, kn_ref, ones_ref, *rest):
    u_ref, q_ref, k_ref, v_ref, d_ref, z_ref, b_ref, a_ref = rest[-8:]
    x = x_ref[...]
    ms = jnp.mean(x * x, axis=-1, keepdims=True)
    h = (x * lax.rsqrt(ms + EPS) * n1_ref[...]) * (1.0 + sc_ref[...]) + sh_ref[...]
    hb = h.astype(BF16)

    def proj(c):
        return jnp.dot(hb, w_ref[:, c[0]:c[1]], preferred_element_type=F32)

    def head_norm(t, w4):
        ss = jnp.dot((t * t).astype(BF16), ones_ref[...], preferred_element_type=F32)
        return t * lax.rsqrt(ss * (1.0 / HEAD_DIM) + EPS) * w4

    u_ref[...] = proj(C_U)
    q_ref[...] = (head_norm(proj(C_Q), qn_ref[...]) * (HEAD_DIM ** -0.5 * LOG2E)).astype(BF16)
    k_ref[...] = head_norm(proj(C_K), kn_ref[...])
    v_ref[...] = proj(C_V)
    d_ref[...] = proj(C_D)
    z_ref[...] = proj(C_Z)
    b_ref[...] = proj(C_B)
    a_ref[...] = proj(C_A)


def _in_proj(x3, mod, l, per_token, n1, w_ext, qn4, kn4, ones_att, tm, kv_depth=None, kv_prev=None):
    nb, length, _ = x3.shape
    widths = (POOL_W, ATT_W, ATT_W, ATT_W, 3 * DN_W, DN_W, DN_W, DN_W)
    dtypes = (F32, BF16, F32, F32, F32, F32, F32, F32)
    const = lambda b, t: (0, 0)
    tok = lambda w: pl.BlockSpec((None, tm, w), lambda b, t: (b, t, 0))
    out_specs = [tok(w) for w in widths]
    out_shape = [jax.ShapeDtypeStruct((nb, length, w), dt) for w, dt in zip(widths, dtypes)]
    operands = [x3, mod, mod, n1, w_ext, qn4, kn4, ones_att]
    in_specs = [tok(D_MODEL), _mod_spec(mod, l, 0, per_token), _mod_spec(mod, l, 1, per_token),
                _layer_spec(n1, l), _layer_spec(w_ext, l), _layer_spec(qn4, l), _layer_spec(kn4, l),
                pl.BlockSpec((ATT_W, ATT_W), const)]
    aliases = {}
    if kv_depth is not None:
        for o in (2, 3):
            out_specs[o] = pl.BlockSpec((None, None, tm, ATT_W), lambda b, t: (l, b, t, 0))
            out_shape[o] = jax.ShapeDtypeStruct((kv_depth, nb, length, ATT_W), F32)
        if kv_prev is not None:
            aliases = {len(operands): 2, len(operands) + 1: 3}
            operands += list(kv_prev)
            in_specs += [pl.BlockSpec(memory_space=pl.ANY)] * 2
    return pl.pallas_call(
        _in_kernel,
        grid=(nb, length // tm),
        in_specs=in_specs,
        out_specs=out_specs,
        out_shape=out_shape,
        input_output_aliases=aliases,
        compiler_params=_cparams("arbitrary", "arbitrary"),
        name="in_proj",
    )(*operands)


POOL_ROW0 = 32


def _pool_kernel(u_ref, buf_ref, pw_ref, ps_ref, y_ref, new_ref, e_scr, s2_scr, s4_scr, s8_scr, *,
                 length, pos0, bs):
    zeros16 = jnp.zeros((16, POOL_W), F32)
    lane = lax.broadcasted_iota(jnp.int32, (1, POOL_W), 1)
    grp = jnp.right_shift(lane, 6)
    wl = jnp.where(grp == 0, 2, jnp.where(grp == 1, 4, jnp.where(grp == 2, 8, 16)))
    ch = min(length, 256)
    row = lax.broadcasted_iota(jnp.int32, (ch, 1), 0)
    spans = [(16, 16)] + [(POOL_ROW0 + c * ch, ch) for c in range(length // ch)]
    for s in range(bs):
        e_scr[s, pl.ds(8, 16), :] = zeros16
        s2_scr[s, pl.ds(8, 8), :] = zeros16[:8]
        s4_scr[s, pl.ds(8, 8), :] = zeros16[:8]
        e_scr[s, pl.ds(POOL_ROW0 - POOL_BUF, POOL_BUF), :] = buf_ref[s]
        e_scr[s, pl.ds(POOL_ROW0, length), :] = u_ref[s]
        for src, dst, shift in ((e_scr, s2_scr, 1), (s2_scr, s4_scr, 2), (s4_scr, s8_scr, 4)):
            for base, n in spans:
                dst[s, pl.ds(base, n), :] = src[s, pl.ds(base, n), :] + src[s, pl.ds(base - shift, n), :]
        for c in range(length // ch):
            base = POOL_ROW0 + c * ch
            cur = e_scr[s, pl.ds(base, ch), :]
            s16 = s8_scr[s, pl.ds(base, ch), :] + s8_scr[s, pl.ds(base - 8, ch), :]
            wsum = jnp.where(grp == 0, s2_scr[s, pl.ds(base, ch), :],
                             jnp.where(grp == 1, s4_scr[s, pl.ds(base, ch), :],
                                       jnp.where(grp == 2, s8_scr[s, pl.ds(base, ch), :], s16)))
            cnt = jnp.minimum(wl, row + (pos0 + c * ch + 1)).astype(F32)
            d = wsum / cnt - cur
            y = jnp.dot(d.astype(BF16), pw_ref[...], preferred_element_type=F32) * ps_ref[...]
            y_ref[s, pl.ds(c * ch, ch), :] = y.astype(y_ref.dtype)
        new_ref[s] = e_scr[s, pl.ds(POOL_ROW0 + length - POOL_BUF, POOL_BUF), :]


def _pool(u3, buf3, pw_bd, ps, pos0, l, bs=1):
    nb, length, _ = u3.shape
    return pl.pallas_call(
        functools.partial(_pool_kernel, length=length, pos0=pos0, bs=bs),
        grid=(nb // bs,),
        in_specs=[pl.BlockSpec((bs, length, POOL_W), lambda b: (b, 0, 0)),
                  pl.BlockSpec((bs, POOL_BUF, POOL_W), lambda b: (b, 0, 0)),
                  _layer_spec(pw_bd, l), _layer_spec(ps, l)],
        out_specs=[pl.BlockSpec((bs, length, POOL_W), lambda b: (b, 0, 0)),
                   pl.BlockSpec((bs, POOL_BUF, POOL_W), lambda b: (b, 0, 0))],
        out_shape=[jax.ShapeDtypeStruct((nb, length, POOL_W), BF16),
                   jax.ShapeDtypeStruct((nb, POOL_BUF, POOL_W), F32)],
        scratch_shapes=[pltpu.VMEM((bs, POOL_ROW0 + length, POOL_W), F32)] * 4,
        compiler_params=_cparams("arbitrary"),
        name="pool_mixer",
    )(u3, buf3, pw_bd, ps)


def _attn_kernel(q_ref, k_ref, v_ref, o_ref, km_scr, vt_scr, ot_scr, s_scr, p_scr, *, length):
    nb = length // BLK
    lane = lax.broadcasted_iota(jnp.int32, (1, 128), 1)
    k2 = k_ref[...]
    kmean = jnp.sum(k2.reshape(nb, BLK, 128), axis=1) * (1.0 / BLK)
    vt_scr[...] = v_ref[...].T.astype(BF16)
    krow = lax.broadcasted_iota(jnp.int32, (BLK, BLK), 0)
    qcol = lax.broadcasted_iota(jnp.int32, (BLK, BLK), 1)
    causal = krow <= qcol
    blkrow = lax.broadcasted_iota(jnp.int32, (nb, BLK), 0)
    pad_rows = jnp.zeros((16 - nb, 128), F32) if nb < 16 else None
    kmh = []
    for hh in range(2):
        hm = jnp.right_shift(lane, 6) == hh
        km_scr[hh] = jnp.where(hm, k2, 0.0).astype(BF16)
        t = jnp.where(hm, kmean, 0.0)
        kmh.append(t if pad_rows is None else jnp.concatenate([t, pad_rows], axis=0))

    def fold8(t):
        return t.reshape(BLK // 8, 8, BLK)

    def scores(hh, i, slot):
        qi = q_ref[pl.ds(i * BLK, BLK), :]
        bias = None
        if i > 0:
            gate = _bdot_t(kmh[hh], qi)[:nb]
            past = blkrow < i
            gate = jnp.where(past, gate, -jnp.inf)
            cnt = jnp.zeros((nb, BLK), jnp.int32)
            for jp in range(i):
                gj = gate[jp:jp + 1, :]
                cnt = cnt + jnp.where(gj > gate, 1,
                                      jnp.where(gj == gate, (jp < blkrow).astype(jnp.int32), 0))
            sel = jnp.where(past, cnt, TOPK) < TOPK
            bias = jnp.where(sel, 0.0, NEG)
        m8 = None
        for j in range(i + 1):
            s = _bdot_t(km_scr[hh, j * BLK:(j + 1) * BLK, :], qi)
            s = jnp.where(causal, s, NEG) if j == i else s + bias[j:j + 1, :]
            s_scr[slot, j] = s
            t = jnp.max(fold8(s), axis=0)
            m8 = t if m8 is None else jnp.maximum(m8, t)
        return jnp.max(m8, axis=0, keepdims=True)

    def weighted(hh, i, slot, m):
        l8 = None
        for j in range(i + 1):
            p = jnp.exp2(s_scr[slot, j] - m)
            t = jnp.sum(fold8(p), axis=0)
            l8 = t if l8 is None else l8 + t
            p_scr[slot, j * BLK:(j + 1) * BLK, :] = p.astype(BF16)
        acc = jnp.dot(vt_scr[hh * 64:(hh + 1) * 64, 0:(i + 1) * BLK], p_scr[slot, 0:(i + 1) * BLK, :],
                      preferred_element_type=F32)
        l = jnp.sum(l8, axis=0, keepdims=True)
        ot_scr[hh * 64:(hh + 1) * 64, i * BLK:(i + 1) * BLK] = acc / l

    items = [(hh, i) for hh in range(2) for i in range(nb)]
    pending = None
    for n, (hh, i) in enumerate(items):
        m = scores(hh, i, n % 2)
        if pending is not None:
            weighted(*pending)
        pending = (hh, i, n % 2, m)
    weighted(*pending)
    o_ref[...] = ot_scr[...].T.astype(o_ref.dtype)


def _attn_prompt(q3, k_all, v_all, l):
    nb, length, _ = q3.shape
    nblk = length // BLK
    spec = pl.BlockSpec((None, length, 128), lambda b, h: (b, 0, h))
    kv_spec = pl.BlockSpec((None, None, length, 128), lambda b, h: (l, b, 0, h))
    return pl.pallas_call(
        functools.partial(_attn_kernel, length=length),
        grid=(nb, ATT_W // 128),
        in_specs=[spec, kv_spec, kv_spec],
        out_specs=spec,
        out_shape=jax.ShapeDtypeStruct((nb, length, ATT_W), BF16),
        scratch_shapes=[pltpu.VMEM((2, length, 128), BF16),
                        pltpu.VMEM((128, length), BF16),
                        pltpu.VMEM((128, length), F32),
                        pltpu.VMEM((2, nblk, BLK, BLK), F32),
                        pltpu.VMEM((2, length, BLK), BF16)],
        compiler_params=_cparams("arbitrary", "arbitrary"),
        name="moba_prompt",
    )(q3, k_all, v_all)


def _sa_partials(kp, vp, qbc_ref, s_scr, g_scr, m_scr, l_scr, o_scr, first_blk):
    full = (ATT_HEADS, PAGE)
    for pg in range(len(kp)):
        for h in range(ATT_HEADS):
            s_scr[pg, pl.ds(h, 1), :] = jnp.sum(kp[pg][h] * qbc_ref[h], axis=0, keepdims=True)
    for blk in range(len(kp) // 2):
        s0 = s_scr[2 * blk]
        s1 = s_scr[2 * blk + 1]
        gsum = jnp.sum(s0, axis=-1, keepdims=True) + jnp.sum(s1, axis=-1, keepdims=True)
        mb = jnp.maximum(jnp.max(s0, axis=-1, keepdims=True), jnp.max(s1, axis=-1, keepdims=True))
        p0 = jnp.exp2(s0 - mb)
        p1 = jnp.exp2(s1 - mb)
        lb = jnp.sum(p0, axis=-1, keepdims=True) + jnp.sum(p1, axis=-1, keepdims=True)
        s_scr[2 * blk] = p0
        s_scr[2 * blk + 1] = p1
        idx = first_blk + blk
        g_scr[idx] = jnp.broadcast_to(gsum, full)
        m_scr[idx] = jnp.broadcast_to(mb, full)
        l_scr[idx] = jnp.broadcast_to(lb, full)
        for h in range(ATT_HEADS):
            o_scr[idx, h] = (vp[2 * blk][h] * s_scr[2 * blk, pl.ds(h, 1), :]
                             + vp[2 * blk + 1][h] * s_scr[2 * blk + 1, pl.ds(h, 1), :])


def _sa_merge(q_ref, kn_ref, vn_ref, o_ref, g_scr, m_scr, l_scr, o_scr, w_scr, nblk):
    full = (ATT_HEADS, PAGE)
    gates = g_scr[pl.ds(0, nblk)]
    if nblk > TOPK:
        lane = lax.broadcasted_iota(jnp.int32, full, 1)
        gates_c = jnp.zeros(full, F32)
        for b in range(nblk):
            gates_c = jnp.where(lane == b, gates[b], gates_c)
        cnt = jnp.zeros(full, jnp.int32)
        for b in range(nblk):
            cnt = cnt + jnp.where(gates[b] > gates_c, 1,
                                  jnp.where(gates[b] == gates_c, (b < lane).astype(jnp.int32), 0))
        last = jnp.where(lane < nblk, cnt, -1) == TOPK - 1
        g3 = jnp.max(jnp.where(last, gates_c, -jnp.inf), axis=-1, keepdims=True)
        b3 = jnp.max(jnp.where(last, lane.astype(F32), -1.0), axis=-1, keepdims=True)
        bidx = lax.broadcasted_iota(jnp.int32, gates.shape, 0).astype(F32)
        sel = (gates > g3[None]) | ((gates == g3[None]) & (bidx <= b3[None]))
    else:
        sel = jnp.full(gates.shape, True)
    sself = jnp.broadcast_to(jnp.sum(q_ref[...] * kn_ref[...], axis=-1, keepdims=True), full)
    mb = m_scr[pl.ds(0, nblk)]
    mtot = jnp.maximum(jnp.max(jnp.where(sel, mb, -jnp.inf), axis=0), sself)
    w = jnp.where(sel, jnp.exp2(jnp.where(sel, mb - mtot[None], 0.0)), 0.0)
    wself = jnp.exp2(sself - mtot)
    denom = jnp.sum(w * l_scr[pl.ds(0, nblk)], axis=0) + wself
    w_scr[...] = w
    ones8 = jnp.ones((8, PAGE), BF16)
    sub = lax.broadcasted_iota(jnp.int32, (ATT_HEADS, HEAD_DIM), 0)
    o_acc = jnp.zeros((ATT_HEADS, HEAD_DIM), F32)
    for h in range(ATT_HEADS):
        tot = w_scr[0, pl.ds(h, 1), :] * o_scr[0, h]
        for b in range(1, nblk):
            tot = tot + w_scr[b, pl.ds(h, 1), :] * o_scr[b, h]
        r = _dot3_t_lhs_exact(ones8, tot)
        o_acc = jnp.where(sub == h, r, o_acc)
    o_ref[...] = (o_acc + wself[:, :HEAD_DIM] * vn_ref[...]) / denom[:, :HEAD_DIM]


def _sa_kernel(pt_ref, qbc_ref, q_ref, kn_ref, vn_ref, *refs, gpages, n_steps):
    kp = refs[:gpages]
    vp = refs[gpages:2 * gpages]
    o_ref = refs[2 * gpages]
    g_scr, m_scr, l_scr, o_scr, s_scr, w_scr = refs[2 * gpages + 1:]
    step = pl.program_id(1)
    _sa_partials(kp, vp, qbc_ref, s_scr, g_scr, m_scr, l_scr, o_scr, step * (gpages // 2))

    @pl.when(step == n_steps - 1)
    def _merge():
        _sa_merge(q_ref, kn_ref, vn_ref, o_ref, g_scr, m_scr, l_scr, o_scr, w_scr, n_steps * (gpages // 2))


def _attn_sample(q3, kn3, vn3, cache_kt, cache_vt, page_table, l):
    nb = q3.shape[0]
    n_pages = page_table.shape[1]
    gpages = SA_PAGES_PER_STEP
    n_steps = n_pages // gpages
    nblk = n_pages // 2
    qbc = jnp.broadcast_to(q3[..., None], (nb, ATT_HEADS, HEAD_DIM, PAGE))
    tok = lambda: pl.BlockSpec((None, ATT_HEADS, HEAD_DIM), lambda b, s, pt: (b, 0, 0))

    def page_spec(i):
        return pl.BlockSpec((None, None, ATT_HEADS, HEAD_DIM, PAGE),
                            lambda b, s, pt: (l, pt[b, s * gpages + i], 0, 0, 0))

    grid_spec = pltpu.PrefetchScalarGridSpec(
        num_scalar_prefetch=1,
        grid=(nb, n_steps),
        in_specs=[pl.BlockSpec((None, ATT_HEADS, HEAD_DIM, PAGE), lambda b, s, pt: (b, 0, 0, 0)),
                  tok(), tok(), tok()]
                 + [page_spec(i) for i in range(gpages)] + [page_spec(i) for i in range(gpages)],
        out_specs=tok(),
        scratch_shapes=[pltpu.VMEM((nblk, ATT_HEADS, PAGE), F32),
                        pltpu.VMEM((nblk, ATT_HEADS, PAGE), F32),
                        pltpu.VMEM((nblk, ATT_HEADS, PAGE), F32),
                        pltpu.VMEM((nblk, ATT_HEADS, HEAD_DIM, PAGE), F32),
                        pltpu.VMEM((gpages, ATT_HEADS, PAGE), F32),
                        pltpu.VMEM((nblk, ATT_HEADS, PAGE), F32)],
    )
    return pl.pallas_call(
        functools.partial(_sa_kernel, gpages=gpages, n_steps=n_steps),
        grid_spec=grid_spec,
        out_shape=jax.ShapeDtypeStruct((nb, ATT_HEADS, HEAD_DIM), F32),
        compiler_params=_cparams("arbitrary", "arbitrary"),
        name="moba_sample",
    )(page_table, qbc, q3, kn3, vn3, *([cache_kt] * gpages), *([cache_vt] * gpages))


def _gdn_constants(chunk):
    C = chunk
    lane = np.arange(DN_W)
    lane_j = lane % 64
    row = np.arange(C)[:, None]
    tri = np.stack([lane_j[None] <= row, lane_j[None] < row, lane_j[None] == row, row <= lane_j[None]])
    l_incl = np.arange(C)[None, :] <= np.arange(C)[:, None]
    blk = np.arange(DN_W) // 64
    bd = blk[:, None] == blk[None, :]
    place = np.stack([np.arange(DN_W)[None, :] == (np.arange(DN_DK)[:, None] + 64 * h) for h in range(DN_HEADS)])
    half = np.stack([np.broadcast_to((np.arange(128) // 64) == s, (C, 128)) for s in range(2)])
    return (jnp.asarray(tri, F32), jnp.asarray(l_incl, BF16), jnp.asarray(bd, BF16), jnp.asarray(bd, F32),
            jnp.asarray(place, BF16), jnp.asarray(half, F32))


def _gdn_kernel(d_ref, z_ref, b_ref, a_ref, c0_ref, s0_ref, cw_ref, alog_ref, dtb_ref, dnw_ref,
                tri_ref, lincl_ref, onesbd_ref, bdmask_ref, place_ref, half_ref,
                y_ref, cout_ref, sout_ref, ext_scr, s_scr, *, bt, n_chunks, l_valid, chunk):
    n = pl.program_id(1)
    C = chunk
    incl4 = tri_ref[0] > 0.5
    strict4 = tri_ref[1] > 0.5
    eye4 = tri_ref[2]
    ut4 = tri_ref[3]
    l_incl = lincl_ref[...]
    ones_bd = onesbd_ref[...]
    zero_half = jnp.zeros((C, 128), F32)
    left = half_ref[0] > 0.5
    right = half_ref[1] > 0.5

    def stack_mask(a4):
        lo = a4[:, :128]
        hi = a4[:, 128:]
        pieces = [jnp.concatenate([jnp.where(left, lo, 0.0), zero_half], axis=1),
                  jnp.concatenate([jnp.where(right, lo, 0.0), zero_half], axis=1),
                  jnp.concatenate([zero_half, jnp.where(left, hi, 0.0)], axis=1),
                  jnp.concatenate([zero_half, jnp.where(right, hi, 0.0)], axis=1)]
        if C < DN_DK:
            fill = jnp.zeros((DN_DK - C, DN_W), F32)
            pieces = [x for p in pieces for x in (p, fill)]
        return jnp.concatenate(pieces, axis=0).astype(BF16)

    @pl.when(n == 0)
    def _init():
        for bi in range(bt):
            ext_scr[bi, pl.ds(5, CONV_W - 1), :] = c0_ref[bi]
            s_scr[bi] = jnp.concatenate(
                [_dot3_rhs_exact(s0_ref[bi, h], place_ref[h]) for h in range(DN_HEADS)], axis=0)

    last_valid = l_valid - (n_chunks - 1) * C
    masked = last_valid != C
    if masked:
        row = lax.broadcasted_iota(jnp.int32, (C, 1), 0)
        valid = (n * C + row) < l_valid

    seqs = range(bt)
    each = lambda f, *cols: [f(*xs) for xs in zip(*cols)]
    cw = cw_ref[...]
    act = []
    for bi in seqs:
        ext_scr[bi, pl.ds(8, C), :] = d_ref[bi]
        yc = (cw[0:1] * ext_scr[bi, pl.ds(5, C), :] + cw[1:2] * ext_scr[bi, pl.ds(6, C), :]
              + cw[2:3] * ext_scr[bi, pl.ds(7, C), :] + cw[3:4] * ext_scr[bi, pl.ds(8, C), :])
        act.append(yc * _sigmoid(yc))
        cout_ref[bi] = ext_scr[bi, pl.ds(last_valid + 5, CONV_W - 1), :]
        ext_scr[bi, pl.ds(5, CONV_W - 1), :] = ext_scr[bi, pl.ds(8 + C - (CONV_W - 1), CONV_W - 1), :]

    q_raw = [a[:, 0:DN_W] for a in act]
    k_raw = [a[:, DN_W:2 * DN_W] for a in act]
    v4 = [a[:, 2 * DN_W:3 * DN_W] for a in act]
    ssq = each(lambda x: _bdot(x * x, ones_bd), q_raw)
    ssk = each(lambda x: _bdot(x * x, ones_bd), k_raw)
    q4 = each(lambda x, s: x * lax.rsqrt(s + EPS) * (DN_DK ** -0.5), q_raw, ssq)
    k4 = each(lambda x, s: x * lax.rsqrt(s + EPS), k_raw, ssk)
    beta4 = [_sigmoid(b_ref[bi]) for bi in seqs]

    def log_decay(bi):
        xg = a_ref[bi] + dtb_ref[...]
        return -jnp.exp(alog_ref[...]) * (jnp.maximum(xg, 0.0) + jnp.log1p(jnp.exp(-jnp.abs(xg))))

    g4 = [log_decay(bi) for bi in seqs]
    if masked:
        zero_pad = lambda x: jnp.where(valid, x, 0.0)
        q4, k4, v4, beta4, g4 = (each(zero_pad, c) for c in (q4, k4, v4, beta4, g4))

    gi = each(lambda g: _dot3_lhs_exact(l_incl, g), g4)
    gj = each(lambda g: jnp.sum(g * ut4, axis=0, keepdims=True), g4)
    decay4 = each(lambda a, b: jnp.where(incl4, jnp.exp(jnp.where(incl4, a - b, 0.0)), 0.0), gi, gj)
    eg4 = each(jnp.exp, gi)
    glast = [g[C - 1:C, :] for g in gi]
    kfac = each(lambda a, b: jnp.exp(a - b), glast, gi)
    gtot = each(jnp.exp, glast)

    kb4 = each(lambda a, b: a * b, k4, beta4)
    mk = each(lambda kb, q, k: _bdot_t(jnp.concatenate([kb, q], axis=0), stack_mask(k)), kb4, q4, k4)
    attn4 = each(lambda r, dc: r[C:] * dc, mk, decay4)

    p = each(lambda r, dc: -jnp.where(strict4, r[:C] * dc, 0.0), mk, decay4)
    t = each(lambda x: eye4 + x, p)
    p = each(lambda x: _bdot(x, stack_mask(x)), p)
    for _ in range(C.bit_length() - 3):
        r = each(lambda a, b: _bdot(jnp.concatenate([a, b], axis=0), stack_mask(b)), t, p)
        t = each(lambda a, b: a + b[:C], t, r)
        p = [x[C:] for x in r]
    t = each(lambda a, b: a + _bdot(a, stack_mask(b)), t, p)

    u4 = each(lambda a, v, b: _bdot(a, stack_mask(v * b)), t, v4, beta4)
    kc4 = each(lambda a, kb, e: _bdot(a, stack_mask(kb * e)), t, kb4, eg4)

    sbd = [s_scr[bi] for bi in seqs]
    r = each(lambda kc, q, e, s: _bdot(jnp.concatenate([kc, q * e], axis=0), s), kc4, q4, eg4, sbd)
    vnew = each(lambda u, x: u - x[:C], u4, r)
    o4 = each(lambda x, a, v: x[C:] + _bdot(a, stack_mask(v)), r, attn4, vnew)
    upd = each(lambda k, f, v: lax.dot_general((k * f).astype(BF16), v.astype(BF16), (((0,), (0,)), ((), ())),
                                               preferred_element_type=F32), k4, kfac, vnew)
    sso = each(lambda o: _bdot(o * o, ones_bd), o4)
    for bi in seqs:
        s_scr[bi] = sbd[bi] * gtot[bi] + upd[bi] * bdmask_ref[...]
        zz = z_ref[bi]
        y_ref[bi] = (o4[bi] * lax.rsqrt(sso[bi] * (1.0 / DN_DK) + EPS) * dnw_ref[...]
                     * (zz * _sigmoid(zz))).astype(y_ref.dtype)

    @pl.when(n == n_chunks - 1)
    def _fin():
        for bi in range(bt):
            for h in range(DN_HEADS):
                sout_ref[bi, h] = _dot3_t_rhs_exact(s_scr[bi, h * 64:(h + 1) * 64, :], place_ref[h])


def _gdn(d3, z3, b3, a3, conv0, s0, cw, alog4, dtb4, dnw4, l, l_valid, bt, chunk, stacked_state=False):
    nb, length, _ = d3.shape
    n_chunks = length // chunk
    tok = lambda w: pl.BlockSpec((bt, chunk, w), lambda i, n: (i, n, 0))
    const = lambda i, n: (0, 0)
    const3 = lambda i, n: (0, 0, 0)
    conv_blk = (bt, CONV_W - 1, 3 * DN_W)
    state_blk = (bt, DN_HEADS, DN_DK, DN_DK)
    if stacked_state:
        conv_spec = pl.BlockSpec((None,) + conv_blk, lambda i, n: (l, i, 0, 0))
        state_spec = pl.BlockSpec((None,) + state_blk, lambda i, n: (l, i, 0, 0, 0))
    else:
        conv_spec = pl.BlockSpec(conv_blk, lambda i, n: (i, 0, 0))
        state_spec = pl.BlockSpec(state_blk, lambda i, n: (i, 0, 0, 0))
    return pl.pallas_call(
        functools.partial(_gdn_kernel, bt=bt, n_chunks=n_chunks, l_valid=l_valid, chunk=chunk),
        grid=(nb // bt, n_chunks),
        in_specs=[tok(3 * DN_W), tok(DN_W), tok(DN_W), tok(DN_W), conv_spec, state_spec,
                  _layer_spec(cw, l), _layer_spec(alog4, l), _layer_spec(dtb4, l), _layer_spec(dnw4, l),
                  pl.BlockSpec((4, chunk, DN_W), const3),
                  pl.BlockSpec((chunk, chunk), const),
                  pl.BlockSpec((DN_W, DN_W), const), pl.BlockSpec((DN_W, DN_W), const),
                  pl.BlockSpec((DN_HEADS, DN_DK, DN_W), const3),
                  pl.BlockSpec((2, chunk, 128), const3)],
        out_specs=[tok(DN_W),
                   pl.BlockSpec((bt, CONV_W - 1, 3 * DN_W), lambda i, n: (i, 0, 0)),
                   pl.BlockSpec((bt, DN_HEADS, DN_DK, DN_DK), lambda i, n: (i, 0, 0, 0))],
        out_shape=[jax.ShapeDtypeStruct((nb, length, DN_W), BF16),
                   jax.ShapeDtypeStruct((nb, CONV_W - 1, 3 * DN_W), F32),
                   jax.ShapeDtypeStruct((nb, DN_HEADS, DN_DK, DN_DK), F32)],
        scratch_shapes=[pltpu.VMEM((bt, 8 + chunk, 3 * DN_W), F32),
                        pltpu.VMEM((bt, DN_W, DN_W), F32)],
        compiler_params=_cparams("arbitrary", "arbitrary"),
        name="gated_delta",
    )(d3, z3, b3, a3, conv0, s0, cw, alog4, dtb4, dnw4, *_gdn_constants(chunk))


FF_CHUNK = D_FF // 2


def _mix_residual(x_ref, yp_ref, ya_ref, yd_ref, g1_ref, sh2_ref, sc2_ref, n2_ref, wo_ref):
    mix = jnp.concatenate([yp_ref[...].astype(BF16), ya_ref[...].astype(BF16), yd_ref[...].astype(BF16)],
                          axis=1)
    x1 = x_ref[...] + g1_ref[...] * jnp.dot(mix, wo_ref[...], preferred_element_type=F32)
    ms = jnp.mean(x1 * x1, axis=-1, keepdims=True)
    h2 = ((x1 * lax.rsqrt(ms + EPS) * n2_ref[...]) * (1.0 + sc2_ref[...]) + sh2_ref[...]).astype(BF16)
    return x1, h2


def _out_kernel(x_ref, yp_ref, ya_ref, yd_ref, g1_ref, sh2_ref, sc2_ref, g2_ref, n2_ref,
                wo_ref, wg_ref, wu_ref, wd_ref, o_ref):
    x1, h2 = _mix_residual(x_ref, yp_ref, ya_ref, yd_ref, g1_ref, sh2_ref, sc2_ref, n2_ref, wo_ref)
    acc = None
    for c in range(D_FF // FF_CHUNK):
        c0 = c * FF_CHUNK
        gt = jnp.dot(h2, wg_ref[:, c0:c0 + FF_CHUNK], preferred_element_type=F32)
        up = jnp.dot(h2, wu_ref[:, c0:c0 + FF_CHUNK], preferred_element_type=F32)
        act = (gt * _sigmoid(gt) * up).astype(BF16)
        part = jnp.dot(act, wd_ref[c0:c0 + FF_CHUNK, :], preferred_element_type=F32)
        acc = part if acc is None else acc + part
    o_ref[...] = x1 + g2_ref[...] * acc


def _out_ffn(x3, yp, ya, yd, mod, l, per_token, n2, wo, wg, wu, wd, tm):
    nb, length, _ = x3.shape
    tok = lambda w: pl.BlockSpec((None, tm, w), lambda b, t: (b, t, 0))
    single = pl.Buffered(1)
    return pl.pallas_call(
        _out_kernel,
        grid=(nb, length // tm),
        in_specs=[tok(D_MODEL), tok(POOL_W), tok(ATT_W), tok(DN_W),
                  _mod_spec(mod, l, 2, per_token), _mod_spec(mod, l, 3, per_token),
                  _mod_spec(mod, l, 4, per_token), _mod_spec(mod, l, 5, per_token),
                  _layer_spec(n2, l),
                  _layer_spec(wo, l, pipeline_mode=single), _layer_spec(wg, l, pipeline_mode=single),
                  _layer_spec(wu, l, pipeline_mode=single), _layer_spec(wd, l, pipeline_mode=single)],
        out_specs=tok(D_MODEL),
        out_shape=jax.ShapeDtypeStruct((nb, length, D_MODEL), F32),
        compiler_params=_cparams("arbitrary", "arbitrary"),
        name="out_ffn",
    )(x3, yp, ya, yd, mod, mod, mod, mod, n2, wo, wg, wu, wd)


def kernel(x_prompt, x_sample, cache_k, cache_v, state_pool, state_conv, state_delta, page_table,
           c_prompt, c_sample, norm1_w, ada_w, ada_b, w_in, pool_w, pool_scale, q_norm_w, k_norm_w,
           conv_w, a_log, dt_bias, dn_norm_w, w_out, norm2_w, w_gate, w_up, w_down):
    depth = w_in.shape[0]
    nbp, seq, _ = x_prompt.shape
    nbs = x_sample.shape[0]
    past_len = page_table.shape[1] * PAGE

    n_main = C_Z[1]
    w_ext = jnp.concatenate(
        [w_in[:, :, :n_main],
         jnp.repeat(w_in[:, :, n_main:n_main + DN_HEADS], DN_DK, axis=-1),
         jnp.repeat(w_in[:, :, n_main + DN_HEADS:], DN_DK, axis=-1)], axis=-1).astype(BF16)
    wo_b, wg_b, wu_b, wd_b = (w.astype(BF16) for w in (w_out, w_gate, w_up, w_down))
    eye_g = jnp.eye(len(POOL_WINDOWS), dtype=F32)
    pw_bd = (eye_g[None, :, None, :, None] * pool_w[:, :, :, None, :]).reshape(depth, POOL_W, POOL_W).astype(BF16)
    ones_att = jnp.kron(jnp.eye(ATT_HEADS, dtype=F32), jnp.ones((HEAD_DIM, HEAD_DIM), F32)).astype(BF16)
    qn4 = jnp.tile(q_norm_w, (1, ATT_HEADS)).reshape(depth, 1, ATT_W)
    kn4 = jnp.tile(k_norm_w, (1, ATT_HEADS)).reshape(depth, 1, ATT_W)
    dnw4 = jnp.tile(dn_norm_w, (1, DN_HEADS)).reshape(depth, 1, DN_W)
    alog4 = jnp.repeat(a_log, DN_DK, axis=-1).reshape(depth, 1, DN_W)
    dtb4 = jnp.repeat(dt_bias, DN_DK, axis=-1).reshape(depth, 1, DN_W)
    n1 = norm1_w.reshape(depth, 1, D_MODEL)
    n2 = norm2_w.reshape(depth, 1, D_MODEL)
    ps = pool_scale.reshape(depth, 1, POOL_W)
    cache_kt = jnp.transpose(cache_k, (0, 1, 3, 4, 2))
    cache_vt = jnp.transpose(cache_v, (0, 1, 3, 4, 2))

    mod = _modulation(jnp.concatenate([c_prompt, c_sample], axis=0), ada_w, ada_b)
    mod_p = mod[:, :nbp].reshape(depth, nbp, N_MOD, 1, D_MODEL)
    mod_s = mod[:, nbp:].reshape(depth, nbs, N_MOD, D_MODEL).transpose(0, 2, 1, 3)

    xp = x_prompt
    xs = x_sample.reshape(1, nbs, D_MODEL)
    zero_pool = jnp.zeros((nbp, POOL_BUF, POOL_W), F32)
    zero_pool_s = jnp.zeros((nbs, POOL_BUF, POOL_W), F32)
    zero_conv = jnp.zeros((nbp, CONV_W - 1, 3 * DN_W), F32)
    zero_state = jnp.zeros((nbp, DN_HEADS, DN_DK, DN_DK), F32)
    pad_rows = lambda a: jnp.pad(a.reshape(nbs, 1, a.shape[-1]), ((0, 0), (0, SAMPLE_CHUNK - 1), (0, 0)))

    outs = {k: [] for k in ("ks", "vs", "pp", "ps", "cp", "cs", "sp", "ss")}
    kv_all = None
    for l in range(depth):
        u, q, k_all, v_all, d, z, b4, a4 = _in_proj(xp, mod_p, l, False, n1, w_ext, qn4, kn4, ones_att,
                                                    2 * TM, kv_depth=depth, kv_prev=kv_all)
        kv_all = (k_all, v_all)
        y_pool, pool_new = _pool(u, zero_pool, pw_bd, ps, 0, l)
        y_att = _attn_prompt(q, k_all, v_all, l)
        y_dn, conv_new, s_new = _gdn(d, z, b4, a4, zero_conv, zero_state, conv_w, alog4, dtb4, dnw4,
                                     l, seq, GDN_BT, CHUNK)
        xp = _out_ffn(xp, y_pool, y_att, y_dn, mod_p, l, False, n2, wo_b, wg_b, wu_b, wd_b, TM)
        outs["pp"].append(pool_new)
        outs["cp"].append(conv_new)
        outs["sp"].append(s_new)

        us, qs, ks, vs, ds, zs, b4s, a4s = _in_proj(xs, mod_s, l, True, n1, w_ext, qn4, kn4, ones_att, nbs)
        ext = jnp.concatenate([state_pool[l], us.reshape(nbs, 1, POOL_W)], axis=1)
        y_pool16, pool_new = _pool(ext, zero_pool_s, pw_bd, ps, past_len - POOL_BUF, l, bs=8)
        y_pool = y_pool16[:, POOL_BUF:].reshape(1, nbs, POOL_W)
        o_att = _attn_sample(qs.reshape(nbs, ATT_HEADS, HEAD_DIM).astype(F32),
                             ks.reshape(nbs, ATT_HEADS, HEAD_DIM), vs.reshape(nbs, ATT_HEADS, HEAD_DIM),
                             cache_kt, cache_vt, page_table, l)
        y_att = o_att.reshape(1, nbs, ATT_W)
        y_dnc, conv_new, s_new = _gdn(pad_rows(ds), pad_rows(zs), pad_rows(b4s), pad_rows(a4s),
                                      state_conv, state_delta, conv_w, alog4, dtb4, dnw4,
                                      l, 1, GDN_BT, SAMPLE_CHUNK, stacked_state=True)
        y_dn = y_dnc[:, 0].reshape(1, nbs, DN_W)
        xs = _out_ffn(xs, y_pool, y_att, y_dn, mod_s, l, True, n2, wo_b, wg_b, wu_b, wd_b, nbs)
        outs["ks"].append(ks.reshape(nbs, 1, ATT_HEADS, HEAD_DIM))
        outs["vs"].append(vs.reshape(nbs, 1, ATT_HEADS, HEAD_DIM))
        outs["ps"].append(pool_new)
        outs["cs"].append(conv_new)
        outs["ss"].append(s_new)

    st = lambda name: jnp.stack(outs[name])
    k_prompt, v_prompt = (a.reshape(depth, nbp, seq, ATT_HEADS, HEAD_DIM) for a in kv_all)
    return (xp, xs.reshape(nbs, 1, D_MODEL), k_prompt, v_prompt, st("ks"), st("vs"),
            st("pp"), st("ps"), st("cp"), st("cs"), st("sp"), st("ss"))
```

```python
import functools

import numpy as np
import jax
import jax.numpy as jnp
from jax import lax
from jax.experimental import pallas as pl
from jax.experimental.pallas import tpu as pltpu

F32 = jnp.float32
BF16 = jnp.bfloat16

D_MODEL = 1024
PAGE = 128
POOL_WINDOWS = (2, 4, 8, 16)
POOL_GDIM = 64
POOL_W = 256
POOL_BUF = 15
HEAD_DIM = 64
ATT_HEADS = 8
ATT_W = 512
BLK = 256
TOPK = 3
DN_HEADS = 4
DN_DK = 64
DN_W = 256
CHUNK = 64
SAMPLE_CHUNK = 16
CONV_W = 4
D_FF = 2816
N_MOD = 6
EPS = 1e-6
NEG = -1e30
LOG2E = 1.4426950408889634

C_U = (0, 256)
C_Q = (256, 768)
C_K = (768, 1280)
C_V = (1280, 1792)
C_D = (1792, 2560)
C_Z = (2560, 2816)
C_G = (2816, 2944)
IN_EXT = 2944
GATE_W = 128

TM = 512
GDN_BT = 8
SA_PAGES_PER_STEP = 32
VMEM_LIMIT = 56 * 1024 * 1024


def _cparams(*sem):
    return pltpu.CompilerParams(dimension_semantics=sem, vmem_limit_bytes=VMEM_LIMIT)


def _sigmoid(x):
    return 1.0 / (1.0 + jnp.exp(-x))


def _bdot(a, b):
    return jnp.dot(a.astype(BF16), b.astype(BF16), preferred_element_type=F32)


def _bdot_t(a, b):
    return lax.dot_general(a.astype(BF16), b.astype(BF16), (((1,), (1,)), ((), ())),
                           preferred_element_type=F32)


def _split3(a):
    hi = a.astype(BF16)
    r1 = a - hi.astype(F32)
    mid = r1.astype(BF16)
    lo = (r1 - mid.astype(F32)).astype(BF16)
    return hi, mid, lo


def _dot3_rhs_exact(a, b01):
    return sum(jnp.dot(p, b01, preferred_element_type=F32) for p in _split3(a))


def _dot3_lhs_exact(a01, b):
    return sum(jnp.dot(a01, p, preferred_element_type=F32) for p in _split3(b))


def _dot3_t_rhs_exact(a, b01):
    return sum(lax.dot_general(p, b01, (((1,), (1,)), ((), ())), preferred_element_type=F32)
               for p in _split3(a))


def _dot3_t_lhs_exact(a01, b):
    return sum(lax.dot_general(a01, p, (((1,), (1,)), ((), ())), preferred_element_type=F32)
               for p in _split3(b))


def _mod_kernel(c_ref, w_ref, b_ref, o_ref):
    c = c_ref[...]
    a = (c * _sigmoid(c)).astype(BF16)
    o_ref[...] = jnp.dot(a, w_ref[...].astype(BF16), preferred_element_type=F32) + b_ref[...]


def _modulation(c_all, ada_w, ada_b):
    depth, _, ncol = ada_w.shape
    nseq = c_all.shape[0]
    tn = 1536
    return pl.pallas_call(
        _mod_kernel,
        grid=(depth, ncol // tn),
        in_specs=[pl.BlockSpec((nseq, D_MODEL), lambda l, j: (0, 0)),
                  pl.BlockSpec((None, D_MODEL, tn), lambda l, j: (l, 0, j)),
                  pl.BlockSpec((None, 1, tn), lambda l, j: (l, 0, j))],
        out_specs=pl.BlockSpec((None, nseq, tn), lambda l, j: (l, 0, j)),
        out_shape=jax.ShapeDtypeStruct((depth, nseq, ncol), F32),
        compiler_params=_cparams("arbitrary", "arbitrary"),
        name="modulation",
    )(c_all, ada_w, ada_b.reshape(depth, 1, ncol))


def _layer_spec(stacked, l, **kw):
    zeros = (0,) * (stacked.ndim - 1)
    return pl.BlockSpec((None,) + stacked.shape[1:], lambda *g: (l,) + zeros, **kw)


def _mod_spec(mod, l, k, per_token):
    if per_token:
        return pl.BlockSpec((None, None, mod.shape[2], D_MODEL), lambda *g: (l, k, 0, 0))
    return pl.BlockSpec((None, None, None, 1, D_MODEL), lambda *g: (l, g[0], k, 0, 0))


def _in_kernel(x_ref, sh_ref, sc_ref, n1_ref, w_ref, qn_ref, kn_ref, ones_ref, *rest):
    u_ref, q_ref, k_ref, v_ref, d_ref, z_ref, g_ref = rest[-7:]
    x = x_ref[...]
    ms = jnp.mean(x * x, axis=-1, keepdims=True)
    h = (x * lax.rsqrt(ms + EPS) * n1_ref[...]) * (1.0 + sc_ref[...]) + sh_ref[...]
    hb = h.astype(BF16)

    def proj(c):
        return jnp.dot(hb, w_ref[:, c[0]:c[1]], preferred_element_type=F32)

    def head_norm(t, w4):
        ss = jnp.dot((t * t).astype(BF16), ones_ref[...], preferred_element_type=F32)
        return t * lax.rsqrt(ss * (1.0 / HEAD_DIM) + EPS) * w4

    u_ref[...] = proj(C_U)
    q_ref[...] = (head_norm(proj(C_Q), qn_ref[...]) * (HEAD_DIM ** -0.5 * LOG2E)).astype(BF16)
    k_ref[...] = head_norm(proj(C_K), kn_ref[...])
    v_ref[...] = proj(C_V)
    d_ref[...] = proj(C_D)
    z_ref[...] = proj(C_Z)
    g_ref[...] = proj(C_G)


def _in_proj(x3, mod, l, per_token, n1, w_ext, qn4, kn4, ones_att, tm, kv_depth=None, kv_prev=None):
    nb, length, _ = x3.shape
    widths = (POOL_W, ATT_W, ATT_W, ATT_W, 3 * DN_W, DN_W, GATE_W)
    dtypes = (F32, BF16, F32, F32, F32, F32, F32)
    const = lambda b, t: (0, 0)
    tok = lambda w: pl.BlockSpec((None, tm, w), lambda b, t: (b, t, 0))
    out_specs = [tok(w) for w in widths]
    out_shape = [jax.ShapeDtypeStruct((nb, length, w), dt) for w, dt in zip(widths, dtypes)]
    operands = [x3, mod, mod, n1, w_ext, qn4, kn4, ones_att]
    in_specs = [tok(D_MODEL), _mod_spec(mod, l, 0, per_token), _mod_spec(mod, l, 1, per_token),
                _layer_spec(n1, l), _layer_spec(w_ext, l), _layer_spec(qn4, l), _layer_spec(kn4, l),
                pl.BlockSpec((ATT_W, ATT_W), const)]
    aliases = {}
    if kv_depth is not None:
        for o in (2, 3):
            out_specs[o] = pl.BlockSpec((None, None, tm, ATT_W), lambda b, t: (l, b, t, 0))
            out_shape[o] = jax.ShapeDtypeStruct((kv_depth, nb, length, ATT_W), F32)
        if kv_prev is not None:
            aliases = {len(operands): 2, len(operands) + 1: 3}
            operands += list(kv_prev)
            in_specs += [pl.BlockSpec(memory_space=pl.ANY)] * 2
    return pl.pallas_call(
        _in_kernel,
        grid=(nb, length // tm),
        in_specs=in_specs,
        out_specs=out_specs,
        out_shape=out_shape,
        input_output_aliases=aliases,
        compiler_params=_cparams("arbitrary", "arbitrary"),
        name="in_proj",
    )(*operands)


POOL_ROW0 = 32


def _pool_kernel(u_ref, buf_ref, pw_ref, ps_ref, y_ref, new_ref, e_scr, s2_scr, s4_scr, s8_scr, *,
                 length, pos0, bs):
    zeros16 = jnp.zeros((16, POOL_W), F32)
    lane = lax.broadcasted_iota(jnp.int32, (1, POOL_W), 1)
    grp = jnp.right_shift(lane, 6)
    wl = jnp.where(grp == 0, 2, jnp.where(grp == 1, 4, jnp.where(grp == 2, 8, 16)))
    ch = min(length, 256)
    row = lax.broadcasted_iota(jnp.int32, (ch, 1), 0)
    spans = [(16, 16)] + [(POOL_ROW0 + c * ch, ch) for c in range(length // ch)]
    for s in range(bs):
        e_scr[s, pl.ds(8, 16), :] = zeros16
        s2_scr[s, pl.ds(8, 8), :] = zeros16[:8]
        s4_scr[s, pl.ds(8, 8), :] = zeros16[:8]
        e_scr[s, pl.ds(POOL_ROW0 - POOL_BUF, POOL_BUF), :] = buf_ref[s]
        e_scr[s, pl.ds(POOL_ROW0, length), :] = u_ref[s]
        for src, dst, shift in ((e_scr, s2_scr, 1), (s2_scr, s4_scr, 2), (s4_scr, s8_scr, 4)):
            for base, n in spans:
                dst[s, pl.ds(base, n), :] = src[s, pl.ds(base, n), :] + src[s, pl.ds(base - shift, n), :]
        for c in range(length // ch):
            base = POOL_ROW0 + c * ch
            cur = e_scr[s, pl.ds(base, ch), :]
            s16 = s8_scr[s, pl.ds(base, ch), :] + s8_scr[s, pl.ds(base - 8, ch), :]
            wsum = jnp.where(grp == 0, s2_scr[s, pl.ds(base, ch), :],
                             jnp.where(grp == 1, s4_scr[s, pl.ds(base, ch), :],
                                       jnp.where(grp == 2, s8_scr[s, pl.ds(base, ch), :], s16)))
            cnt = jnp.minimum(wl, row + (pos0 + c * ch + 1)).astype(F32)
            d = wsum / cnt - cur
            y = jnp.dot(d.astype(BF16), pw_ref[...], preferred_element_type=F32) * ps_ref[...]
            y_ref[s, pl.ds(c * ch, ch), :] = y.astype(y_ref.dtype)
        new_ref[s] = e_scr[s, pl.ds(POOL_ROW0 + length - POOL_BUF, POOL_BUF), :]


def _pool(u3, buf3, pw_bd, ps, pos0, l, bs=1):
    nb, length, _ = u3.shape
    return pl.pallas_call(
        functools.partial(_pool_kernel, length=length, pos0=pos0, bs=bs),
        grid=(nb // bs,),
        in_specs=[pl.BlockSpec((bs, length, POOL_W), lambda b: (b, 0, 0)),
                  pl.BlockSpec((bs, POOL_BUF, POOL_W), lambda b: (b, 0, 0)),
                  _layer_spec(pw_bd, l), _layer_spec(ps, l)],
        out_specs=[pl.BlockSpec((bs, length, POOL_W), lambda b: (b, 0, 0)),
                   pl.BlockSpec((bs, POOL_BUF, POOL_W), lambda b: (b, 0, 0))],
        out_shape=[jax.ShapeDtypeStruct((nb, length, POOL_W), BF16),
                   jax.ShapeDtypeStruct((nb, POOL_BUF, POOL_W), F32)],
        scratch_shapes=[pltpu.VMEM((bs, POOL_ROW0 + length, POOL_W), F32)] * 4,
        compiler_params=_cparams("arbitrary"),
        name="pool_mixer",
    )(u3, buf3, pw_bd, ps)


def _attn_kernel(q_ref, k_ref, v_ref, o_ref, km_scr, vt_scr, ot_scr, s_scr, p_scr, *, length):
    nb = length // BLK
    lane = lax.broadcasted_iota(jnp.int32, (1, 128), 1)
    k2 = k_ref[...]
    kmean = jnp.sum(k2.reshape(nb, BLK, 128), axis=1) * (1.0 / BLK)
    vt_scr[...] = v_ref[...].T.astype(BF16)
    krow = lax.broadcasted_iota(jnp.int32, (BLK, BLK), 0)
    qcol = lax.broadcasted_iota(jnp.int32, (BLK, BLK), 1)
    causal = krow <= qcol
    blkrow = lax.broadcasted_iota(jnp.int32, (nb, BLK), 0)
    pad_rows = jnp.zeros((16 - nb, 128), F32) if nb < 16 else None
    kmh = []
    for hh in range(2):
        hm = jnp.right_shift(lane, 6) == hh
        km_scr[hh] = jnp.where(hm, k2, 0.0).astype(BF16)
        t = jnp.where(hm, kmean, 0.0)
        kmh.append(t if pad_rows is None else jnp.concatenate([t, pad_rows], axis=0))

    def fold8(t):
        return t.reshape(BLK // 8, 8, BLK)

    def scores(hh, i, slot):
        qi = q_ref[pl.ds(i * BLK, BLK), :]
        bias = None
        if i > 0:
            gate = _bdot_t(kmh[hh], qi)[:nb]
            past = blkrow < i
            gate = jnp.where(past, gate, -jnp.inf)
            cnt = jnp.zeros((nb, BLK), jnp.int32)
            for jp in range(i):
                gj = gate[jp:jp + 1, :]
                cnt = cnt + jnp.where(gj > gate, 1,
                                      jnp.where(gj == gate, (jp < blkrow).astype(jnp.int32), 0))
            sel = jnp.where(past, cnt, TOPK) < TOPK
            bias = jnp.where(sel, 0.0, NEG)
        m8 = None
        for j in range(i + 1):
            s = _bdot_t(km_scr[hh, j * BLK:(j + 1) * BLK, :], qi)
            s = jnp.where(causal, s, NEG) if j == i else s + bias[j:j + 1, :]
            s_scr[slot, j] = s
            t = jnp.max(fold8(s), axis=0)
            m8 = t if m8 is None else jnp.maximum(m8, t)
        return jnp.max(m8, axis=0, keepdims=True)

    def weighted(hh, i, slot, m):
        l8 = None
        for j in range(i + 1):
            p = jnp.exp2(s_scr[slot, j] - m)
            t = jnp.sum(fold8(p), axis=0)
            l8 = t if l8 is None else l8 + t
            p_scr[slot, j * BLK:(j + 1) * BLK, :] = p.astype(BF16)
        acc = jnp.dot(vt_scr[hh * 64:(hh + 1) * 64, 0:(i + 1) * BLK], p_scr[slot, 0:(i + 1) * BLK, :],
                      preferred_element_type=F32)
        l = jnp.sum(l8, axis=0, keepdims=True)
        ot_scr[hh * 64:(hh + 1) * 64, i * BLK:(i + 1) * BLK] = acc / l

    items = [(hh, i) for hh in range(2) for i in range(nb)]
    pending = None
    for n, (hh, i) in enumerate(items):
        m = scores(hh, i, n % 2)
        if pending is not None:
            weighted(*pending)
        pending = (hh, i, n % 2, m)
    weighted(*pending)
    o_ref[...] = ot_scr[...].T.astype(o_ref.dtype)


def _attn_prompt(q3, k_all, v_all, l):
    nb, length, _ = q3.shape
    nblk = length // BLK
    spec = pl.BlockSpec((None, length, 128), lambda b, h: (b, 0, h))
    kv_spec = pl.BlockSpec((None, None, length, 128), lambda b, h: (l, b, 0, h))
    return pl.pallas_call(
        functools.partial(_attn_kernel, length=length),
        grid=(nb, ATT_W // 128),
        in_specs=[spec, kv_spec, kv_spec],
        out_specs=spec,
        out_shape=jax.ShapeDtypeStruct((nb, length, ATT_W), BF16),
        scratch_shapes=[pltpu.VMEM((2, length, 128), BF16),
                        pltpu.VMEM((128, length), BF16),
                        pltpu.VMEM((128, length), F32),
                        pltpu.VMEM((2, nblk, BLK, BLK), F32),
                        pltpu.VMEM((2, length, BLK), BF16)],
        compiler_params=_cparams("arbitrary", "arbitrary"),
        name="moba_prompt",
    )(q3, k_all, v_all)


def _sa_partials(kp, vp, qbc_ref, s_scr, g_scr, m_scr, l_scr, o_scr, first_blk):
    full = (ATT_HEADS, PAGE)
    for pg in range(len(kp)):
        for h in range(ATT_HEADS):
            s_scr[pg, pl.ds(h, 1), :] = jnp.sum(kp[pg][h] * qbc_ref[h], axis=0, keepdims=True)
    for blk in range(len(kp) // 2):
        s0 = s_scr[2 * blk]
        s1 = s_scr[2 * blk + 1]
        gsum = jnp.sum(s0, axis=-1, keepdims=True) + jnp.sum(s1, axis=-1, keepdims=True)
        mb = jnp.maximum(jnp.max(s0, axis=-1, keepdims=True), jnp.max(s1, axis=-1, keepdims=True))
        p0 = jnp.exp2(s0 - mb)
        p1 = jnp.exp2(s1 - mb)
        lb = jnp.sum(p0, axis=-1, keepdims=True) + jnp.sum(p1, axis=-1, keepdims=True)
        s_scr[2 * blk] = p0
        s_scr[2 * blk + 1] = p1
        idx = first_blk + blk
        g_scr[idx] = jnp.broadcast_to(gsum, full)
        m_scr[idx] = jnp.broadcast_to(mb, full)
        l_scr[idx] = jnp.broadcast_to(lb, full)
        for h in range(ATT_HEADS):
            o_scr[idx, h] = (vp[2 * blk][h] * s_scr[2 * blk, pl.ds(h, 1), :]
                             + vp[2 * blk + 1][h] * s_scr[2 * blk + 1, pl.ds(h, 1), :])


def _sa_merge(q_ref, kn_ref, vn_ref, o_ref, g_scr, m_scr, l_scr, o_scr, w_scr, nblk):
    full = (ATT_HEADS, PAGE)
    gates = g_scr[pl.ds(0, nblk)]
    if nblk > TOPK:
        lane = lax.broadcasted_iota(jnp.int32, full, 1)
        gates_c = jnp.zeros(full, F32)
        for b in range(nblk):
            gates_c = jnp.where(lane == b, gates[b], gates_c)
        cnt = jnp.zeros(full, jnp.int32)
        for b in range(nblk):
            cnt = cnt + jnp.where(gates[b] > gates_c, 1,
                                  jnp.where(gates[b] == gates_c, (b < lane).astype(jnp.int32), 0))
        last = jnp.where(lane < nblk, cnt, -1) == TOPK - 1
        g3 = jnp.max(jnp.where(last, gates_c, -jnp.inf), axis=-1, keepdims=True)
        b3 = jnp.max(jnp.where(last, lane.astype(F32), -1.0), axis=-1, keepdims=True)
        bidx = lax.broadcasted_iota(jnp.int32, gates.shape, 0).astype(F32)
        sel = (gates > g3[None]) | ((gates == g3[None]) & (bidx <= b3[None]))
    else:
        sel = jnp.full(gates.shape, True)
    sself = jnp.broadcast_to(jnp.sum(q_ref[...] * kn_ref[...], axis=-1, keepdims=True), full)
    mb = m_scr[pl.ds(0, nblk)]
    mtot = jnp.maximum(jnp.max(jnp.where(sel, mb, -jnp.inf), axis=0), sself)
    w = jnp.where(sel, jnp.exp2(jnp.where(sel, mb - mtot[None], 0.0)), 0.0)
    wself = jnp.exp2(sself - mtot)
    denom = jnp.sum(w * l_scr[pl.ds(0, nblk)], axis=0) + wself
    w_scr[...] = w
    ones8 = jnp.ones((8, PAGE), BF16)
    sub = lax.broadcasted_iota(jnp.int32, (ATT_HEADS, HEAD_DIM), 0)
    o_acc = jnp.zeros((ATT_HEADS, HEAD_DIM), F32)
    for h in range(ATT_HEADS):
        tot = w_scr[0, pl.ds(h, 1), :] * o_scr[0, h]
        for b in range(1, nblk):
            tot = tot + w_scr[b, pl.ds(h, 1), :] * o_scr[b, h]
        r = _dot3_t_lhs_exact(ones8, tot)
        o_acc = jnp.where(sub == h, r, o_acc)
    o_ref[...] = (o_acc + wself[:, :HEAD_DIM] * vn_ref[...]) / denom[:, :HEAD_DIM]


def _sa_kernel(pt_ref, qbc_ref, q_ref, kn_ref, vn_ref, *refs, gpages, n_steps):
    kp = refs[:gpages]
    vp = refs[gpages:2 * gpages]
    o_ref = refs[2 * gpages]
    g_scr, m_scr, l_scr, o_scr, s_scr, w_scr = refs[2 * gpages + 1:]
    step = pl.program_id(1)
    _sa_partials(kp, vp, qbc_ref, s_scr, g_scr, m_scr, l_scr, o_scr, step * (gpages // 2))

    @pl.when(step == n_steps - 1)
    def _merge():
        _sa_merge(q_ref, kn_ref, vn_ref, o_ref, g_scr, m_scr, l_scr, o_scr, w_scr, n_steps * (gpages // 2))


def _attn_sample(q3, kn3, vn3, cache_kt, cache_vt, page_table, l):
    nb = q3.shape[0]
    n_pages = page_table.shape[1]
    gpages = SA_PAGES_PER_STEP
    n_steps = n_pages // gpages
    nblk = n_pages // 2
    qbc = jnp.broadcast_to(q3[..., None], (nb, ATT_HEADS, HEAD_DIM, PAGE))
    tok = lambda: pl.BlockSpec((None, ATT_HEADS, HEAD_DIM), lambda b, s, pt: (b, 0, 0))

    def page_spec(i):
        return pl.BlockSpec((None, None, ATT_HEADS, HEAD_DIM, PAGE),
                            lambda b, s, pt: (l, pt[b, s * gpages + i], 0, 0, 0))

    grid_spec = pltpu.PrefetchScalarGridSpec(
        num_scalar_prefetch=1,
        grid=(nb, n_steps),
        in_specs=[pl.BlockSpec((None, ATT_HEADS, HEAD_DIM, PAGE), lambda b, s, pt: (b, 0, 0, 0)),
                  tok(), tok(), tok()]
                 + [page_spec(i) for i in range(gpages)] + [page_spec(i) for i in range(gpages)],
        out_specs=tok(),
        scratch_shapes=[pltpu.VMEM((nblk, ATT_HEADS, PAGE), F32),
                        pltpu.VMEM((nblk, ATT_HEADS, PAGE), F32),
                        pltpu.VMEM((nblk, ATT_HEADS, PAGE), F32),
                        pltpu.VMEM((nblk, ATT_HEADS, HEAD_DIM, PAGE), F32),
                        pltpu.VMEM((gpages, ATT_HEADS, PAGE), F32),
                        pltpu.VMEM((nblk, ATT_HEADS, PAGE), F32)],
    )
    return pl.pallas_call(
        functools.partial(_sa_kernel, gpages=gpages, n_steps=n_steps),
        grid_spec=grid_spec,
        out_shape=jax.ShapeDtypeStruct((nb, ATT_HEADS, HEAD_DIM), F32),
        compiler_params=_cparams("arbitrary", "arbitrary"),
        name="moba_sample",
    )(page_table, qbc, q3, kn3, vn3, *([cache_kt] * gpages), *([cache_vt] * gpages))


def _gdn_constants(chunk):
    C = chunk
    lane = np.arange(DN_W)
    lane_j = lane % 64
    row = np.arange(C)[:, None]
    tri = np.stack([lane_j[None] <= row, lane_j[None] < row, lane_j[None] == row, row <= lane_j[None]])
    l_incl = np.arange(C)[None, :] <= np.arange(C)[:, None]
    blk = np.arange(DN_W) // 64
    bd = blk[:, None] == blk[None, :]
    place = np.stack([np.arange(DN_W)[None, :] == (np.arange(DN_DK)[:, None] + 64 * h) for h in range(DN_HEADS)])
    half = np.stack([np.broadcast_to((np.arange(128) // 64) == s, (C, 128)) for s in range(2)])
    expand = np.arange(GATE_W)[:, None] == (np.arange(2 * DN_W)[None, :] // 64)
    return (jnp.asarray(tri, F32), jnp.asarray(l_incl, BF16), jnp.asarray(bd, BF16), jnp.asarray(bd, F32),
            jnp.asarray(place, BF16), jnp.asarray(half, F32), jnp.asarray(expand, BF16))


def _gdn_kernel(d_ref, z_ref, g_ref, c0_ref, s0_ref, cw_ref, alog_ref, dtb_ref, dnw_ref,
                tri_ref, lincl_ref, onesbd_ref, bdmask_ref, place_ref, half_ref, expand_ref,
                y_ref, cout_ref, sout_ref, ext_scr, s_scr, *, bt, n_chunks, l_valid, chunk):
    n = pl.program_id(1)
    C = chunk
    incl4 = tri_ref[0] > 0.5
    strict4 = tri_ref[1] > 0.5
    eye4 = tri_ref[2]
    ut4 = tri_ref[3]
    l_incl = lincl_ref[...]
    ones_bd = onesbd_ref[...]
    zero_half = jnp.zeros((C, 128), F32)
    left = half_ref[0] > 0.5
    right = half_ref[1] > 0.5

    def stack_mask(a4):
        lo = a4[:, :128]
        hi = a4[:, 128:]
        pieces = [jnp.concatenate([jnp.where(left, lo, 0.0), zero_half], axis=1),
                  jnp.concatenate([jnp.where(right, lo, 0.0), zero_half], axis=1),
                  jnp.concatenate([zero_half, jnp.where(left, hi, 0.0)], axis=1),
                  jnp.concatenate([zero_half, jnp.where(right, hi, 0.0)], axis=1)]
        if C < DN_DK:
            fill = jnp.zeros((DN_DK - C, DN_W), F32)
            pieces = [x for p in pieces for x in (p, fill)]
        return jnp.concatenate(pieces, axis=0).astype(BF16)

    @pl.when(n == 0)
    def _init():
        for bi in range(bt):
            ext_scr[bi, pl.ds(5, CONV_W - 1), :] = c0_ref[bi]
            s_scr[bi] = jnp.concatenate(
                [_dot3_rhs_exact(s0_ref[bi, h], place_ref[h]) for h in range(DN_HEADS)], axis=0)

    last_valid = l_valid - (n_chunks - 1) * C
    masked = last_valid != C
    if masked:
        row = lax.broadcasted_iota(jnp.int32, (C, 1), 0)
        valid = (n * C + row) < l_valid

    seqs = range(bt)
    each = lambda f, *cols: [f(*xs) for xs in zip(*cols)]
    cw = cw_ref[...]
    act = []
    for bi in seqs:
        ext_scr[bi, pl.ds(8, C), :] = d_ref[bi]
        yc = (cw[0:1] * ext_scr[bi, pl.ds(5, C), :] + cw[1:2] * ext_scr[bi, pl.ds(6, C), :]
              + cw[2:3] * ext_scr[bi, pl.ds(7, C), :] + cw[3:4] * ext_scr[bi, pl.ds(8, C), :])
        act.append(yc * _sigmoid(yc))
        cout_ref[bi] = ext_scr[bi, pl.ds(last_valid + 5, CONV_W - 1), :]
        ext_scr[bi, pl.ds(5, CONV_W - 1), :] = ext_scr[bi, pl.ds(8 + C - (CONV_W - 1), CONV_W - 1), :]

    q_raw = [a[:, 0:DN_W] for a in act]
    k_raw = [a[:, DN_W:2 * DN_W] for a in act]
    v4 = [a[:, 2 * DN_W:3 * DN_W] for a in act]
    ssq = each(lambda x: _bdot(x * x, ones_bd), q_raw)
    ssk = each(lambda x: _bdot(x * x, ones_bd), k_raw)
    q4 = each(lambda x, s: x * lax.rsqrt(s + EPS) * (DN_DK ** -0.5), q_raw, ssq)
    k4 = each(lambda x, s: x * lax.rsqrt(s + EPS), k_raw, ssk)
    raw = [_dot3_rhs_exact(g_ref[bi], expand_ref[...]) for bi in seqs]
    beta4 = [_sigmoid(r[:, :DN_W]) for r in raw]

    def log_decay(r):
        xg = r[:, DN_W:] + dtb_ref[...]
        return -jnp.exp(alog_ref[...]) * (jnp.maximum(xg, 0.0) + jnp.log1p(jnp.exp(-jnp.abs(xg))))

    g4 = [log_decay(r) for r in raw]
    if masked:
        zero_pad = lambda x: jnp.where(valid, x, 0.0)
        q4, k4, v4, beta4, g4 = (each(zero_pad, c) for c in (q4, k4, v4, beta4, g4))

    gi = each(lambda g: _dot3_lhs_exact(l_incl, g), g4)
    gj = each(lambda g: jnp.sum(g * ut4, axis=0, keepdims=True), g4)
    decay4 = each(lambda a, b: jnp.where(incl4, jnp.exp(jnp.where(incl4, a - b, 0.0)), 0.0), gi, gj)
    eg4 = each(jnp.exp, gi)
    glast = [g[C - 1:C, :] for g in gi]
    kfac = each(lambda a, b: jnp.exp(a - b), glast, gi)
    gtot = each(jnp.exp, glast)

    kb4 = each(lambda a, b: a * b, k4, beta4)
    mk = each(lambda kb, q, k: _bdot_t(jnp.concatenate([kb, q], axis=0), stack_mask(k)), kb4, q4, k4)
    attn4 = each(lambda r, dc: r[C:] * dc, mk, decay4)

    p = each(lambda r, dc: -jnp.where(strict4, r[:C] * dc, 0.0), mk, decay4)
    t = each(lambda x: eye4 + x, p)
    p = each(lambda x: _bdot(x, stack_mask(x)), p)
    for _ in range(C.bit_length() - 3):
        r = each(lambda a, b: _bdot(jnp.concatenate([a, b], axis=0), stack_mask(b)), t, p)
        t = each(lambda a, b: a + b[:C], t, r)
        p = [x[C:] for x in r]
    t = each(lambda a, b: a + _bdot(a, stack_mask(b)), t, p)

    u4 = each(lambda a, v, b: _bdot(a, stack_mask(v * b)), t, v4, beta4)
    kc4 = each(lambda a, kb, e: _bdot(a, stack_mask(kb * e)), t, kb4, eg4)

    sbd = [s_scr[bi] for bi in seqs]
    r = each(lambda kc, q, e, s: _bdot(jnp.concatenate([kc, q * e], axis=0), s), kc4, q4, eg4, sbd)
    vnew = each(lambda u, x: u - x[:C], u4, r)
    o4 = each(lambda x, a, v: x[C:] + _bdot(a, stack_mask(v)), r, attn4, vnew)
    upd = each(lambda k, f, v: lax.dot_general((k * f).astype(BF16), v.astype(BF16), (((0,), (0,)), ((), ())),
                                               preferred_element_type=F32), k4, kfac, vnew)
    sso = each(lambda o: _bdot(o * o, ones_bd), o4)
    for bi in seqs:
        s_scr[bi] = sbd[bi] * gtot[bi] + upd[bi] * bdmask_ref[...]
        zz = z_ref[bi]
        y_ref[bi] = (o4[bi] * lax.rsqrt(sso[bi] * (1.0 / DN_DK) + EPS) * dnw_ref[...]
                     * (zz * _sigmoid(zz))).astype(y_ref.dtype)

    @pl.when(n == n_chunks - 1)
    def _fin():
        for bi in range(bt):
            for h in range(DN_HEADS):
                sout_ref[bi, h] = _dot3_t_rhs_exact(s_scr[bi, h * 64:(h + 1) * 64, :], place_ref[h])


def _gdn(d3, z3, g3, conv0, s0, cw, alog4, dtb4, dnw4, l, l_valid, bt, chunk, stacked_state=False):
    nb, length, _ = d3.shape
    n_chunks = length // chunk
    tok = lambda w: pl.BlockSpec((bt, chunk, w), lambda i, n: (i, n, 0))
    const = lambda i, n: (0, 0)
    const3 = lambda i, n: (0, 0, 0)
    conv_blk = (bt, CONV_W - 1, 3 * DN_W)
    state_blk = (bt, DN_HEADS, DN_DK, DN_DK)
    if stacked_state:
        conv_spec = pl.BlockSpec((None,) + conv_blk, lambda i, n: (l, i, 0, 0))
        state_spec = pl.BlockSpec((None,) + state_blk, lambda i, n: (l, i, 0, 0, 0))
    else:
        conv_spec = pl.BlockSpec(conv_blk, lambda i, n: (i, 0, 0))
        state_spec = pl.BlockSpec(state_blk, lambda i, n: (i, 0, 0, 0))
    return pl.pallas_call(
        functools.partial(_gdn_kernel, bt=bt, n_chunks=n_chunks, l_valid=l_valid, chunk=chunk),
        grid=(nb // bt, n_chunks),
        in_specs=[tok(3 * DN_W), tok(DN_W), tok(GATE_W), conv_spec, state_spec,
                  _layer_spec(cw, l), _layer_spec(alog4, l), _layer_spec(dtb4, l), _layer_spec(dnw4, l),
                  pl.BlockSpec((4, chunk, DN_W), const3),
                  pl.BlockSpec((chunk, chunk), const),
                  pl.BlockSpec((DN_W, DN_W), const), pl.BlockSpec((DN_W, DN_W), const),
                  pl.BlockSpec((DN_HEADS, DN_DK, DN_W), const3),
                  pl.BlockSpec((2, chunk, 128), const3),
                  pl.BlockSpec((GATE_W, 2 * DN_W), const)],
        out_specs=[tok(DN_W),
                   pl.BlockSpec((bt, CONV_W - 1, 3 * DN_W), lambda i, n: (i, 0, 0)),
                   pl.BlockSpec((bt, DN_HEADS, DN_DK, DN_DK), lambda i, n: (i, 0, 0, 0))],
        out_shape=[jax.ShapeDtypeStruct((nb, length, DN_W), BF16),
                   jax.ShapeDtypeStruct((nb, CONV_W - 1, 3 * DN_W), F32),
                   jax.ShapeDtypeStruct((nb, DN_HEADS, DN_DK, DN_DK), F32)],
        scratch_shapes=[pltpu.VMEM((bt, 8 + chunk, 3 * DN_W), F32),
                        pltpu.VMEM((bt, DN_W, DN_W), F32)],
        compiler_params=_cparams("arbitrary", "arbitrary"),
        name="gated_delta",
    )(d3, z3, g3, conv0, s0, cw, alog4, dtb4, dnw4, *_gdn_constants(chunk))


MXU_N = 256
FF_CHUNKS = ((0, 6 * MXU_N), (6 * MXU_N, D_FF))


def _mix_residual(x_ref, yp_ref, ya_ref, yd_ref, g1_ref, sh2_ref, sc2_ref, n2_ref, wo_ref):
    mix = jnp.concatenate([yp_ref[...].astype(BF16), ya_ref[...].astype(BF16), yd_ref[...].astype(BF16)],
                          axis=1)
    x1 = x_ref[...] + g1_ref[...] * jnp.dot(mix, wo_ref[...], preferred_element_type=F32)
    ms = jnp.mean(x1 * x1, axis=-1, keepdims=True)
    h2 = ((x1 * lax.rsqrt(ms + EPS) * n2_ref[...]) * (1.0 + sc2_ref[...]) + sh2_ref[...]).astype(BF16)
    return x1, h2


def _out_kernel(x_ref, yp_ref, ya_ref, yd_ref, g1_ref, sh2_ref, sc2_ref, g2_ref, n2_ref,
                wo_ref, wg_ref, wu_ref, wd_ref, o_ref):
    x1, h2 = _mix_residual(x_ref, yp_ref, ya_ref, yd_ref, g1_ref, sh2_ref, sc2_ref, n2_ref, wo_ref)
    acc = None
    for c0, c1 in FF_CHUNKS:
        gt = jnp.dot(h2, wg_ref[:, c0:c1], preferred_element_type=F32)
        up = jnp.dot(h2, wu_ref[:, c0:c1], preferred_element_type=F32)
        act = (gt * _sigmoid(gt) * up).astype(BF16)
        part = jnp.dot(act, wd_ref[c0:c1, :], preferred_element_type=F32)
        acc = part if acc is None else acc + part
    o_ref[...] = x1 + g2_ref[...] * acc


def _out_ffn(x3, yp, ya, yd, mod, l, per_token, n2, wo, wg, wu, wd, tm):
    nb, length, _ = x3.shape
    tok = lambda w: pl.BlockSpec((None, tm, w), lambda b, t: (b, t, 0))
    single = pl.Buffered(1)
    return pl.pallas_call(
        _out_kernel,
        grid=(nb, length // tm),
        in_specs=[tok(D_MODEL), tok(POOL_W), tok(ATT_W), tok(DN_W),
                  _mod_spec(mod, l, 2, per_token), _mod_spec(mod, l, 3, per_token),
                  _mod_spec(mod, l, 4, per_token), _mod_spec(mod, l, 5, per_token),
                  _layer_spec(n2, l),
                  _layer_spec(wo, l, pipeline_mode=single), _layer_spec(wg, l, pipeline_mode=single),
                  _layer_spec(wu, l, pipeline_mode=single), _layer_spec(wd, l, pipeline_mode=single)],
        out_specs=tok(D_MODEL),
        out_shape=jax.ShapeDtypeStruct((nb, length, D_MODEL), F32),
        compiler_params=_cparams("arbitrary", "arbitrary"),
        name="out_ffn",
    )(x3, yp, ya, yd, mod, mod, mod, mod, n2, wo, wg, wu, wd)


def kernel(x_prompt, x_sample, cache_k, cache_v, state_pool, state_conv, state_delta, page_table,
           c_prompt, c_sample, norm1_w, ada_w, ada_b, w_in, pool_w, pool_scale, q_norm_w, k_norm_w,
           conv_w, a_log, dt_bias, dn_norm_w, w_out, norm2_w, w_gate, w_up, w_down):
    depth = w_in.shape[0]
    nbp, seq, _ = x_prompt.shape
    nbs = x_sample.shape[0]
    past_len = page_table.shape[1] * PAGE

    w_ext = jnp.pad(w_in, ((0, 0), (0, 0), (0, IN_EXT - w_in.shape[2]))).astype(BF16)
    wo_b, wg_b, wu_b, wd_b = (w.astype(BF16) for w in (w_out, w_gate, w_up, w_down))
    eye_g = jnp.eye(len(POOL_WINDOWS), dtype=F32)
    pw_bd = (eye_g[None, :, None, :, None] * pool_w[:, :, :, None, :]).reshape(depth, POOL_W, POOL_W).astype(BF16)
    ones_att = jnp.kron(jnp.eye(ATT_HEADS, dtype=F32), jnp.ones((HEAD_DIM, HEAD_DIM), F32)).astype(BF16)
    qn4 = jnp.tile(q_norm_w, (1, ATT_HEADS)).reshape(depth, 1, ATT_W)
    kn4 = jnp.tile(k_norm_w, (1, ATT_HEADS)).reshape(depth, 1, ATT_W)
    dnw4 = jnp.tile(dn_norm_w, (1, DN_HEADS)).reshape(depth, 1, DN_W)
    alog4 = jnp.repeat(a_log, DN_DK, axis=-1).reshape(depth, 1, DN_W)
    dtb4 = jnp.repeat(dt_bias, DN_DK, axis=-1).reshape(depth, 1, DN_W)
    n1 = norm1_w.reshape(depth, 1, D_MODEL)
    n2 = norm2_w.reshape(depth, 1, D_MODEL)
    ps = pool_scale.reshape(depth, 1, POOL_W)
    cache_kt = jnp.transpose(cache_k, (0, 1, 3, 4, 2))
    cache_vt = jnp.transpose(cache_v, (0, 1, 3, 4, 2))

    mod = _modulation(jnp.concatenate([c_prompt, c_sample], axis=0), ada_w, ada_b)
    mod_p = mod[:, :nbp].reshape(depth, nbp, N_MOD, 1, D_MODEL)
    mod_s = mod[:, nbp:].reshape(depth, nbs, N_MOD, D_MODEL).transpose(0, 2, 1, 3)

    xp = x_prompt
    xs = x_sample.reshape(1, nbs, D_MODEL)
    zero_pool = jnp.zeros((nbp, POOL_BUF, POOL_W), F32)
    zero_pool_s = jnp.zeros((nbs, POOL_BUF, POOL_W), F32)
    zero_conv = jnp.zeros((nbp, CONV_W - 1, 3 * DN_W), F32)
    zero_state = jnp.zeros((nbp, DN_HEADS, DN_DK, DN_DK), F32)
    pad_rows = lambda a: jnp.pad(a.reshape(nbs, 1, a.shape[-1]), ((0, 0), (0, SAMPLE_CHUNK - 1), (0, 0)))

    outs = {k: [] for k in ("ks", "vs", "pp", "ps", "cp", "cs", "sp", "ss")}
    kv_all = None
    for l in range(depth):
        u, q, k_all, v_all, d, z, gates = _in_proj(xp, mod_p, l, False, n1, w_ext, qn4, kn4, ones_att,
                                                   2 * TM, kv_depth=depth, kv_prev=kv_all)
        kv_all = (k_all, v_all)
        y_pool, pool_new = _pool(u, zero_pool, pw_bd, ps, 0, l)
        y_att = _attn_prompt(q, k_all, v_all, l)
        y_dn, conv_new, s_new = _gdn(d, z, gates, zero_conv, zero_state, conv_w, alog4, dtb4, dnw4,
                                     l, seq, GDN_BT, CHUNK)
        xp = _out_ffn(xp, y_pool, y_att, y_dn, mod_p, l, False, n2, wo_b, wg_b, wu_b, wd_b, TM)
        outs["pp"].append(pool_new)
        outs["cp"].append(conv_new)
        outs["sp"].append(s_new)

        us, qs, ks, vs, ds, zs, gs = _in_proj(xs, mod_s, l, True, n1, w_ext, qn4, kn4, ones_att, nbs)
        ext = jnp.concatenate([state_pool[l], us.reshape(nbs, 1, POOL_W)], axis=1)
        y_pool16, pool_new = _pool(ext, zero_pool_s, pw_bd, ps, past_len - POOL_BUF, l, bs=8)
        y_pool = y_pool16[:, POOL_BUF:].reshape(1, nbs, POOL_W)
        o_att = _attn_sample(qs.reshape(nbs, ATT_HEADS, HEAD_DIM).astype(F32),
                             ks.reshape(nbs, ATT_HEADS, HEAD_DIM), vs.reshape(nbs, ATT_HEADS, HEAD_DIM),
                             cache_kt, cache_vt, page_table, l)
        y_att = o_att.reshape(1, nbs, ATT_W)
        y_dnc, conv_new, s_new = _gdn(pad_rows(ds), pad_rows(zs), pad_rows(gs),
                                      state_conv, state_delta, conv_w, alog4, dtb4, dnw4,
                                      l, 1, GDN_BT, SAMPLE_CHUNK, stacked_state=True)
        y_dn = y_dnc[:, 0].reshape(1, nbs, DN_W)
        xs = _out_ffn(xs, y_pool, y_att, y_dn, mod_s, l, True, n2, wo_b, wg_b, wu_b, wd_b, nbs)
        outs["ks"].append(ks.reshape(nbs, 1, ATT_HEADS, HEAD_DIM))
        outs["vs"].append(vs.reshape(nbs, 1, ATT_HEADS, HEAD_DIM))
        outs["ps"].append(pool_new)
        outs["cs"].append(conv_new)
        outs["ss"].append(s_new)

    st = lambda name: jnp.stack(outs[name])
    k_prompt, v_prompt = (a.reshape(depth, nbp, seq, ATT_HEADS, HEAD_DIM) for a in kv_all)
    return (xp, xs.reshape(nbs, 1, D_MODEL), k_prompt, v_prompt, st("ks"), st("vs"),
            st("pp"), st("ps"), st("cp"), st("cs"), st("sp"), st("ss"))
```

```python
import functools

import numpy as np
import jax
import jax.numpy as jnp
from jax import lax
from jax.experimental import pallas as pl
from jax.experimental.pallas import tpu as pltpu

F32 = jnp.float32
BF16 = jnp.bfloat16

D_MODEL = 1024
PAGE = 128
POOL_WINDOWS = (2, 4, 8, 16)
POOL_W = 256
POOL_BUF = 15
HEAD_DIM = 64
ATT_HEADS = 8
ATT_W = 512
BLK = 256
TOPK = 3
DN_HEADS = 4
DN_DK = 64
DN_W = 256
CHUNK = 64
SAMPLE_CHUNK = 16
CONV_W = 4
D_FF = 2816
N_MOD = 6
EPS = 1e-6
NEG = -1e30
LOG2E = 1.4426950408889634

C_U = (0, 256)
C_Q = (256, 768)
C_K = (768, 1280)
C_V = (1280, 1792)
C_D = (1792, 2560)
C_Z = (2560, 2816)
C_G = (2816, 2944)
IN_EXT = 2944
GATE_W = 128

TM = 512
TM_IN = 1024
MOD_TN = 1536
GDN_BT = 8
GDN_BT_SAMPLE = 8
POOL_BS_SAMPLE = 8
SA_PAGES_PER_STEP = 32
VMEM_LIMIT = 56 * 1024 * 1024


def _cparams(*sem):
    return pltpu.CompilerParams(dimension_semantics=sem, vmem_limit_bytes=VMEM_LIMIT)


def _sigmoid(x):
    return 1.0 / (1.0 + jnp.exp(-x))


def _bdot(a, b):
    return jnp.dot(a.astype(BF16), b.astype(BF16), preferred_element_type=F32)


def _bdot_t(a, b):
    return lax.dot_general(a.astype(BF16), b.astype(BF16), (((1,), (1,)), ((), ())),
                           preferred_element_type=F32)


def _split3(a):
    hi = a.astype(BF16)
    r1 = a - hi.astype(F32)
    mid = r1.astype(BF16)
    lo = (r1 - mid.astype(F32)).astype(BF16)
    return hi, mid, lo


def _dot3_rhs_exact(a, b01):
    return sum(jnp.dot(p, b01, preferred_element_type=F32) for p in _split3(a))


def _dot3_lhs_exact(a01, b):
    return sum(jnp.dot(a01, p, preferred_element_type=F32) for p in _split3(b))


def _dot3_t_rhs_exact(a, b01):
    return sum(lax.dot_general(p, b01, (((1,), (1,)), ((), ())), preferred_element_type=F32)
               for p in _split3(a))


def _dot3_t_lhs_exact(a01, b):
    return sum(lax.dot_general(a01, p, (((1,), (1,)), ((), ())), preferred_element_type=F32)
               for p in _split3(b))


def _mod_kernel(c_ref, w_ref, b_ref, o_ref):
    c = c_ref[...]
    a = (c * _sigmoid(c)).astype(BF16)
    o_ref[...] = jnp.dot(a, w_ref[...].astype(BF16), preferred_element_type=F32) + b_ref[...]


def _modulation(c_all, ada_w, ada_b):
    depth, _, ncol = ada_w.shape
    nseq = c_all.shape[0]
    tn = MOD_TN
    return pl.pallas_call(
        _mod_kernel,
        grid=(depth, ncol // tn),
        in_specs=[pl.BlockSpec((nseq, D_MODEL), lambda l, j: (0, 0)),
                  pl.BlockSpec((None, D_MODEL, tn), lambda l, j: (l, 0, j)),
                  pl.BlockSpec((None, 1, tn), lambda l, j: (l, 0, j))],
        out_specs=pl.BlockSpec((None, nseq, tn), lambda l, j: (l, 0, j)),
        out_shape=jax.ShapeDtypeStruct((depth, nseq, ncol), F32),
        compiler_params=_cparams("arbitrary", "arbitrary"),
        name="modulation",
    )(c_all, ada_w, ada_b.reshape(depth, 1, ncol))


def _layer_spec(stacked, l, **kw):
    zeros = (0,) * (stacked.ndim - 1)
    return pl.BlockSpec((None,) + stacked.shape[1:], lambda *g: (l,) + zeros, **kw)


def _mod_spec(mod, l, k, per_token):
    if per_token:
        return pl.BlockSpec((None, None, mod.shape[2], D_MODEL), lambda *g: (l, k, 0, 0))
    return pl.BlockSpec((None, None, None, 1, D_MODEL), lambda *g: (l, g[0], k, 0, 0))


def _in_kernel(x_ref, sh_ref, sc_ref, n1_ref, w_ref, qn_ref, kn_ref, ones_ref, *rest):
    u_ref, q_ref, k_ref, v_ref, d_ref, z_ref, g_ref = rest[-7:]
    x = x_ref[...]
    ms = jnp.mean(x * x, axis=-1, keepdims=True)
    h = (x * lax.rsqrt(ms + EPS) * n1_ref[...]) * (1.0 + sc_ref[...]) + sh_ref[...]
    hb = h.astype(BF16)

    def proj(c):
        return jnp.dot(hb, w_ref[:, c[0]:c[1]], preferred_element_type=F32)

    def head_norm(t, w4):
        ss = jnp.dot((t * t).astype(BF16), ones_ref[...], preferred_element_type=F32)
        return t * lax.rsqrt(ss * (1.0 / HEAD_DIM) + EPS) * w4

    u_ref[...] = proj(C_U)
    q_ref[...] = (head_norm(proj(C_Q), qn_ref[...]) * (HEAD_DIM ** -0.5 * LOG2E)).astype(BF16)
    k_ref[...] = head_norm(proj(C_K), kn_ref[...])
    v_ref[...] = proj(C_V)
    d_ref[...] = proj(C_D)
    z_ref[...] = proj(C_Z)
    g_ref[...] = proj(C_G)


def _in_proj(x3, mod, l, per_token, n1, w_ext, qn4, kn4, ones_att, tm, kv_depth=None, kv_prev=None):
    nb, length, _ = x3.shape
    widths = (POOL_W, ATT_W, ATT_W, ATT_W, 3 * DN_W, DN_W, GATE_W)
    dtypes = (F32, BF16, F32, F32, F32, F32, F32)
    const = lambda b, t: (0, 0)
    tok = lambda w: pl.BlockSpec((None, tm, w), lambda b, t: (b, t, 0))
    out_specs = [tok(w) for w in widths]
    out_shape = [jax.ShapeDtypeStruct((nb, length, w), dt) for w, dt in zip(widths, dtypes)]
    operands = [x3, mod, mod, n1, w_ext, qn4, kn4, ones_att]
    in_specs = [tok(D_MODEL), _mod_spec(mod, l, 0, per_token), _mod_spec(mod, l, 1, per_token),
                _layer_spec(n1, l), _layer_spec(w_ext, l), _layer_spec(qn4, l), _layer_spec(kn4, l),
                pl.BlockSpec((ATT_W, ATT_W), const)]
    aliases = {}
    if kv_depth is not None:
        for o in (2, 3):
            out_specs[o] = pl.BlockSpec((None, None, tm, ATT_W), lambda b, t: (l, b, t, 0))
            out_shape[o] = jax.ShapeDtypeStruct((kv_depth, nb, length, ATT_W), F32)
        if kv_prev is not None:
            aliases = {len(operands): 2, len(operands) + 1: 3}
            operands += list(kv_prev)
            in_specs += [pl.BlockSpec(memory_space=pl.ANY)] * 2
    return pl.pallas_call(
        _in_kernel,
        grid=(nb, length // tm),
        in_specs=in_specs,
        out_specs=out_specs,
        out_shape=out_shape,
        input_output_aliases=aliases,
        compiler_params=_cparams("arbitrary", "arbitrary"),
        name="in_proj",
    )(*operands)


POOL_ROW0 = 32


def _pool_kernel(u_ref, buf_ref, pw_ref, ps_ref, y_ref, new_ref, e_scr, s2_scr, s4_scr, s8_scr, *,
                 length, pos0, bs):
    zeros16 = jnp.zeros((16, POOL_W), F32)
    lane = lax.broadcasted_iota(jnp.int32, (1, POOL_W), 1)
    grp = jnp.right_shift(lane, 6)
    wl = jnp.where(grp == 0, 2, jnp.where(grp == 1, 4, jnp.where(grp == 2, 8, 16)))
    ch = min(length, 256)
    row = lax.broadcasted_iota(jnp.int32, (ch, 1), 0)
    spans = [(16, 16)] + [(POOL_ROW0 + c * ch, ch) for c in range(length // ch)]
    for s in range(bs):
        e_scr[s, pl.ds(8, 16), :] = zeros16
        s2_scr[s, pl.ds(8, 8), :] = zeros16[:8]
        s4_scr[s, pl.ds(8, 8), :] = zeros16[:8]
        e_scr[s, pl.ds(POOL_ROW0 - POOL_BUF, POOL_BUF), :] = buf_ref[s]
        e_scr[s, pl.ds(POOL_ROW0, length), :] = u_ref[s]
        for src, dst, shift in ((e_scr, s2_scr, 1), (s2_scr, s4_scr, 2), (s4_scr, s8_scr, 4)):
            for base, n in spans:
                dst[s, pl.ds(base, n), :] = src[s, pl.ds(base, n), :] + src[s, pl.ds(base - shift, n), :]
        for c in range(length // ch):
            base = POOL_ROW0 + c * ch
            cur = e_scr[s, pl.ds(base, ch), :]
            s16 = s8_scr[s, pl.ds(base, ch), :] + s8_scr[s, pl.ds(base - 8, ch), :]
            wsum = jnp.where(grp == 0, s2_scr[s, pl.ds(base, ch), :],
                             jnp.where(grp == 1, s4_scr[s, pl.ds(base, ch), :],
                                       jnp.where(grp == 2, s8_scr[s, pl.ds(base, ch), :], s16)))
            cnt = jnp.minimum(wl, row + (pos0 + c * ch + 1)).astype(F32)
            d = wsum / cnt - cur
            y = jnp.dot(d.astype(BF16), pw_ref[...], preferred_element_type=F32) * ps_ref[...]
            y_ref[s, pl.ds(c * ch, ch), :] = y.astype(y_ref.dtype)
        new_ref[s] = e_scr[s, pl.ds(POOL_ROW0 + length - POOL_BUF, POOL_BUF), :]


def _pool(u3, buf3, pw_bd, ps, pos0, l, bs=1):
    nb, length, _ = u3.shape
    return pl.pallas_call(
        functools.partial(_pool_kernel, length=length, pos0=pos0, bs=bs),
        grid=(nb // bs,),
        in_specs=[pl.BlockSpec((bs, length, POOL_W), lambda b: (b, 0, 0)),
                  pl.BlockSpec((bs, POOL_BUF, POOL_W), lambda b: (b, 0, 0)),
                  _layer_spec(pw_bd, l), _layer_spec(ps, l)],
        out_specs=[pl.BlockSpec((bs, length, POOL_W), lambda b: (b, 0, 0)),
                   pl.BlockSpec((bs, POOL_BUF, POOL_W), lambda b: (b, 0, 0))],
        out_shape=[jax.ShapeDtypeStruct((nb, length, POOL_W), BF16),
                   jax.ShapeDtypeStruct((nb, POOL_BUF, POOL_W), F32)],
        scratch_shapes=[pltpu.VMEM((bs, POOL_ROW0 + length, POOL_W), F32)] * 4,
        compiler_params=_cparams("arbitrary"),
        name="pool_mixer",
    )(u3, buf3, pw_bd, ps)


def _attn_kernel(q_ref, k_ref, v_ref, o_ref, km_scr, vt_scr, ot_scr, s_scr, p_scr, *, length):
    nb = length // BLK
    lane = lax.broadcasted_iota(jnp.int32, (1, 128), 1)
    k2 = k_ref[...]
    kmean = jnp.sum(k2.reshape(nb, BLK, 128), axis=1) * (1.0 / BLK)
    vt_scr[...] = v_ref[...].T.astype(BF16)
    krow = lax.broadcasted_iota(jnp.int32, (BLK, BLK), 0)
    qcol = lax.broadcasted_iota(jnp.int32, (BLK, BLK), 1)
    causal = krow <= qcol
    blkrow = lax.broadcasted_iota(jnp.int32, (nb, BLK), 0)
    pad_rows = jnp.zeros((16 - nb, 128), F32) if nb < 16 else None
    kmh = []
    for hh in range(2):
        hm = jnp.right_shift(lane, 6) == hh
        km_scr[hh] = jnp.where(hm, k2, 0.0).astype(BF16)
        t = jnp.where(hm, kmean, 0.0)
        kmh.append(t if pad_rows is None else jnp.concatenate([t, pad_rows], axis=0))

    def fold8(t):
        return t.reshape(BLK // 8, 8, BLK)

    def scores(hh, i, slot):
        qi = q_ref[pl.ds(i * BLK, BLK), :]
        bias = None
        if i > 0:
            gate = _bdot_t(kmh[hh], qi)[:nb]
            past = blkrow < i
            gate = jnp.where(past, gate, -jnp.inf)
            cnt = jnp.zeros((nb, BLK), jnp.int32)
            for jp in range(i):
                gj = gate[jp:jp + 1, :]
                cnt = cnt + jnp.where(gj > gate, 1,
                                      jnp.where(gj == gate, (jp < blkrow).astype(jnp.int32), 0))
            sel = jnp.where(past, cnt, TOPK) < TOPK
            bias = jnp.where(sel, 0.0, NEG)
        m8 = None
        for j in range(i + 1):
            s = _bdot_t(km_scr[hh, j * BLK:(j + 1) * BLK, :], qi)
            s = jnp.where(causal, s, NEG) if j == i else s + bias[j:j + 1, :]
            s_scr[slot, j] = s
            t = jnp.max(fold8(s), axis=0)
            m8 = t if m8 is None else jnp.maximum(m8, t)
        return jnp.max(m8, axis=0, keepdims=True)

    def weighted(hh, i, slot, m):
        l8 = None
        for j in range(i + 1):
            p = jnp.exp2(s_scr[slot, j] - m)
            t = jnp.sum(fold8(p), axis=0)
            l8 = t if l8 is None else l8 + t
            p_scr[slot, j * BLK:(j + 1) * BLK, :] = p.astype(BF16)
        acc = jnp.dot(vt_scr[hh * 64:(hh + 1) * 64, 0:(i + 1) * BLK], p_scr[slot, 0:(i + 1) * BLK, :],
                      preferred_element_type=F32)
        l = jnp.sum(l8, axis=0, keepdims=True)
        ot_scr[hh * 64:(hh + 1) * 64, i * BLK:(i + 1) * BLK] = acc / l

    items = [(hh, i) for hh in range(2) for i in range(nb)]
    pending = None
    for n, (hh, i) in enumerate(items):
        m = scores(hh, i, n % 2)
        if pending is not None:
            weighted(*pending)
        pending = (hh, i, n % 2, m)
    weighted(*pending)
    o_ref[...] = ot_scr[...].T.astype(o_ref.dtype)


def _attn_prompt(q3, k_all, v_all, l):
    nb, length, _ = q3.shape
    nblk = length // BLK
    spec = pl.BlockSpec((None, length, 128), lambda b, h: (b, 0, h))
    kv_spec = pl.BlockSpec((None, None, length, 128), lambda b, h: (l, b, 0, h))
    return pl.pallas_call(
        functools.partial(_attn_kernel, length=length),
        grid=(nb, ATT_W // 128),
        in_specs=[spec, kv_spec, kv_spec],
        out_specs=spec,
        out_shape=jax.ShapeDtypeStruct((nb, length, ATT_W), BF16),
        scratch_shapes=[pltpu.VMEM((2, length, 128), BF16),
                        pltpu.VMEM((128, length), BF16),
                        pltpu.VMEM((128, length), F32),
                        pltpu.VMEM((2, nblk, BLK, BLK), F32),
                        pltpu.VMEM((2, length, BLK), BF16)],
        compiler_params=_cparams("arbitrary", "arbitrary"),
        name="moba_prompt",
    )(q3, k_all, v_all)


def _sa_partials(kp, vp, qbc_ref, s_scr, g_scr, m_scr, l_scr, o_scr, first_blk):
    full = (ATT_HEADS, PAGE)
    for pg in range(len(kp)):
        for h in range(ATT_HEADS):
            s_scr[pg, pl.ds(h, 1), :] = jnp.sum(kp[pg][h] * qbc_ref[h], axis=0, keepdims=True)
    for blk in range(len(kp) // 2):
        s0 = s_scr[2 * blk]
        s1 = s_scr[2 * blk + 1]
        gsum = jnp.sum(s0, axis=-1, keepdims=True) + jnp.sum(s1, axis=-1, keepdims=True)
        mb = jnp.maximum(jnp.max(s0, axis=-1, keepdims=True), jnp.max(s1, axis=-1, keepdims=True))
        p0 = jnp.exp2(s0 - mb)
        p1 = jnp.exp2(s1 - mb)
        lb = jnp.sum(p0, axis=-1, keepdims=True) + jnp.sum(p1, axis=-1, keepdims=True)
        s_scr[2 * blk] = p0
        s_scr[2 * blk + 1] = p1
        idx = first_blk + blk
        g_scr[idx] = jnp.broadcast_to(gsum, full)
        m_scr[idx] = jnp.broadcast_to(mb, full)
        l_scr[idx] = jnp.broadcast_to(lb, full)
        for h in range(ATT_HEADS):
            o_scr[idx, h] = (vp[2 * blk][h] * s_scr[2 * blk, pl.ds(h, 1), :]
                             + vp[2 * blk + 1][h] * s_scr[2 * blk + 1, pl.ds(h, 1), :])


def _sa_merge(q_ref, kn_ref, vn_ref, o_ref, g_scr, m_scr, l_scr, o_scr, w_scr, nblk):
    full = (ATT_HEADS, PAGE)
    gates = g_scr[pl.ds(0, nblk)]
    if nblk > TOPK:
        lane = lax.broadcasted_iota(jnp.int32, full, 1)
        gates_c = jnp.zeros(full, F32)
        for b in range(nblk):
            gates_c = jnp.where(lane == b, gates[b], gates_c)
        cnt = jnp.zeros(full, jnp.int32)
        for b in range(nblk):
            cnt = cnt + jnp.where(gates[b] > gates_c, 1,
                                  jnp.where(gates[b] == gates_c, (b < lane).astype(jnp.int32), 0))
        last = jnp.where(lane < nblk, cnt, -1) == TOPK - 1
        g3 = jnp.max(jnp.where(last, gates_c, -jnp.inf), axis=-1, keepdims=True)
        b3 = jnp.max(jnp.where(last, lane.astype(F32), -1.0), axis=-1, keepdims=True)
        bidx = lax.broadcasted_iota(jnp.int32, gates.shape, 0).astype(F32)
        sel = (gates > g3[None]) | ((gates == g3[None]) & (bidx <= b3[None]))
    else:
        sel = jnp.full(gates.shape, True)
    sself = jnp.broadcast_to(jnp.sum(q_ref[...] * kn_ref[...], axis=-1, keepdims=True), full)
    mb = m_scr[pl.ds(0, nblk)]
    mtot = jnp.maximum(jnp.max(jnp.where(sel, mb, -jnp.inf), axis=0), sself)
    w = jnp.where(sel, jnp.exp2(jnp.where(sel, mb - mtot[None], 0.0)), 0.0)
    wself = jnp.exp2(sself - mtot)
    denom = jnp.sum(w * l_scr[pl.ds(0, nblk)], axis=0) + wself
    w_scr[...] = w
    ones8 = jnp.ones((8, PAGE), BF16)
    sub = lax.broadcasted_iota(jnp.int32, (ATT_HEADS, HEAD_DIM), 0)
    o_acc = jnp.zeros((ATT_HEADS, HEAD_DIM), F32)
    for h in range(ATT_HEADS):
        tot = w_scr[0, pl.ds(h, 1), :] * o_scr[0, h]
        for b in range(1, nblk):
            tot = tot + w_scr[b, pl.ds(h, 1), :] * o_scr[b, h]
        r = _dot3_t_lhs_exact(ones8, tot)
        o_acc = jnp.where(sub == h, r, o_acc)
    o_ref[...] = (o_acc + wself[:, :HEAD_DIM] * vn_ref[...]) / denom[:, :HEAD_DIM]


def _sa_kernel(pt_ref, qbc_ref, q_ref, kn_ref, vn_ref, *refs, gpages, n_steps):
    kp = refs[:gpages]
    vp = refs[gpages:2 * gpages]
    o_ref = refs[2 * gpages]
    g_scr, m_scr, l_scr, o_scr, s_scr, w_scr = refs[2 * gpages + 1:]
    step = pl.program_id(1)
    _sa_partials(kp, vp, qbc_ref, s_scr, g_scr, m_scr, l_scr, o_scr, step * (gpages // 2))

    @pl.when(step == n_steps - 1)
    def _merge():
        _sa_merge(q_ref, kn_ref, vn_ref, o_ref, g_scr, m_scr, l_scr, o_scr, w_scr, n_steps * (gpages // 2))


def _attn_sample(q3, kn3, vn3, cache_kt, cache_vt, page_table, l):
    nb = q3.shape[0]
    n_pages = page_table.shape[1]
    gpages = SA_PAGES_PER_STEP
    n_steps = n_pages // gpages
    nblk = n_pages // 2
    qbc = jnp.broadcast_to(q3[..., None], (nb, ATT_HEADS, HEAD_DIM, PAGE))
    tok = lambda: pl.BlockSpec((None, ATT_HEADS, HEAD_DIM), lambda b, s, pt: (b, 0, 0))

    def page_spec(i):
        return pl.BlockSpec((None, None, ATT_HEADS, HEAD_DIM, PAGE),
                            lambda b, s, pt: (l, pt[b, s * gpages + i], 0, 0, 0))

    grid_spec = pltpu.PrefetchScalarGridSpec(
        num_scalar_prefetch=1,
        grid=(nb, n_steps),
        in_specs=[pl.BlockSpec((None, ATT_HEADS, HEAD_DIM, PAGE), lambda b, s, pt: (b, 0, 0, 0)),
                  tok(), tok(), tok()]
                 + [page_spec(i) for i in range(gpages)] + [page_spec(i) for i in range(gpages)],
        out_specs=tok(),
        scratch_shapes=[pltpu.VMEM((nblk, ATT_HEADS, PAGE), F32),
                        pltpu.VMEM((nblk, ATT_HEADS, PAGE), F32),
                        pltpu.VMEM((nblk, ATT_HEADS, PAGE), F32),
                        pltpu.VMEM((nblk, ATT_HEADS, HEAD_DIM, PAGE), F32),
                        pltpu.VMEM((gpages, ATT_HEADS, PAGE), F32),
                        pltpu.VMEM((nblk, ATT_HEADS, PAGE), F32)],
    )
    return pl.pallas_call(
        functools.partial(_sa_kernel, gpages=gpages, n_steps=n_steps),
        grid_spec=grid_spec,
        out_shape=jax.ShapeDtypeStruct((nb, ATT_HEADS, HEAD_DIM), F32),
        compiler_params=_cparams("arbitrary", "arbitrary"),
        name="moba_sample",
    )(page_table, qbc, q3, kn3, vn3, *([cache_kt] * gpages), *([cache_vt] * gpages))


def _gdn_constants(chunk):
    C = chunk
    lane = np.arange(DN_W)
    lane_j = lane % 64
    row = np.arange(C)[:, None]
    tri = np.stack([lane_j[None] <= row, lane_j[None] < row, lane_j[None] == row, row <= lane_j[None]])
    l_incl = np.arange(C)[None, :] <= np.arange(C)[:, None]
    blk = np.arange(DN_W) // 64
    bd = blk[:, None] == blk[None, :]
    place = np.stack([np.arange(DN_W)[None, :] == (np.arange(DN_DK)[:, None] + 64 * h) for h in range(DN_HEADS)])
    half = np.stack([np.broadcast_to((np.arange(128) // 64) == s, (C, 128)) for s in range(2)])
    expand = np.arange(GATE_W)[:, None] == (np.arange(2 * DN_W)[None, :] // 64)
    return (jnp.asarray(tri, F32), jnp.asarray(l_incl, BF16), jnp.asarray(bd, BF16), jnp.asarray(bd, F32),
            jnp.asarray(place, BF16), jnp.asarray(half, F32), jnp.asarray(expand, BF16))


def _gdn_kernel(d_ref, z_ref, g_ref, c0_ref, s0_ref, cw_ref, alog_ref, dtb_ref, dnw_ref,
                tri_ref, lincl_ref, onesbd_ref, bdmask_ref, place_ref, half_ref, expand_ref,
                y_ref, cout_ref, sout_ref, ext_scr, s_scr, *, bt, n_chunks, l_valid, chunk):
    n = pl.program_id(1)
    C = chunk
    incl4 = tri_ref[0] > 0.5
    strict4 = tri_ref[1] > 0.5
    eye4 = tri_ref[2]
    ut4 = tri_ref[3]
    l_incl = lincl_ref[...]
    ones_bd = onesbd_ref[...]
    zero_half = jnp.zeros((C, 128), F32)
    left = half_ref[0] > 0.5
    right = half_ref[1] > 0.5

    def stack_mask(a4):
        lo = a4[:, :128]
        hi = a4[:, 128:]
        pieces = [jnp.concatenate([jnp.where(left, lo, 0.0), zero_half], axis=1),
                  jnp.concatenate([jnp.where(right, lo, 0.0), zero_half], axis=1),
                  jnp.concatenate([zero_half, jnp.where(left, hi, 0.0)], axis=1),
                  jnp.concatenate([zero_half, jnp.where(right, hi, 0.0)], axis=1)]
        if C < DN_DK:
            fill = jnp.zeros((DN_DK - C, DN_W), F32)
            pieces = [x for p in pieces for x in (p, fill)]
        return jnp.concatenate(pieces, axis=0).astype(BF16)

    @pl.when(n == 0)
    def _init():
        for bi in range(bt):
            ext_scr[bi, pl.ds(5, CONV_W - 1), :] = c0_ref[bi]
            s_scr[bi] = jnp.concatenate(
                [_dot3_rhs_exact(s0_ref[bi, h], place_ref[h]) for h in range(DN_HEADS)], axis=0)

    last_valid = l_valid - (n_chunks - 1) * C
    masked = last_valid != C
    if masked:
        row = lax.broadcasted_iota(jnp.int32, (C, 1), 0)
        valid = (n * C + row) < l_valid

    seqs = range(bt)
    each = lambda f, *cols: [f(*xs) for xs in zip(*cols)]
    cw = cw_ref[...]
    act = []
    for bi in seqs:
        ext_scr[bi, pl.ds(8, C), :] = d_ref[bi]
        yc = (cw[0:1] * ext_scr[bi, pl.ds(5, C), :] + cw[1:2] * ext_scr[bi, pl.ds(6, C), :]
              + cw[2:3] * ext_scr[bi, pl.ds(7, C), :] + cw[3:4] * ext_scr[bi, pl.ds(8, C), :])
        act.append(yc * _sigmoid(yc))
        cout_ref[bi] = ext_scr[bi, pl.ds(last_valid + 5, CONV_W - 1), :]
        ext_scr[bi, pl.ds(5, CONV_W - 1), :] = ext_scr[bi, pl.ds(8 + C - (CONV_W - 1), CONV_W - 1), :]

    q_raw = [a[:, 0:DN_W] for a in act]
    k_raw = [a[:, DN_W:2 * DN_W] for a in act]
    v4 = [a[:, 2 * DN_W:3 * DN_W] for a in act]
    ssq = each(lambda x: _bdot(x * x, ones_bd), q_raw)
    ssk = each(lambda x: _bdot(x * x, ones_bd), k_raw)
    q4 = each(lambda x, s: x * lax.rsqrt(s + EPS) * (DN_DK ** -0.5), q_raw, ssq)
    k4 = each(lambda x, s: x * lax.rsqrt(s + EPS), k_raw, ssk)
    raw = [_dot3_rhs_exact(g_ref[bi], expand_ref[...]) for bi in seqs]
    beta4 = [_sigmoid(r[:, :DN_W]) for r in raw]

    def log_decay(r):
        xg = r[:, DN_W:] + dtb_ref[...]
        return -jnp.exp(alog_ref[...]) * (jnp.maximum(xg, 0.0) + jnp.log1p(jnp.exp(-jnp.abs(xg))))

    g4 = [log_decay(r) for r in raw]
    if masked:
        zero_pad = lambda x: jnp.where(valid, x, 0.0)
        q4, k4, v4, beta4, g4 = (each(zero_pad, c) for c in (q4, k4, v4, beta4, g4))

    gi = each(lambda g: _dot3_lhs_exact(l_incl, g), g4)
    gj = each(lambda g: jnp.sum(g * ut4, axis=0, keepdims=True), g4)
    decay4 = each(lambda a, b: jnp.where(incl4, jnp.exp(jnp.where(incl4, a - b, 0.0)), 0.0), gi, gj)
    eg4 = each(jnp.exp, gi)
    glast = [g[C - 1:C, :] for g in gi]
    kfac = each(lambda a, b: jnp.exp(a - b), glast, gi)
    gtot = each(jnp.exp, glast)

    kb4 = each(lambda a, b: a * b, k4, beta4)
    mk = each(lambda kb, q, k: _bdot_t(jnp.concatenate([kb, q], axis=0), stack_mask(k)), kb4, q4, k4)
    attn4 = each(lambda r, dc: r[C:] * dc, mk, decay4)

    p = each(lambda r, dc: -jnp.where(strict4, r[:C] * dc, 0.0), mk, decay4)
    t = each(lambda x: eye4 + x, p)
    p = each(lambda x: _bdot(x, stack_mask(x)), p)
    for _ in range(C.bit_length() - 3):
        r = each(lambda a, b: _bdot(jnp.concatenate([a, b], axis=0), stack_mask(b)), t, p)
        t = each(lambda a, b: a + b[:C], t, r)
        p = [x[C:] for x in r]
    t = each(lambda a, b: a + _bdot(a, stack_mask(b)), t, p)

    u4 = each(lambda a, v, b: _bdot(a, stack_mask(v * b)), t, v4, beta4)
    kc4 = each(lambda a, kb, e: _bdot(a, stack_mask(kb * e)), t, kb4, eg4)

    sbd = [s_scr[bi] for bi in seqs]
    r = each(lambda kc, q, e, s: _bdot(jnp.concatenate([kc, q * e], axis=0), s), kc4, q4, eg4, sbd)
    vnew = each(lambda u, x: u - x[:C], u4, r)
    o4 = each(lambda x, a, v: x[C:] + _bdot(a, stack_mask(v)), r, attn4, vnew)
    upd = each(lambda k, f, v: lax.dot_general((k * f).astype(BF16), v.astype(BF16), (((0,), (0,)), ((), ())),
                                               preferred_element_type=F32), k4, kfac, vnew)
    sso = each(lambda o: _bdot(o * o, ones_bd), o4)
    for bi in seqs:
        s_scr[bi] = sbd[bi] * gtot[bi] + upd[bi] * bdmask_ref[...]
        zz = z_ref[bi]
        y_ref[bi] = (o4[bi] * lax.rsqrt(sso[bi] * (1.0 / DN_DK) + EPS) * dnw_ref[...]
                     * (zz * _sigmoid(zz))).astype(y_ref.dtype)

    @pl.when(n == n_chunks - 1)
    def _fin():
        for bi in range(bt):
            for h in range(DN_HEADS):
                sout_ref[bi, h] = _dot3_t_rhs_exact(s_scr[bi, h * 64:(h + 1) * 64, :], place_ref[h])


def _gdn(d3, z3, g3, conv0, s0, cw, alog4, dtb4, dnw4, l, l_valid, bt, chunk, stacked_state=False):
    nb, length, _ = d3.shape
    n_chunks = length // chunk
    tok = lambda w: pl.BlockSpec((bt, chunk, w), lambda i, n: (i, n, 0))
    const = lambda i, n: (0, 0)
    const3 = lambda i, n: (0, 0, 0)
    conv_blk = (bt, CONV_W - 1, 3 * DN_W)
    state_blk = (bt, DN_HEADS, DN_DK, DN_DK)
    if stacked_state:
        conv_spec = pl.BlockSpec((None,) + conv_blk, lambda i, n: (l, i, 0, 0))
        state_spec = pl.BlockSpec((None,) + state_blk, lambda i, n: (l, i, 0, 0, 0))
    else:
        conv_spec = pl.BlockSpec(conv_blk, lambda i, n: (i, 0, 0))
        state_spec = pl.BlockSpec(state_blk, lambda i, n: (i, 0, 0, 0))
    return pl.pallas_call(
        functools.partial(_gdn_kernel, bt=bt, n_chunks=n_chunks, l_valid=l_valid, chunk=chunk),
        grid=(nb // bt, n_chunks),
        in_specs=[tok(3 * DN_W), tok(DN_W), tok(GATE_W), conv_spec, state_spec,
                  _layer_spec(cw, l), _layer_spec(alog4, l), _layer_spec(dtb4, l), _layer_spec(dnw4, l),
                  pl.BlockSpec((4, chunk, DN_W), const3),
                  pl.BlockSpec((chunk, chunk), const),
                  pl.BlockSpec((DN_W, DN_W), const), pl.BlockSpec((DN_W, DN_W), const),
                  pl.BlockSpec((DN_HEADS, DN_DK, DN_W), const3),
                  pl.BlockSpec((2, chunk, 128), const3),
                  pl.BlockSpec((GATE_W, 2 * DN_W), const)],
        out_specs=[tok(DN_W),
                   pl.BlockSpec((bt, CONV_W - 1, 3 * DN_W), lambda i, n: (i, 0, 0)),
                   pl.BlockSpec((bt, DN_HEADS, DN_DK, DN_DK), lambda i, n: (i, 0, 0, 0))],
        out_shape=[jax.ShapeDtypeStruct((nb, length, DN_W), BF16),
                   jax.ShapeDtypeStruct((nb, CONV_W - 1, 3 * DN_W), F32),
                   jax.ShapeDtypeStruct((nb, DN_HEADS, DN_DK, DN_DK), F32)],
        scratch_shapes=[pltpu.VMEM((bt, 8 + chunk, 3 * DN_W), F32),
                        pltpu.VMEM((bt, DN_W, DN_W), F32)],
        compiler_params=_cparams("arbitrary", "arbitrary"),
        name="gated_delta",
    )(d3, z3, g3, conv0, s0, cw, alog4, dtb4, dnw4, *_gdn_constants(chunk))


MXU_N = 256
FF_CHUNKS = ((0, 6 * MXU_N), (6 * MXU_N, D_FF))


def _mix_residual(x_ref, yp_ref, ya_ref, yd_ref, g1_ref, sh2_ref, sc2_ref, n2_ref, wo_ref):
    mix = jnp.concatenate([yp_ref[...].astype(BF16), ya_ref[...].astype(BF16), yd_ref[...].astype(BF16)],
                          axis=1)
    x1 = x_ref[...] + g1_ref[...] * jnp.dot(mix, wo_ref[...], preferred_element_type=F32)
    ms = jnp.mean(x1 * x1, axis=-1, keepdims=True)
    h2 = ((x1 * lax.rsqrt(ms + EPS) * n2_ref[...]) * (1.0 + sc2_ref[...]) + sh2_ref[...]).astype(BF16)
    return x1, h2


def _out_kernel(x_ref, yp_ref, ya_ref, yd_ref, g1_ref, sh2_ref, sc2_ref, g2_ref, n2_ref,
                wo_ref, wg_ref, wu_ref, wd_ref, o_ref):
    x1, h2 = _mix_residual(x_ref, yp_ref, ya_ref, yd_ref, g1_ref, sh2_ref, sc2_ref, n2_ref, wo_ref)
    acc = None
    for c0, c1 in FF_CHUNKS:
        gt = jnp.dot(h2, wg_ref[:, c0:c1], preferred_element_type=F32)
        up = jnp.dot(h2, wu_ref[:, c0:c1], preferred_element_type=F32)
        act = (gt * _sigmoid(gt) * up).astype(BF16)
        part = jnp.dot(act, wd_ref[c0:c1, :], preferred_element_type=F32)
        acc = part if acc is None else acc + part
    o_ref[...] = x1 + g2_ref[...] * acc


def _out_ffn(x3, yp, ya, yd, mod, l, per_token, n2, wo, wg, wu, wd, tm):
    nb, length, _ = x3.shape
    tok = lambda w: pl.BlockSpec((None, tm, w), lambda b, t: (b, t, 0))
    single = pl.Buffered(1)
    return pl.pallas_call(
        _out_kernel,
        grid=(nb, length // tm),
        in_specs=[tok(D_MODEL), tok(POOL_W), tok(ATT_W), tok(DN_W),
                  _mod_spec(mod, l, 2, per_token), _mod_spec(mod, l, 3, per_token),
                  _mod_spec(mod, l, 4, per_token), _mod_spec(mod, l, 5, per_token),
                  _layer_spec(n2, l),
                  _layer_spec(wo, l, pipeline_mode=single), _layer_spec(wg, l, pipeline_mode=single),
                  _layer_spec(wu, l, pipeline_mode=single), _layer_spec(wd, l, pipeline_mode=single)],
        out_specs=tok(D_MODEL),
        out_shape=jax.ShapeDtypeStruct((nb, length, D_MODEL), F32),
        compiler_params=_cparams("arbitrary", "arbitrary"),
        name="out_ffn",
    )(x3, yp, ya, yd, mod, mod, mod, mod, n2, wo, wg, wu, wd)


def kernel(x_prompt, x_sample, cache_k, cache_v, state_pool, state_conv, state_delta, page_table,
           c_prompt, c_sample, norm1_w, ada_w, ada_b, w_in, pool_w, pool_scale, q_norm_w, k_norm_w,
           conv_w, a_log, dt_bias, dn_norm_w, w_out, norm2_w, w_gate, w_up, w_down):
    depth = w_in.shape[0]
    nbp, seq, _ = x_prompt.shape
    nbs = x_sample.shape[0]
    past_len = page_table.shape[1] * PAGE

    w_ext = jnp.pad(w_in, ((0, 0), (0, 0), (0, IN_EXT - w_in.shape[2]))).astype(BF16)
    wo_b, wg_b, wu_b, wd_b = (w.astype(BF16) for w in (w_out, w_gate, w_up, w_down))
    eye_g = jnp.eye(len(POOL_WINDOWS), dtype=F32)
    pw_bd = (eye_g[None, :, None, :, None] * pool_w[:, :, :, None, :]).reshape(depth, POOL_W, POOL_W).astype(BF16)
    ones_att = jnp.kron(jnp.eye(ATT_HEADS, dtype=F32), jnp.ones((HEAD_DIM, HEAD_DIM), F32)).astype(BF16)
    qn4 = jnp.tile(q_norm_w, (1, ATT_HEADS)).reshape(depth, 1, ATT_W)
    kn4 = jnp.tile(k_norm_w, (1, ATT_HEADS)).reshape(depth, 1, ATT_W)
    dnw4 = jnp.tile(dn_norm_w, (1, DN_HEADS)).reshape(depth, 1, DN_W)
    alog4 = jnp.repeat(a_log, DN_DK, axis=-1).reshape(depth, 1, DN_W)
    dtb4 = jnp.repeat(dt_bias, DN_DK, axis=-1).reshape(depth, 1, DN_W)
    n1 = norm1_w.reshape(depth, 1, D_MODEL)
    n2 = norm2_w.reshape(depth, 1, D_MODEL)
    ps = pool_scale.reshape(depth, 1, POOL_W)
    cache_kt = jnp.transpose(cache_k, (0, 1, 3, 4, 2))
    cache_vt = jnp.transpose(cache_v, (0, 1, 3, 4, 2))

    mod = _modulation(jnp.concatenate([c_prompt, c_sample], axis=0), ada_w, ada_b)
    mod_p = mod[:, :nbp].reshape(depth, nbp, N_MOD, 1, D_MODEL)
    mod_s = mod[:, nbp:].reshape(depth, nbs, N_MOD, D_MODEL).transpose(0, 2, 1, 3)

    xp = x_prompt
    xs = x_sample.reshape(1, nbs, D_MODEL)
    zero_pool = jnp.zeros((nbp, POOL_BUF, POOL_W), F32)
    zero_pool_s = jnp.zeros((nbs, POOL_BUF, POOL_W), F32)
    zero_conv = jnp.zeros((nbp, CONV_W - 1, 3 * DN_W), F32)
    zero_state = jnp.zeros((nbp, DN_HEADS, DN_DK, DN_DK), F32)
    pad_rows = lambda a: jnp.pad(a.reshape(nbs, 1, a.shape[-1]), ((0, 0), (0, SAMPLE_CHUNK - 1), (0, 0)))

    outs = {k: [] for k in ("ks", "vs", "pp", "ps", "cp", "cs", "sp", "ss")}
    kv_all = None
    for l in range(depth):
        u, q, k_all, v_all, d, z, gates = _in_proj(xp, mod_p, l, False, n1, w_ext, qn4, kn4, ones_att,
                                                   TM_IN, kv_depth=depth, kv_prev=kv_all)
        kv_all = (k_all, v_all)
        y_pool, pool_new = _pool(u, zero_pool, pw_bd, ps, 0, l)
        y_att = _attn_prompt(q, k_all, v_all, l)
        y_dn, conv_new, s_new = _gdn(d, z, gates, zero_conv, zero_state, conv_w, alog4, dtb4, dnw4,
                                     l, seq, GDN_BT, CHUNK)
        xp = _out_ffn(xp, y_pool, y_att, y_dn, mod_p, l, False, n2, wo_b, wg_b, wu_b, wd_b, TM)
        outs["pp"].append(pool_new)
        outs["cp"].append(conv_new)
        outs["sp"].append(s_new)

        us, qs, ks, vs, ds, zs, gs = _in_proj(xs, mod_s, l, True, n1, w_ext, qn4, kn4, ones_att, nbs)
        ext = jnp.concatenate([state_pool[l], us.reshape(nbs, 1, POOL_W)], axis=1)
        y_pool16, pool_new = _pool(ext, zero_pool_s, pw_bd, ps, past_len - POOL_BUF, l, bs=POOL_BS_SAMPLE)
        y_pool = y_pool16[:, POOL_BUF:].reshape(1, nbs, POOL_W)
        o_att = _attn_sample(qs.reshape(nbs, ATT_HEADS, HEAD_DIM).astype(F32),
                             ks.reshape(nbs, ATT_HEADS, HEAD_DIM), vs.reshape(nbs, ATT_HEADS, HEAD_DIM),
                             cache_kt, cache_vt, page_table, l)
        y_att = o_att.reshape(1, nbs, ATT_W)
        y_dnc, conv_new, s_new = _gdn(pad_rows(ds), pad_rows(zs), pad_rows(gs),
                                      state_conv, state_delta, conv_w, alog4, dtb4, dnw4,
                                      l, 1, GDN_BT_SAMPLE, SAMPLE_CHUNK, stacked_state=True)
        y_dn = y_dnc[:, 0].reshape(1, nbs, DN_W)
        xs = _out_ffn(xs, y_pool, y_att, y_dn, mod_s, l, True, n2, wo_b, wg_b, wu_b, wd_b, nbs)
        outs["ks"].append(ks.reshape(nbs, 1, ATT_HEADS, HEAD_DIM))
        outs["vs"].append(vs.reshape(nbs, 1, ATT_HEADS, HEAD_DIM))
        outs["ps"].append(pool_new)
        outs["cs"].append(conv_new)
        outs["ss"].append(s_new)

    st = lambda name: jnp.stack(outs[name])
    k_prompt, v_prompt = (a.reshape(depth, nbp, seq, ATT_HEADS, HEAD_DIM) for a in kv_all)
    return (xp, xs.reshape(nbs, 1, D_MODEL), k_prompt, v_prompt, st("ks"), st("vs"),
            st("pp"), st("ps"), st("cp"), st("cs"), st("sp"), st("ss"))
```

```python
import functools

import numpy as np
import jax
import jax.numpy as jnp
from jax import lax
from jax.experimental import pallas as pl
from jax.experimental.pallas import tpu as pltpu

F32 = jnp.float32
BF16 = jnp.bfloat16

D_MODEL = 1024
PAGE = 128
POOL_WINDOWS = (2, 4, 8, 16)
POOL_GDIM = 64
POOL_W = 256
POOL_BUF = 15
HEAD_DIM = 64
ATT_HEADS = 8
ATT_W = 512
BLK = 256
TOPK = 3
DN_HEADS = 4
DN_DK = 64
DN_W = 256
CHUNK = 64
SAMPLE_CHUNK = 16
CONV_W = 4
D_FF = 2816
N_MOD = 6
EPS = 1e-6
NEG = -1e30
LOG2E = 1.4426950408889634

C_U = (0, 256)
C_Q = (256, 768)
C_K = (768, 1280)
C_V = (1280, 1792)
C_D = (1792, 2560)
C_Z = (2560, 2816)
C_G = (2816, 2944)
IN_EXT = 2944
GATE_W = 128

TM = 512
GDN_BT = 8
SA_PAGES_PER_STEP = 32
VMEM_LIMIT = 56 * 1024 * 1024


def _cparams(*sem):
    return pltpu.CompilerParams(dimension_semantics=sem, vmem_limit_bytes=VMEM_LIMIT)


def _sigmoid(x):
    return 1.0 / (1.0 + jnp.exp(-x))


def _bdot(a, b):
    return jnp.dot(a.astype(BF16), b.astype(BF16), preferred_element_type=F32)


def _bdot_t(a, b):
    return lax.dot_general(a.astype(BF16), b.astype(BF16), (((1,), (1,)), ((), ())),
                           preferred_element_type=F32)


def _split3(a):
    hi = a.astype(BF16)
    r1 = a - hi.astype(F32)
    mid = r1.astype(BF16)
    lo = (r1 - mid.astype(F32)).astype(BF16)
    return hi, mid, lo


def _dot3_rhs_exact(a, b01):
    return sum(jnp.dot(p, b01, preferred_element_type=F32) for p in _split3(a))


def _dot3_lhs_exact(a01, b):
    return sum(jnp.dot(a01, p, preferred_element_type=F32) for p in _split3(b))


def _dot3_t_rhs_exact(a, b01):
    return sum(lax.dot_general(p, b01, (((1,), (1,)), ((), ())), preferred_element_type=F32)
               for p in _split3(a))


def _dot3_t_lhs_exact(a01, b):
    return sum(lax.dot_general(a01, p, (((1,), (1,)), ((), ())), preferred_element_type=F32)
               for p in _split3(b))


def _mod_kernel(c_ref, w_ref, b_ref, o_ref):
    c = c_ref[...]
    a = (c * _sigmoid(c)).astype(BF16)
    o_ref[...] = jnp.dot(a, w_ref[...].astype(BF16), preferred_element_type=F32) + b_ref[...]


def _modulation(c_all, ada_w, ada_b):
    depth, _, ncol = ada_w.shape
    nseq = c_all.shape[0]
    tn = 1536
    return pl.pallas_call(
        _mod_kernel,
        grid=(depth, ncol // tn),
        in_specs=[pl.BlockSpec((nseq, D_MODEL), lambda l, j: (0, 0)),
                  pl.BlockSpec((None, D_MODEL, tn), lambda l, j: (l, 0, j)),
                  pl.BlockSpec((None, 1, tn), lambda l, j: (l, 0, j))],
        out_specs=pl.BlockSpec((None, nseq, tn), lambda l, j: (l, 0, j)),
        out_shape=jax.ShapeDtypeStruct((depth, nseq, ncol), F32),
        compiler_params=_cparams("arbitrary", "arbitrary"),
        name="modulation",
    )(c_all, ada_w, ada_b.reshape(depth, 1, ncol))


def _layer_spec(stacked, l, **kw):
    zeros = (0,) * (stacked.ndim - 1)
    return pl.BlockSpec((None,) + stacked.shape[1:], lambda *g: (l,) + zeros, **kw)


def _mod_spec(mod, l, k, per_token):
    if per_token:
        return pl.BlockSpec((None, None, mod.shape[2], D_MODEL), lambda *g: (l, k, 0, 0))
    return pl.BlockSpec((None, None, None, 1, D_MODEL), lambda *g: (l, g[0], k, 0, 0))


def _in_kernel(x_ref, sh_ref, sc_ref, n1_ref, w_ref, qn_ref, kn_ref, ones_ref, *rest):
    u_ref, q_ref, k_ref, v_ref, d_ref, z_ref, g_ref = rest[-7:]
    x = x_ref[...]
    ms = jnp.mean(x * x, axis=-1, keepdims=True)
    h = (x * lax.rsqrt(ms + EPS) * n1_ref[...]) * (1.0 + sc_ref[...]) + sh_ref[...]
    hb = h.astype(BF16)

    def proj(c):
        return jnp.dot(hb, w_ref[:, c[0]:c[1]], preferred_element_type=F32)

    def head_norm(t, w4):
        ss = jnp.dot((t * t).astype(BF16), ones_ref[...], preferred_element_type=F32)
        return t * lax.rsqrt(ss * (1.0 / HEAD_DIM) + EPS) * w4

    u_ref[...] = proj(C_U)
    q_ref[...] = (head_norm(proj(C_Q), qn_ref[...]) * (HEAD_DIM ** -0.5 * LOG2E)).astype(BF16)
    k_ref[...] = head_norm(proj(C_K), kn_ref[...])
    v_ref[...] = proj(C_V)
    d_ref[...] = proj(C_D)
    z_ref[...] = proj(C_Z)
    g_ref[...] = proj(C_G)


def _in_proj(x3, mod, l, per_token, n1, w_ext, qn4, kn4, ones_att, tm, kv_depth=None, kv_prev=None,
             after=None):
    nb, length, _ = x3.shape
    widths = (POOL_W, ATT_W, ATT_W, ATT_W, 3 * DN_W, DN_W, GATE_W)
    dtypes = (F32, BF16, F32, F32, F32, F32, F32)
    const = lambda b, t: (0, 0)
    tok = lambda w: pl.BlockSpec((None, tm, w), lambda b, t: (b, t, 0))
    out_specs = [tok(w) for w in widths]
    out_shape = [jax.ShapeDtypeStruct((nb, length, w), dt) for w, dt in zip(widths, dtypes)]
    operands = [x3, mod, mod, n1, w_ext, qn4, kn4, ones_att]
    in_specs = [tok(D_MODEL), _mod_spec(mod, l, 0, per_token), _mod_spec(mod, l, 1, per_token),
                _layer_spec(n1, l), _layer_spec(w_ext, l), _layer_spec(qn4, l), _layer_spec(kn4, l),
                pl.BlockSpec((ATT_W, ATT_W), const)]
    aliases = {}
    if after is not None:
        operands.append(after)
        in_specs.append(pl.BlockSpec(memory_space=pl.ANY))
    if kv_depth is not None:
        for o in (2, 3):
            out_specs[o] = pl.BlockSpec((None, None, tm, ATT_W), lambda b, t: (l, b, t, 0))
            out_shape[o] = jax.ShapeDtypeStruct((kv_depth, nb, length, ATT_W), F32)
        if kv_prev is not None:
            aliases = {len(operands): 2, len(operands) + 1: 3}
            operands += list(kv_prev)
            in_specs += [pl.BlockSpec(memory_space=pl.ANY)] * 2
    return pl.pallas_call(
        _in_kernel,
        grid=(nb, length // tm),
        in_specs=in_specs,
        out_specs=out_specs,
        out_shape=out_shape,
        input_output_aliases=aliases,
        compiler_params=_cparams("arbitrary", "arbitrary"),
        name="in_proj",
    )(*operands)


POOL_ROW0 = 32


def _pool_kernel(u_ref, buf_ref, pw_ref, ps_ref, y_ref, new_ref, e_scr, s2_scr, s4_scr, s8_scr, *,
                 length, pos0, bs):
    zeros16 = jnp.zeros((16, POOL_W), F32)
    lane = lax.broadcasted_iota(jnp.int32, (1, POOL_W), 1)
    grp = jnp.right_shift(lane, 6)
    wl = jnp.where(grp == 0, 2, jnp.where(grp == 1, 4, jnp.where(grp == 2, 8, 16)))
    ch = min(length, 256)
    row = lax.broadcasted_iota(jnp.int32, (ch, 1), 0)
    spans = [(16, 16)] + [(POOL_ROW0 + c * ch, ch) for c in range(length // ch)]
    for s in range(bs):
        e_scr[s, pl.ds(8, 16), :] = zeros16
        s2_scr[s, pl.ds(8, 8), :] = zeros16[:8]
        s4_scr[s, pl.ds(8, 8), :] = zeros16[:8]
        e_scr[s, pl.ds(POOL_ROW0 - POOL_BUF, POOL_BUF), :] = buf_ref[s]
        e_scr[s, pl.ds(POOL_ROW0, length), :] = u_ref[s]
        for src, dst, shift in ((e_scr, s2_scr, 1), (s2_scr, s4_scr, 2), (s4_scr, s8_scr, 4)):
            for base, n in spans:
                dst[s, pl.ds(base, n), :] = src[s, pl.ds(base, n), :] + src[s, pl.ds(base - shift, n), :]
        for c in range(length // ch):
            base = POOL_ROW0 + c * ch
            cur = e_scr[s, pl.ds(base, ch), :]
            s16 = s8_scr[s, pl.ds(base, ch), :] + s8_scr[s, pl.ds(base - 8, ch), :]
            wsum = jnp.where(grp == 0, s2_scr[s, pl.ds(base, ch), :],
                             jnp.where(grp == 1, s4_scr[s, pl.ds(base, ch), :],
                                       jnp.where(grp == 2, s8_scr[s, pl.ds(base, ch), :], s16)))
            cnt = jnp.minimum(wl, row + (pos0 + c * ch + 1)).astype(F32)
            d = wsum / cnt - cur
            y = jnp.dot(d.astype(BF16), pw_ref[...], preferred_element_type=F32) * ps_ref[...]
            y_ref[s, pl.ds(c * ch, ch), :] = y.astype(y_ref.dtype)
        new_ref[s] = e_scr[s, pl.ds(POOL_ROW0 + length - POOL_BUF, POOL_BUF), :]


def _pool(u3, buf3, pw_bd, ps, pos0, l, bs=1):
    nb, length, _ = u3.shape
    return pl.pallas_call(
        functools.partial(_pool_kernel, length=length, pos0=pos0, bs=bs),
        grid=(nb // bs,),
        in_specs=[pl.BlockSpec((bs, length, POOL_W), lambda b: (b, 0, 0)),
                  pl.BlockSpec((bs, POOL_BUF, POOL_W), lambda b: (b, 0, 0)),
                  _layer_spec(pw_bd, l), _layer_spec(ps, l)],
        out_specs=[pl.BlockSpec((bs, length, POOL_W), lambda b: (b, 0, 0)),
                   pl.BlockSpec((bs, POOL_BUF, POOL_W), lambda b: (b, 0, 0))],
        out_shape=[jax.ShapeDtypeStruct((nb, length, POOL_W), BF16),
                   jax.ShapeDtypeStruct((nb, POOL_BUF, POOL_W), F32)],
        scratch_shapes=[pltpu.VMEM((bs, POOL_ROW0 + length, POOL_W), F32)] * 4,
        compiler_params=_cparams("arbitrary"),
        name="pool_mixer",
    )(u3, buf3, pw_bd, ps)


def _attn_kernel(q_ref, k_ref, v_ref, o_ref, km_scr, vt_scr, ot_scr, s_scr, p_scr, *, length):
    nb = length // BLK
    lane = lax.broadcasted_iota(jnp.int32, (1, 128), 1)
    k2 = k_ref[...]
    kmean = jnp.sum(k2.reshape(nb, BLK, 128), axis=1) * (1.0 / BLK)
    vt_scr[...] = v_ref[...].T.astype(BF16)
    krow = lax.broadcasted_iota(jnp.int32, (BLK, BLK), 0)
    qcol = lax.broadcasted_iota(jnp.int32, (BLK, BLK), 1)
    causal = krow <= qcol
    blkrow = lax.broadcasted_iota(jnp.int32, (nb, BLK), 0)
    pad_rows = jnp.zeros((16 - nb, 128), F32) if nb < 16 else None
    kmh = []
    for hh in range(2):
        hm = jnp.right_shift(lane, 6) == hh
        km_scr[hh] = jnp.where(hm, k2, 0.0).astype(BF16)
        t = jnp.where(hm, kmean, 0.0)
        kmh.append(t if pad_rows is None else jnp.concatenate([t, pad_rows], axis=0))

    def fold8(t):
        return t.reshape(BLK // 8, 8, BLK)

    def scores(hh, i, slot):
        qi = q_ref[pl.ds(i * BLK, BLK), :]
        bias = None
        if i > 0:
            gate = _bdot_t(kmh[hh], qi)[:nb]
            past = blkrow < i
            gate = jnp.where(past, gate, -jnp.inf)
            cnt = jnp.zeros((nb, BLK), jnp.int32)
            for jp in range(i):
                gj = gate[jp:jp + 1, :]
                cnt = cnt + jnp.where(gj > gate, 1,
                                      jnp.where(gj == gate, (jp < blkrow).astype(jnp.int32), 0))
            sel = jnp.where(past, cnt, TOPK) < TOPK
            bias = jnp.where(sel, 0.0, NEG)
        m8 = None
        for j in range(i + 1):
            s = _bdot_t(km_scr[hh, j * BLK:(j + 1) * BLK, :], qi)
            s = jnp.where(causal, s, NEG) if j == i else s + bias[j:j + 1, :]
            s_scr[slot, j] = s
            t = jnp.max(fold8(s), axis=0)
            m8 = t if m8 is None else jnp.maximum(m8, t)
        return jnp.max(m8, axis=0, keepdims=True)

    def weighted(hh, i, slot, m):
        l8 = None
        for j in range(i + 1):
            p = jnp.exp2(s_scr[slot, j] - m)
            t = jnp.sum(fold8(p), axis=0)
            l8 = t if l8 is None else l8 + t
            p_scr[slot, j * BLK:(j + 1) * BLK, :] = p.astype(BF16)
        acc = jnp.dot(vt_scr[hh * 64:(hh + 1) * 64, 0:(i + 1) * BLK], p_scr[slot, 0:(i + 1) * BLK, :],
                      preferred_element_type=F32)
        l = jnp.sum(l8, axis=0, keepdims=True)
        ot_scr[hh * 64:(hh + 1) * 64, i * BLK:(i + 1) * BLK] = acc / l

    items = [(hh, i) for hh in range(2) for i in range(nb)]
    pending = None
    for n, (hh, i) in enumerate(items):
        m = scores(hh, i, n % 2)
        if pending is not None:
            weighted(*pending)
        pending = (hh, i, n % 2, m)
    weighted(*pending)
    o_ref[...] = ot_scr[...].T.astype(o_ref.dtype)


def _attn_prompt(q3, k_all, v_all, l):
    nb, length, _ = q3.shape
    nblk = length // BLK
    spec = pl.BlockSpec((None, length, 128), lambda b, h: (b, 0, h))
    kv_spec = pl.BlockSpec((None, None, length, 128), lambda b, h: (l, b, 0, h))
    return pl.pallas_call(
        functools.partial(_attn_kernel, length=length),
        grid=(nb, ATT_W // 128),
        in_specs=[spec, kv_spec, kv_spec],
        out_specs=spec,
        out_shape=jax.ShapeDtypeStruct((nb, length, ATT_W), BF16),
        scratch_shapes=[pltpu.VMEM((2, length, 128), BF16),
                        pltpu.VMEM((128, length), BF16),
                        pltpu.VMEM((128, length), F32),
                        pltpu.VMEM((2, nblk, BLK, BLK), F32),
                        pltpu.VMEM((2, length, BLK), BF16)],
        compiler_params=_cparams("arbitrary", "arbitrary"),
        name="moba_prompt",
    )(q3, k_all, v_all)


def _sa_partials(kp, vp, qbc_ref, s_scr, g_scr, m_scr, l_scr, o_scr, first_blk):
    full = (ATT_HEADS, PAGE)
    for pg in range(len(kp)):
        for h in range(ATT_HEADS):
            s_scr[pg, pl.ds(h, 1), :] = jnp.sum(kp[pg][h] * qbc_ref[h], axis=0, keepdims=True)
    for blk in range(len(kp) // 2):
        s0 = s_scr[2 * blk]
        s1 = s_scr[2 * blk + 1]
        gsum = jnp.sum(s0, axis=-1, keepdims=True) + jnp.sum(s1, axis=-1, keepdims=True)
        mb = jnp.maximum(jnp.max(s0, axis=-1, keepdims=True), jnp.max(s1, axis=-1, keepdims=True))
        p0 = jnp.exp2(s0 - mb)
        p1 = jnp.exp2(s1 - mb)
        lb = jnp.sum(p0, axis=-1, keepdims=True) + jnp.sum(p1, axis=-1, keepdims=True)
        s_scr[2 * blk] = p0
        s_scr[2 * blk + 1] = p1
        idx = first_blk + blk
        g_scr[idx] = jnp.broadcast_to(gsum, full)
        m_scr[idx] = jnp.broadcast_to(mb, full)
        l_scr[idx] = jnp.broadcast_to(lb, full)
        for h in range(ATT_HEADS):
            o_scr[idx, h] = (vp[2 * blk][h] * s_scr[2 * blk, pl.ds(h, 1), :]
                             + vp[2 * blk + 1][h] * s_scr[2 * blk + 1, pl.ds(h, 1), :])


def _sa_merge(q_ref, kn_ref, vn_ref, o_ref, g_scr, m_scr, l_scr, o_scr, w_scr, nblk):
    full = (ATT_HEADS, PAGE)
    gates = g_scr[pl.ds(0, nblk)]
    if nblk > TOPK:
        lane = lax.broadcasted_iota(jnp.int32, full, 1)
        gates_c = jnp.zeros(full, F32)
        for b in range(nblk):
            gates_c = jnp.where(lane == b, gates[b], gates_c)
        cnt = jnp.zeros(full, jnp.int32)
        for b in range(nblk):
            cnt = cnt + jnp.where(gates[b] > gates_c, 1,
                                  jnp.where(gates[b] == gates_c, (b < lane).astype(jnp.int32), 0))
        last = jnp.where(lane < nblk, cnt, -1) == TOPK - 1
        g3 = jnp.max(jnp.where(last, gates_c, -jnp.inf), axis=-1, keepdims=True)
        b3 = jnp.max(jnp.where(last, lane.astype(F32), -1.0), axis=-1, keepdims=True)
        bidx = lax.broadcasted_iota(jnp.int32, gates.shape, 0).astype(F32)
        sel = (gates > g3[None]) | ((gates == g3[None]) & (bidx <= b3[None]))
    else:
        sel = jnp.full(gates.shape, True)
    sself = jnp.broadcast_to(jnp.sum(q_ref[...] * kn_ref[...], axis=-1, keepdims=True), full)
    mb = m_scr[pl.ds(0, nblk)]
    mtot = jnp.maximum(jnp.max(jnp.where(sel, mb, -jnp.inf), axis=0), sself)
    w = jnp.where(sel, jnp.exp2(jnp.where(sel, mb - mtot[None], 0.0)), 0.0)
    wself = jnp.exp2(sself - mtot)
    denom = jnp.sum(w * l_scr[pl.ds(0, nblk)], axis=0) + wself
    w_scr[...] = w
    ones8 = jnp.ones((8, PAGE), BF16)
    sub = lax.broadcasted_iota(jnp.int32, (ATT_HEADS, HEAD_DIM), 0)
    o_acc = jnp.zeros((ATT_HEADS, HEAD_DIM), F32)
    for h in range(ATT_HEADS):
        tot = w_scr[0, pl.ds(h, 1), :] * o_scr[0, h]
        for b in range(1, nblk):
            tot = tot + w_scr[b, pl.ds(h, 1), :] * o_scr[b, h]
        r = _dot3_t_lhs_exact(ones8, tot)
        o_acc = jnp.where(sub == h, r, o_acc)
    o_ref[...] = (o_acc + wself[:, :HEAD_DIM] * vn_ref[...]) / denom[:, :HEAD_DIM]


def _sa_kernel(pt_ref, qbc_ref, q_ref, kn_ref, vn_ref, *refs, gpages, n_steps):
    kp = refs[:gpages]
    vp = refs[gpages:2 * gpages]
    o_ref = refs[2 * gpages]
    g_scr, m_scr, l_scr, o_scr, s_scr, w_scr = refs[2 * gpages + 1:]
    step = pl.program_id(1)
    _sa_partials(kp, vp, qbc_ref, s_scr, g_scr, m_scr, l_scr, o_scr, step * (gpages // 2))

    @pl.when(step == n_steps - 1)
    def _merge():
        _sa_merge(q_ref, kn_ref, vn_ref, o_ref, g_scr, m_scr, l_scr, o_scr, w_scr, n_steps * (gpages // 2))


def _attn_sample(q3, kn3, vn3, cache_kt, cache_vt, page_table, l):
    nb = q3.shape[0]
    n_pages = page_table.shape[1]
    gpages = SA_PAGES_PER_STEP
    n_steps = n_pages // gpages
    nblk = n_pages // 2
    qbc = jnp.broadcast_to(q3[..., None], (nb, ATT_HEADS, HEAD_DIM, PAGE))
    tok = lambda: pl.BlockSpec((None, ATT_HEADS, HEAD_DIM), lambda b, s, pt: (b, 0, 0))

    def page_spec(i):
        return pl.BlockSpec((None, None, ATT_HEADS, HEAD_DIM, PAGE),
                            lambda b, s, pt: (l, pt[b, s * gpages + i], 0, 0, 0))

    grid_spec = pltpu.PrefetchScalarGridSpec(
        num_scalar_prefetch=1,
        grid=(nb, n_steps),
        in_specs=[pl.BlockSpec((None, ATT_HEADS, HEAD_DIM, PAGE), lambda b, s, pt: (b, 0, 0, 0)),
                  tok(), tok(), tok()]
                 + [page_spec(i) for i in range(gpages)] + [page_spec(i) for i in range(gpages)],
        out_specs=tok(),
        scratch_shapes=[pltpu.VMEM((nblk, ATT_HEADS, PAGE), F32),
                        pltpu.VMEM((nblk, ATT_HEADS, PAGE), F32),
                        pltpu.VMEM((nblk, ATT_HEADS, PAGE), F32),
                        pltpu.VMEM((nblk, ATT_HEADS, HEAD_DIM, PAGE), F32),
                        pltpu.VMEM((gpages, ATT_HEADS, PAGE), F32),
                        pltpu.VMEM((nblk, ATT_HEADS, PAGE), F32)],
    )
    return pl.pallas_call(
        functools.partial(_sa_kernel, gpages=gpages, n_steps=n_steps),
        grid_spec=grid_spec,
        out_shape=jax.ShapeDtypeStruct((nb, ATT_HEADS, HEAD_DIM), F32),
        compiler_params=_cparams("arbitrary", "arbitrary"),
        name="moba_sample",
    )(page_table, qbc, q3, kn3, vn3, *([cache_kt] * gpages), *([cache_vt] * gpages))


def _gdn_constants(chunk):
    C = chunk
    lane = np.arange(DN_W)
    lane_j = lane % 64
    row = np.arange(C)[:, None]
    tri = np.stack([lane_j[None] <= row, lane_j[None] < row, lane_j[None] == row, row <= lane_j[None]])
    l_incl = np.arange(C)[None, :] <= np.arange(C)[:, None]
    blk = np.arange(DN_W) // 64
    bd = blk[:, None] == blk[None, :]
    place = np.stack([np.arange(DN_W)[None, :] == (np.arange(DN_DK)[:, None] + 64 * h) for h in range(DN_HEADS)])
    half = np.stack([np.broadcast_to((np.arange(128) // 64) == s, (C, 128)) for s in range(2)])
    expand = np.arange(GATE_W)[:, None] == (np.arange(2 * DN_W)[None, :] // 64)
    return (jnp.asarray(tri, F32), jnp.asarray(l_incl, BF16), jnp.asarray(bd, BF16), jnp.asarray(bd, F32),
            jnp.asarray(place, BF16), jnp.asarray(half, F32), jnp.asarray(expand, BF16))


def _gdn_kernel(d_ref, z_ref, g_ref, c0_ref, s0_ref, cw_ref, alog_ref, dtb_ref, dnw_ref,
                tri_ref, lincl_ref, onesbd_ref, bdmask_ref, place_ref, half_ref, expand_ref,
                y_ref, cout_ref, sout_ref, ext_scr, s_scr, *, bt, n_chunks, l_valid, chunk):
    n = pl.program_id(1)
    C = chunk
    incl4 = tri_ref[0] > 0.5
    strict4 = tri_ref[1] > 0.5
    eye4 = tri_ref[2]
    ut4 = tri_ref[3]
    l_incl = lincl_ref[...]
    ones_bd = onesbd_ref[...]
    zero_half = jnp.zeros((C, 128), F32)
    left = half_ref[0] > 0.5
    right = half_ref[1] > 0.5

    def stack_mask(a4):
        lo = a4[:, :128]
        hi = a4[:, 128:]
        pieces = [jnp.concatenate([jnp.where(left, lo, 0.0), zero_half], axis=1),
                  jnp.concatenate([jnp.where(right, lo, 0.0), zero_half], axis=1),
                  jnp.concatenate([zero_half, jnp.where(left, hi, 0.0)], axis=1),
                  jnp.concatenate([zero_half, jnp.where(right, hi, 0.0)], axis=1)]
        if C < DN_DK:
            fill = jnp.zeros((DN_DK - C, DN_W), F32)
            pieces = [x for p in pieces for x in (p, fill)]
        return jnp.concatenate(pieces, axis=0).astype(BF16)

    @pl.when(n == 0)
    def _init():
        for bi in range(bt):
            ext_scr[bi, pl.ds(5, CONV_W - 1), :] = c0_ref[bi]
            s_scr[bi] = jnp.concatenate(
                [_dot3_rhs_exact(s0_ref[bi, h], place_ref[h]) for h in range(DN_HEADS)], axis=0)

    last_valid = l_valid - (n_chunks - 1) * C
    masked = last_valid != C
    if masked:
        row = lax.broadcasted_iota(jnp.int32, (C, 1), 0)
        valid = (n * C + row) < l_valid

    seqs = range(bt)
    each = lambda f, *cols: [f(*xs) for xs in zip(*cols)]
    cw = cw_ref[...]
    act = []
    for bi in seqs:
        ext_scr[bi, pl.ds(8, C), :] = d_ref[bi]
        yc = (cw[0:1] * ext_scr[bi, pl.ds(5, C), :] + cw[1:2] * ext_scr[bi, pl.ds(6, C), :]
              + cw[2:3] * ext_scr[bi, pl.ds(7, C), :] + cw[3:4] * ext_scr[bi, pl.ds(8, C), :])
        act.append(yc * _sigmoid(yc))
        cout_ref[bi] = ext_scr[bi, pl.ds(last_valid + 5, CONV_W - 1), :]
        ext_scr[bi, pl.ds(5, CONV_W - 1), :] = ext_scr[bi, pl.ds(8 + C - (CONV_W - 1), CONV_W - 1), :]

    q_raw = [a[:, 0:DN_W] for a in act]
    k_raw = [a[:, DN_W:2 * DN_W] for a in act]
    v4 = [a[:, 2 * DN_W:3 * DN_W] for a in act]
    ssq = each(lambda x: _bdot(x * x, ones_bd), q_raw)
    ssk = each(lambda x: _bdot(x * x, ones_bd), k_raw)
    q4 = each(lambda x, s: x * lax.rsqrt(s + EPS) * (DN_DK ** -0.5), q_raw, ssq)
    k4 = each(lambda x, s: x * lax.rsqrt(s + EPS), k_raw, ssk)
    raw = [_dot3_rhs_exact(g_ref[bi], expand_ref[...]) for bi in seqs]
    beta4 = [_sigmoid(r[:, :DN_W]) for r in raw]

    def log_decay(r):
        xg = r[:, DN_W:] + dtb_ref[...]
        return -jnp.exp(alog_ref[...]) * (jnp.maximum(xg, 0.0) + jnp.log1p(jnp.exp(-jnp.abs(xg))))

    g4 = [log_decay(r) for r in raw]
    if masked:
        zero_pad = lambda x: jnp.where(valid, x, 0.0)
        q4, k4, v4, beta4, g4 = (each(zero_pad, c) for c in (q4, k4, v4, beta4, g4))

    gi = each(lambda g: _dot3_lhs_exact(l_incl, g), g4)
    gj = each(lambda g: jnp.sum(g * ut4, axis=0, keepdims=True), g4)
    decay4 = each(lambda a, b: jnp.where(incl4, jnp.exp(jnp.where(incl4, a - b, 0.0)), 0.0), gi, gj)
    eg4 = each(jnp.exp, gi)
    glast = [g[C - 1:C, :] for g in gi]
    kfac = each(lambda a, b: jnp.exp(a - b), glast, gi)
    gtot = each(jnp.exp, glast)

    kb4 = each(lambda a, b: a * b, k4, beta4)
    mk = each(lambda kb, q, k: _bdot_t(jnp.concatenate([kb, q], axis=0), stack_mask(k)), kb4, q4, k4)
    attn4 = each(lambda r, dc: r[C:] * dc, mk, decay4)

    p = each(lambda r, dc: -jnp.where(strict4, r[:C] * dc, 0.0), mk, decay4)
    t = each(lambda x: eye4 + x, p)
    p = each(lambda x: _bdot(x, stack_mask(x)), p)
    for _ in range(C.bit_length() - 3):
        r = each(lambda a, b: _bdot(jnp.concatenate([a, b], axis=0), stack_mask(b)), t, p)
        t = each(lambda a, b: a + b[:C], t, r)
        p = [x[C:] for x in r]
    t = each(lambda a, b: a + _bdot(a, stack_mask(b)), t, p)

    u4 = each(lambda a, v, b: _bdot(a, stack_mask(v * b)), t, v4, beta4)
    kc4 = each(lambda a, kb, e: _bdot(a, stack_mask(kb * e)), t, kb4, eg4)

    sbd = [s_scr[bi] for bi in seqs]
    r = each(lambda kc, q, e, s: _bdot(jnp.concatenate([kc, q * e], axis=0), s), kc4, q4, eg4, sbd)
    vnew = each(lambda u, x: u - x[:C], u4, r)
    o4 = each(lambda x, a, v: x[C:] + _bdot(a, stack_mask(v)), r, attn4, vnew)
    upd = each(lambda k, f, v: lax.dot_general((k * f).astype(BF16), v.astype(BF16), (((0,), (0,)), ((), ())),
                                               preferred_element_type=F32), k4, kfac, vnew)
    sso = each(lambda o: _bdot(o * o, ones_bd), o4)
    for bi in seqs:
        s_scr[bi] = sbd[bi] * gtot[bi] + upd[bi] * bdmask_ref[...]
        zz = z_ref[bi]
        y_ref[bi] = (o4[bi] * lax.rsqrt(sso[bi] * (1.0 / DN_DK) + EPS) * dnw_ref[...]
                     * (zz * _sigmoid(zz))).astype(y_ref.dtype)

    @pl.when(n == n_chunks - 1)
    def _fin():
        for bi in range(bt):
            for h in range(DN_HEADS):
                sout_ref[bi, h] = _dot3_t_rhs_exact(s_scr[bi, h * 64:(h + 1) * 64, :], place_ref[h])


def _gdn(d3, z3, g3, conv0, s0, cw, alog4, dtb4, dnw4, l, l_valid, bt, chunk, stacked_state=False):
    nb, length, _ = d3.shape
    n_chunks = length // chunk
    tok = lambda w: pl.BlockSpec((bt, chunk, w), lambda i, n: (i, n, 0))
    const = lambda i, n: (0, 0)
    const3 = lambda i, n: (0, 0, 0)
    conv_blk = (bt, CONV_W - 1, 3 * DN_W)
    state_blk = (bt, DN_HEADS, DN_DK, DN_DK)
    if stacked_state:
        conv_spec = pl.BlockSpec((None,) + conv_blk, lambda i, n: (l, i, 0, 0))
        state_spec = pl.BlockSpec((None,) + state_blk, lambda i, n: (l, i, 0, 0, 0))
    else:
        conv_spec = pl.BlockSpec(conv_blk, lambda i, n: (i, 0, 0))
        state_spec = pl.BlockSpec(state_blk, lambda i, n: (i, 0, 0, 0))
    return pl.pallas_call(
        functools.partial(_gdn_kernel, bt=bt, n_chunks=n_chunks, l_valid=l_valid, chunk=chunk),
        grid=(nb // bt, n_chunks),
        in_specs=[tok(3 * DN_W), tok(DN_W), tok(GATE_W), conv_spec, state_spec,
                  _layer_spec(cw, l), _layer_spec(alog4, l), _layer_spec(dtb4, l), _layer_spec(dnw4, l),
                  pl.BlockSpec((4, chunk, DN_W), const3),
                  pl.BlockSpec((chunk, chunk), const),
                  pl.BlockSpec((DN_W, DN_W), const), pl.BlockSpec((DN_W, DN_W), const),
                  pl.BlockSpec((DN_HEADS, DN_DK, DN_W), const3),
                  pl.BlockSpec((2, chunk, 128), const3),
                  pl.BlockSpec((GATE_W, 2 * DN_W), const)],
        out_specs=[tok(DN_W),
                   pl.BlockSpec((bt, CONV_W - 1, 3 * DN_W), lambda i, n: (i, 0, 0)),
                   pl.BlockSpec((bt, DN_HEADS, DN_DK, DN_DK), lambda i, n: (i, 0, 0, 0))],
        out_shape=[jax.ShapeDtypeStruct((nb, length, DN_W), BF16),
                   jax.ShapeDtypeStruct((nb, CONV_W - 1, 3 * DN_W), F32),
                   jax.ShapeDtypeStruct((nb, DN_HEADS, DN_DK, DN_DK), F32)],
        scratch_shapes=[pltpu.VMEM((bt, 8 + chunk, 3 * DN_W), F32),
                        pltpu.VMEM((bt, DN_W, DN_W), F32)],
        compiler_params=_cparams("arbitrary", "arbitrary"),
        name="gated_delta",
    )(d3, z3, g3, conv0, s0, cw, alog4, dtb4, dnw4, *_gdn_constants(chunk))


MXU_N = 256
FF_CHUNKS = ((0, 6 * MXU_N), (6 * MXU_N, D_FF))


def _mix_residual(x_ref, yp_ref, ya_ref, yd_ref, g1_ref, sh2_ref, sc2_ref, n2_ref, wo_ref):
    mix = jnp.concatenate([yp_ref[...].astype(BF16), ya_ref[...].astype(BF16), yd_ref[...].astype(BF16)],
                          axis=1)
    x1 = x_ref[...] + g1_ref[...] * jnp.dot(mix, wo_ref[...], preferred_element_type=F32)
    ms = jnp.mean(x1 * x1, axis=-1, keepdims=True)
    h2 = ((x1 * lax.rsqrt(ms + EPS) * n2_ref[...]) * (1.0 + sc2_ref[...]) + sh2_ref[...]).astype(BF16)
    return x1, h2


def _out_kernel(x_ref, yp_ref, ya_ref, yd_ref, g1_ref, sh2_ref, sc2_ref, g2_ref, n2_ref,
                wo_ref, wg_ref, wu_ref, wd_ref, o_ref):
    x1, h2 = _mix_residual(x_ref, yp_ref, ya_ref, yd_ref, g1_ref, sh2_ref, sc2_ref, n2_ref, wo_ref)
    acc = None
    for c0, c1 in FF_CHUNKS:
        gt = jnp.dot(h2, wg_ref[:, c0:c1], preferred_element_type=F32)
        up = jnp.dot(h2, wu_ref[:, c0:c1], preferred_element_type=F32)
        act = (gt * _sigmoid(gt) * up).astype(BF16)
        part = jnp.dot(act, wd_ref[c0:c1, :], preferred_element_type=F32)
        acc = part if acc is None else acc + part
    o_ref[...] = x1 + g2_ref[...] * acc


def _out_ffn(x3, yp, ya, yd, mod, l, per_token, n2, wo, wg, wu, wd, tm):
    nb, length, _ = x3.shape
    tok = lambda w: pl.BlockSpec((None, tm, w), lambda b, t: (b, t, 0))
    single = pl.Buffered(1)
    return pl.pallas_call(
        _out_kernel,
        grid=(nb, length // tm),
        in_specs=[tok(D_MODEL), tok(POOL_W), tok(ATT_W), tok(DN_W),
                  _mod_spec(mod, l, 2, per_token), _mod_spec(mod, l, 3, per_token),
                  _mod_spec(mod, l, 4, per_token), _mod_spec(mod, l, 5, per_token),
                  _layer_spec(n2, l),
                  _layer_spec(wo, l, pipeline_mode=single), _layer_spec(wg, l, pipeline_mode=single),
                  _layer_spec(wu, l, pipeline_mode=single), _layer_spec(wd, l, pipeline_mode=single)],
        out_specs=tok(D_MODEL),
        out_shape=jax.ShapeDtypeStruct((nb, length, D_MODEL), F32),
        compiler_params=_cparams("arbitrary", "arbitrary"),
        name="out_ffn",
    )(x3, yp, ya, yd, mod, mod, mod, mod, n2, wo, wg, wu, wd)


def kernel(x_prompt, x_sample, cache_k, cache_v, state_pool, state_conv, state_delta, page_table,
           c_prompt, c_sample, norm1_w, ada_w, ada_b, w_in, pool_w, pool_scale, q_norm_w, k_norm_w,
           conv_w, a_log, dt_bias, dn_norm_w, w_out, norm2_w, w_gate, w_up, w_down):
    depth = w_in.shape[0]
    nbp, seq, _ = x_prompt.shape
    nbs = x_sample.shape[0]
    past_len = page_table.shape[1] * PAGE

    w_ext = jnp.pad(w_in, ((0, 0), (0, 0), (0, IN_EXT - w_in.shape[2]))).astype(BF16)
    wo_b, wg_b, wu_b, wd_b = (w.astype(BF16) for w in (w_out, w_gate, w_up, w_down))
    eye_g = jnp.eye(len(POOL_WINDOWS), dtype=F32)
    pw_bd = (eye_g[None, :, None, :, None] * pool_w[:, :, :, None, :]).reshape(depth, POOL_W, POOL_W).astype(BF16)
    ones_att = jnp.kron(jnp.eye(ATT_HEADS, dtype=F32), jnp.ones((HEAD_DIM, HEAD_DIM), F32)).astype(BF16)
    qn4 = jnp.tile(q_norm_w, (1, ATT_HEADS)).reshape(depth, 1, ATT_W)
    kn4 = jnp.tile(k_norm_w, (1, ATT_HEADS)).reshape(depth, 1, ATT_W)
    dnw4 = jnp.tile(dn_norm_w, (1, DN_HEADS)).reshape(depth, 1, DN_W)
    alog4 = jnp.repeat(a_log, DN_DK, axis=-1).reshape(depth, 1, DN_W)
    dtb4 = jnp.repeat(dt_bias, DN_DK, axis=-1).reshape(depth, 1, DN_W)
    n1 = norm1_w.reshape(depth, 1, D_MODEL)
    n2 = norm2_w.reshape(depth, 1, D_MODEL)
    ps = pool_scale.reshape(depth, 1, POOL_W)
    cache_kt = jnp.transpose(cache_k, (0, 1, 3, 4, 2))
    cache_vt = jnp.transpose(cache_v, (0, 1, 3, 4, 2))

    mod = _modulation(jnp.concatenate([c_prompt, c_sample], axis=0), ada_w, ada_b)
    mod_p = mod[:, :nbp].reshape(depth, nbp, N_MOD, 1, D_MODEL)
    mod_s = mod[:, nbp:].reshape(depth, nbs, N_MOD, D_MODEL).transpose(0, 2, 1, 3)

    xp = x_prompt
    xs = x_sample.reshape(1, nbs, D_MODEL)
    zero_pool = jnp.zeros((nbp, POOL_BUF, POOL_W), F32)
    zero_pool_s = jnp.zeros((nbs, POOL_BUF, POOL_W), F32)
    zero_conv = jnp.zeros((nbp, CONV_W - 1, 3 * DN_W), F32)
    zero_state = jnp.zeros((nbp, DN_HEADS, DN_DK, DN_DK), F32)
    pad_rows = lambda a: jnp.pad(a.reshape(nbs, 1, a.shape[-1]), ((0, 0), (0, SAMPLE_CHUNK - 1), (0, 0)))

    outs = {k: [] for k in ("ks", "vs", "pp", "ps", "cp", "cs", "sp", "ss")}
    kv_all = None
    for l in range(depth):
        u, q, k_all, v_all, d, z, gates = _in_proj(xp, mod_p, l, False, n1, w_ext, qn4, kn4, ones_att,
                                                   2 * TM, kv_depth=depth, kv_prev=kv_all,
                                                   after=xs if l > 0 else None)
        kv_all = (k_all, v_all)
        y_pool, pool_new = _pool(u, zero_pool, pw_bd, ps, 0, l)
        y_att = _attn_prompt(q, k_all, v_all, l)
        y_dn, conv_new, s_new = _gdn(d, z, gates, zero_conv, zero_state, conv_w, alog4, dtb4, dnw4,
                                     l, seq, GDN_BT, CHUNK)
        xp = _out_ffn(xp, y_pool, y_att, y_dn, mod_p, l, False, n2, wo_b, wg_b, wu_b, wd_b, TM)
        outs["pp"].append(pool_new)
        outs["cp"].append(conv_new)
        outs["sp"].append(s_new)

        us, qs, ks, vs, ds, zs, gs = _in_proj(xs, mod_s, l, True, n1, w_ext, qn4, kn4, ones_att, nbs)
        ext = jnp.concatenate([state_pool[l], us.reshape(nbs, 1, POOL_W)], axis=1)
        y_pool16, pool_new = _pool(ext, zero_pool_s, pw_bd, ps, past_len - POOL_BUF, l, bs=8)
        y_pool = y_pool16[:, POOL_BUF:].reshape(1, nbs, POOL_W)
        o_att = _attn_sample(qs.reshape(nbs, ATT_HEADS, HEAD_DIM).astype(F32),
                             ks.reshape(nbs, ATT_HEADS, HEAD_DIM), vs.reshape(nbs, ATT_HEADS, HEAD_DIM),
                             cache_kt, cache_vt, page_table, l)
        y_att = o_att.reshape(1, nbs, ATT_W)
        y_dnc, conv_new, s_new = _gdn(pad_rows(ds), pad_rows(zs), pad_rows(gs),
                                      state_conv, state_delta, conv_w, alog4, dtb4, dnw4,
                                      l, 1, GDN_BT, SAMPLE_CHUNK, stacked_state=True)
        y_dn = y_dnc[:, 0].reshape(1, nbs, DN_W)
        xs = _out_ffn(xs, y_pool, y_att, y_dn, mod_s, l, True, n2, wo_b, wg_b, wu_b, wd_b, nbs)
        outs["ks"].append(ks.reshape(nbs, 1, ATT_HEADS, HEAD_DIM))
        outs["vs"].append(vs.reshape(nbs, 1, ATT_HEADS, HEAD_DIM))
        outs["ps"].append(pool_new)
        outs["cs"].append(conv_new)
        outs["ss"].append(s_new)

    st = lambda name: jnp.stack(outs[name])
    k_prompt, v_prompt = (a.reshape(depth, nbp, seq, ATT_HEADS, HEAD_DIM) for a in kv_all)
    return (xp, xs.reshape(nbs, 1, D_MODEL), k_prompt, v_prompt, st("ks"), st("vs"),
            st("pp"), st("ps"), st("cp"), st("cs"), st("sp"), st("ss"))
```
